```python
import math
import numpy as np
import jax
import jax.numpy as jnp
from jax import lax

D_MODEL = 1024
BATCH = 2
SEQ = 16384
DEPTH = 2
DEC_BATCH = 4
DEC_SEQ = 4096
PAST_LEN = 128

GLA_HEADS = 4
GLA_DK = 128
GLA_DV = 128
GLA_KW = GLA_HEADS * GLA_DK
GLA_WIDTH = GLA_HEADS * GLA_DV
GLA_RANK = 16
GLA_TAU = 16.0
GLA_CHUNK = 64
SWA_HEADS = 8
SWA_KV_HEADS = 2
SWA_GROUP = SWA_HEADS // SWA_KV_HEADS
SWA_HD = 64
SWA_WIDTH = SWA_HEADS * SWA_HD
SWA_KVW = SWA_KV_HEADS * SWA_HD
WINDOW = 128
BLOCK = 128
REL_BUCKETS = 32
REL_MAX_DIST = 128
CONV_WIDTH = D_MODEL
CONV_K = 3
DN_ALPHA = (2 * DEPTH) ** 0.25
DN_BETA = (8 * DEPTH) ** -0.25
LN_EPS = 1e-5
NORM_EPS = 1e-6
N_EVEN = (DEPTH + 1) // 2
N_ODD = DEPTH // 2
EVEN_SPLITS = (GLA_KW, GLA_KW, GLA_WIDTH, GLA_WIDTH, 2 * GLA_RANK, SWA_WIDTH, SWA_KVW, SWA_KVW, SWA_WIDTH)
EVEN_IN = sum(EVEN_SPLITS)
EVEN_MIX = GLA_WIDTH + SWA_WIDTH
ODD_IN = 4 * CONV_WIDTH

kernel_name = "hybrid_gla_swa_shortconv_encoder"


def _offsets(splits):
    out, acc = [], 0
    for s in splits[:-1]:
        acc += s
        out.append(acc)
    return out


def layer_norm(x, g, b):
    xf = x.astype(jnp.float32)
    mu = jnp.mean(xf, -1, keepdims=True)
    var = jnp.mean(jnp.square(xf - mu), -1, keepdims=True)
    y = (xf - mu) * lax.rsqrt(var + LN_EPS) * g.astype(jnp.float32) + b.astype(jnp.float32)
    return y.astype(x.dtype)


def _gla_scan(q, k, v, logd, strict):
    Bn, H, L, dk = q.shape
    dv = v.shape[-1]
    n = L // GLA_CHUNK

    def chunks(t):
        return t.reshape(Bn, H, n, GLA_CHUNK, t.shape[-1]).transpose(2, 0, 1, 3, 4)

    qc, kc, vc, gc = chunks(q), chunks(k), chunks(v), chunks(logd)
    idx = jnp.arange(GLA_CHUNK)
    mask = (idx[None, :] < idx[:, None]) if strict else (idx[None, :] <= idx[:, None])

    def step(S, inp):
        qi, ki, vi, gi = inp
        b = jnp.cumsum(gi, axis=2)
        inter = jnp.einsum('bhcd,bhde->bhce', qi * jnp.exp(b), S)
        diff = b[:, :, :, None, :] - b[:, :, None, :, :]
        decay = jnp.exp(jnp.where(mask[:, :, None], diff, -jnp.inf))
        scores = jnp.sum(qi[:, :, :, None, :] * ki[:, :, None, :, :] * decay, axis=-1)
        intra = jnp.einsum('bhij,bhje->bhie', scores, vi)
        b_last = b[:, :, -1:, :]
        S_new = jnp.exp(b_last[:, :, 0, :])[..., None] * S + jnp.einsum(
            'bhcd,bhce->bhde', ki * jnp.exp(b_last - b), vi)
        return S_new, inter + intra

    S0 = jnp.zeros((Bn, H, dk, dv), jnp.float32)
    _, out = lax.scan(step, S0, (qc, kc, vc, gc))
    return out.transpose(1, 2, 0, 3, 4).reshape(Bn, H, L, dv)


def gla_mixer(q, k, v, z, gdown, w_up_f, b_f, w_up_b, b_b, norm_g):
    Bn, L, _ = q.shape
    f32 = jnp.float32

    def heads(t, d):
        return t.astype(f32).reshape(Bn, L, GLA_HEADS, d).transpose(0, 2, 1, 3)

    qh = heads(q, GLA_DK) * (GLA_DK ** -0.5)
    kh = heads(k, GLA_DK)
    vh = heads(v, GLA_DV)
    gd_f, gd_b = gdown[..., :GLA_RANK], gdown[..., GLA_RANK:]
    logd_f = heads(jax.nn.log_sigmoid((gd_f @ w_up_f + b_f).astype(f32)) / GLA_TAU, GLA_DK)
    logd_b = heads(jax.nn.log_sigmoid((gd_b @ w_up_b + b_b).astype(f32)) / GLA_TAU, GLA_DK)
    o_f = _gla_scan(qh, kh, vh, logd_f, strict=False)
    flip = lambda t: t[:, :, ::-1]
    o_b = flip(_gla_scan(flip(qh), flip(kh), flip(vh), flip(logd_b), strict=True))
    o = o_f + o_b
    o = o * lax.rsqrt(jnp.mean(jnp.square(o), -1, keepdims=True) + NORM_EPS) * norm_g.astype(f32)
    o = o.transpose(0, 2, 1, 3).reshape(Bn, L, GLA_WIDTH).astype(v.dtype)
    return o * jax.nn.silu(z)


def _rel_buckets():
    i = np.arange(BLOCK)[:, None]
    j = np.arange(3 * BLOCK)[None, :]
    rel = j - BLOCK - i
    half = REL_BUCKETS // 2
    max_exact = half // 2
    n = np.abs(rel)
    large = max_exact + (np.log(np.maximum(n, 1) / max_exact) / np.log(REL_MAX_DIST / max_exact)
                         * (half - max_exact)).astype(np.int32)
    large = np.minimum(large, half - 1)
    bucket = (rel > 0).astype(np.int32) * half + np.where(n < max_exact, n, large)
    return bucket.astype(np.int32), rel


def swa_mixer(q, k, v, z, rel_bias, sink):
    Bn, L, _ = q.shape
    nb = L // BLOCK
    qh = q.reshape(Bn, nb, BLOCK, SWA_KV_HEADS, SWA_GROUP, SWA_HD) * (SWA_HD ** -0.5)

    def banded(t):
        tp = jnp.pad(t.reshape(Bn, L, SWA_KV_HEADS, SWA_HD), ((0, 0), (BLOCK, BLOCK), (0, 0), (0, 0)))
        tp = tp.reshape(Bn, nb + 2, BLOCK, SWA_KV_HEADS, SWA_HD)
        return jnp.concatenate([tp[:, :-2], tp[:, 1:-1], tp[:, 2:]], axis=2)

    kb, vb = banded(k), banded(v)
    bucket, rel = _rel_buckets()
    key_pos = np.arange(nb)[:, None] * BLOCK - BLOCK + np.arange(3 * BLOCK)[None, :]
    valid = (key_pos >= 0) & (key_pos < L)
    mask = jnp.asarray((np.abs(rel) <= WINDOW)[None] & valid[:, None, :])[:, None, None]
    bias = rel_bias.astype(jnp.float32)[bucket]
    bias = bias.transpose(2, 0, 1).reshape(SWA_KV_HEADS, SWA_GROUP, BLOCK, 3 * BLOCK)

    s = jnp.einsum('bnqkgd,bnskd->bnkgqs', qh, kb).astype(jnp.float32) + bias
    s = jnp.where(mask, s, -jnp.inf)
    sk = sink.astype(jnp.float32).reshape(SWA_KV_HEADS, SWA_GROUP)[:, :, None, None]
    m = jnp.maximum(jnp.max(s, -1, keepdims=True), sk)
    e = jnp.exp(s - m)
    p = e / (jnp.sum(e, -1, keepdims=True) + jnp.exp(sk - m))
    o = jnp.einsum('bnkgqs,bnskd->bnqkgd', p.astype(v.dtype), vb).reshape(Bn, L, SWA_WIDTH)
    return o * jax.nn.silu(z)


def even_sublayer(x, w_in, w_up_f, b_f, w_up_b, b_b, norm_g, sink, rel_bias, w_out):
    u = x @ w_in
    qa, ka, va, za, gd, qb, kb, vb, zb = jnp.split(u, _offsets(EVEN_SPLITS), axis=-1)
    ya = gla_mixer(qa, ka, va, za, gd, w_up_f, b_f, w_up_b, b_b, norm_g)
    yb = swa_mixer(qb, kb, vb, zb, rel_bias, sink)
    return jnp.concatenate([ya, yb], axis=-1) @ w_out


def odd_sublayer(x, w_in, conv_w, w_out):
    u = x @ w_in
    bg, cg, h, z = jnp.split(u, 4, axis=-1)
    t = jnp.pad(cg * h, ((0, 0), (1, 1), (0, 0)))
    conv = conv_w[0] * t[:, :-2] + conv_w[1] * t[:, 1:-1] + conv_w[2] * t[:, 2:]
    return (jax.nn.silu(z) * bg * conv) @ w_out


def trunk(x, w_in_even, gla_w_up_fwd, gla_b_fwd, gla_w_up_bwd, gla_b_bwd, gla_norm_g, swa_sink,
          rel_bias, w_out_even, w_in_odd, conv_w, w_out_odd, ln_g, ln_b):
    for l in range(DEPTH):
        i = l // 2
        if l % 2 == 0:
            sub = even_sublayer(x, w_in_even[i], gla_w_up_fwd[i], gla_b_fwd[i], gla_w_up_bwd[i], gla_b_bwd[i],
                                gla_norm_g[i], swa_sink[i], rel_bias, w_out_even[i])
        else:
            sub = odd_sublayer(x, w_in_odd[i], conv_w[i], w_out_odd[i])
        x = layer_norm(DN_ALPHA * x + sub, ln_g[l], ln_b[l])
    return x


def setup_inputs(seed: int = 0) -> dict:
    key = jax.random.key(seed)
    ks = jax.random.split(key, 20)
    nrm = lambda k, shape, s: jax.random.normal(k, shape, jnp.float32) * s
    return {
        "x_prompt": nrm(ks[0], (BATCH, SEQ, D_MODEL), 1.0),
        "x_sample": nrm(ks[1], (DEC_BATCH, DEC_SEQ, D_MODEL), 1.0),
        "w_in_even": nrm(ks[2], (N_EVEN, D_MODEL, EVEN_IN), D_MODEL ** -0.5),
        "gla_w_up_fwd": nrm(ks[3], (N_EVEN, GLA_RANK, GLA_KW), GLA_RANK ** -0.5),
        "gla_b_fwd": nrm(ks[4], (N_EVEN, GLA_KW), 0.01),
        "gla_w_up_bwd": nrm(ks[5], (N_EVEN, GLA_RANK, GLA_KW), GLA_RANK ** -0.5),
        "gla_b_bwd": nrm(ks[6], (N_EVEN, GLA_KW), 0.01),
        "gla_norm_g": 1.0 + nrm(ks[7], (N_EVEN, GLA_DV), 0.01),
        "swa_sink": nrm(ks[8], (N_EVEN, SWA_HEADS), 0.5),
        "rel_bias": nrm(ks[9], (REL_BUCKETS, SWA_HEADS), 0.5),
        "w_out_even": nrm(ks[10], (N_EVEN, EVEN_MIX, D_MODEL), EVEN_MIX ** -0.5 * DN_BETA),
        "w_in_odd": nrm(ks[11], (N_ODD, D_MODEL, ODD_IN), D_MODEL ** -0.5),
        "conv_w": nrm(ks[12], (N_ODD, CONV_K, CONV_WIDTH), CONV_K ** -0.5),
        "w_out_odd": nrm(ks[13], (N_ODD, CONV_WIDTH, D_MODEL), CONV_WIDTH ** -0.5 * DN_BETA),
        "ln_g": 1.0 + nrm(ks[14], (DEPTH, D_MODEL), 0.01),
        "ln_b": nrm(ks[15], (DEPTH, D_MODEL), 0.01),
    }


def reference(x_prompt, x_sample, w_in_even, gla_w_up_fwd, gla_b_fwd, gla_w_up_bwd, gla_b_bwd, gla_norm_g,
              swa_sink, rel_bias, w_out_even, w_in_odd, conv_w, w_out_odd, ln_g, ln_b):
    y_prompt = trunk(x_prompt, w_in_even, gla_w_up_fwd, gla_b_fwd, gla_w_up_bwd, gla_b_bwd, gla_norm_g,
                     swa_sink, rel_bias, w_out_even, w_in_odd, conv_w, w_out_odd, ln_g, ln_b)
    y_sample = trunk(x_sample, w_in_even, gla_w_up_fwd, gla_b_fwd, gla_w_up_bwd, gla_b_bwd, gla_norm_g,
                     swa_sink, rel_bias, w_out_even, w_in_odd, conv_w, w_out_odd, ln_g, ln_b)
    return (y_prompt, y_sample)
```

```python
import functools
import math

import numpy as np
import jax
import jax.numpy as jnp
from jax import lax
from jax.experimental import pallas as pl
from jax.experimental.pallas import tpu as pltpu

F32 = jnp.float32
BF16 = jnp.bfloat16

D_MODEL = 1024
DEPTH = 2
GLA_HEADS = 4
GLA_DK = 128
GLA_WIDTH = 512
GLA_RANK = 16
GLA_TAU = 16.0
SWA_HEADS = 8
SWA_KV_HEADS = 2
SWA_HD = 64
SWA_WIDTH = 512
SWA_KVW = 128
WINDOW = 128
BLOCK = 128
REL_BUCKETS = 32
REL_MAX_DIST = 128
CONV_WIDTH = 1024
DN_ALPHA = (2 * DEPTH) ** 0.25
LN_EPS = 1e-5
NORM_EPS = 1e-6
NEG_BIG = -1e30

GLA_COLS = 4 * GLA_WIDTH
SWA_COLS = 2 * SWA_WIDTH + 2 * SWA_KVW
GD_COLS = 2 * GLA_RANK
GD_PAD = 128
EVEN_COLS = GLA_COLS + SWA_COLS + GD_PAD

GLA_CHUNK = 128
TOT_ROWS = 16
VMEM_LIMIT = 56 * 1024 * 1024


def _dot(a, b):
    return jnp.dot(a, b, preferred_element_type=F32)


def _dot_nt(a, b):
    return lax.dot_general(a, b, (((1,), (1,)), ((), ())), preferred_element_type=F32)


def _dot_tn(a, b):
    return lax.dot_general(a, b, (((0,), (0,)), ((), ())), preferred_element_type=F32)


def _silu(z):
    return z / (1.0 + jnp.exp(-z))


def _layer_norm(y, g, b):
    mu = jnp.mean(y, axis=-1, keepdims=True)
    yc = y - mu
    var = jnp.mean(yc * yc, axis=-1, keepdims=True)
    return yc * lax.rsqrt(var + LN_EPS) * g + b


def _params(*sem):
    return pltpu.CompilerParams(dimension_semantics=sem, vmem_limit_bytes=VMEM_LIMIT)


def _inproj_even_kernel(x_ref, w_ref, gla_ref, swa_ref, gd_ref):
    xb = x_ref[0].astype(BF16)
    for c0 in range(0, GLA_COLS, 512):
        gla_ref[0, :, c0:c0 + 512] = _dot(xb, w_ref[:, c0:c0 + 512]).astype(BF16)
    for c0 in range(0, SWA_COLS, 256):
        swa_ref[0, :, c0:c0 + 256] = _dot(xb, w_ref[:, GLA_COLS + c0:GLA_COLS + c0 + 256]).astype(BF16)
    gd = _dot(xb, w_ref[:, GLA_COLS + SWA_COLS:])
    gd_ref[0] = gd[:, :GD_COLS]


def _inproj_even(x, w, tm):
    bn, ln, _ = x.shape
    return pl.pallas_call(
        _inproj_even_kernel,
        grid=(bn, ln // tm),
        in_specs=[
            pl.BlockSpec((1, tm, D_MODEL), lambda b, t: (b, t, 0)),
            pl.BlockSpec((D_MODEL, EVEN_COLS), lambda b, t: (0, 0)),
        ],
        out_specs=[
            pl.BlockSpec((1, tm, GLA_COLS), lambda b, t: (b, t, 0)),
            pl.BlockSpec((1, tm, SWA_COLS), lambda b, t: (b, t, 0)),
            pl.BlockSpec((1, tm, GD_COLS), lambda b, t: (b, t, 0)),
        ],
        out_shape=[
            jax.ShapeDtypeStruct((bn, ln, GLA_COLS), BF16),
            jax.ShapeDtypeStruct((bn, ln, SWA_COLS), BF16),
            jax.ShapeDtypeStruct((bn, ln, GD_COLS), F32),
        ],
        compiler_params=_params("parallel", "parallel"),
        name="inproj_even",
    )(x, w)


def _gla_kernel(q_ref, k_ref, v_ref, gd_ref, wup_ref, bias_ref, tri_ref, mask_ref, o_ref, st_ref, logd_ref, *, tile):
    d = pl.program_id(0)
    t = pl.program_id(2)
    nchunk = tile // GLA_CHUNK

    @pl.when(t == 0)
    def _():
        st_ref[...] = jnp.zeros_like(st_ref)

    a = _dot(gd_ref[0].astype(BF16), wup_ref[0]) + bias_ref[0]
    logd_ref[...] = (jnp.minimum(a, 0.0) - jnp.log(1.0 + jnp.exp(-jnp.abs(a)))) * (1.0 / GLA_TAU)

    tri = tri_ref[0]
    keep = mask_ref[0] > 0.0

    def chunk(c, carry):
        cc = c + d * (nchunk - 1 - 2 * c)
        rows = pl.ds(pl.multiple_of(cc * GLA_CHUNK, GLA_CHUNK), GLA_CHUNK)
        for h in range(GLA_HEADS):
            hs = slice(h * GLA_DK, (h + 1) * GLA_DK)
            ld = logd_ref[rows, hs]
            hi = ld.astype(BF16)
            lo = (ld - hi.astype(F32)).astype(BF16)
            res = _dot(tri, jnp.concatenate([hi, lo], axis=0))
            b = res[:GLA_CHUNK]
            etot = jnp.exp(res[GLA_CHUNK:GLA_CHUNK + 1])
            qt = (q_ref[0, rows, hs].astype(F32) * jnp.exp(b)).astype(BF16)
            kt = k_ref[0, rows, hs].astype(F32) * jnp.exp(-b)
            kd = (kt * etot).astype(BF16)
            v = v_ref[0, rows, hs]
            s = jnp.where(keep, _dot_nt(qt, kt.astype(BF16)), 0.0).astype(BF16)
            st = st_ref[h]
            o_ref[0, 0, rows, hs] = _dot(s, v) + _dot_nt(qt, st.astype(BF16))
            st_ref[h] = st * etot + _dot_tn(v, kd)
        return carry

    lax.fori_loop(0, nchunk, chunk, 0)


def _gla(gla, gd, wup, bias, tri, mask, tile):
    bn, ln, _ = gla.shape
    nt = ln // tile

    def tok(d, b, t):
        return t + d * (nt - 1 - 2 * t)

    return pl.pallas_call(
        functools.partial(_gla_kernel, tile=tile),
        grid=(2, bn, nt),
        in_specs=[
            pl.BlockSpec((1, tile, GLA_WIDTH), lambda d, b, t: (b, tok(d, b, t), 0)),
            pl.BlockSpec((1, tile, GLA_WIDTH), lambda d, b, t: (b, tok(d, b, t), 1)),
            pl.BlockSpec((1, tile, GLA_WIDTH), lambda d, b, t: (b, tok(d, b, t), 2)),
            pl.BlockSpec((1, tile, GD_COLS), lambda d, b, t: (b, tok(d, b, t), 0)),
            pl.BlockSpec((1, GD_COLS, GLA_WIDTH), lambda d, b, t: (d, 0, 0)),
            pl.BlockSpec((1, 1, GLA_WIDTH), lambda d, b, t: (d, 0, 0)),
            pl.BlockSpec((1, GLA_CHUNK + TOT_ROWS, 2 * GLA_CHUNK), lambda d, b, t: (d, 0, 0)),
            pl.BlockSpec((1, GLA_CHUNK, GLA_CHUNK), lambda d, b, t: (d, 0, 0)),
        ],
        out_specs=pl.BlockSpec((1, 1, tile, GLA_WIDTH), lambda d, b, t: (d, b, tok(d, b, t), 0)),
        out_shape=jax.ShapeDtypeStruct((2, bn, ln, GLA_WIDTH), F32),
        scratch_shapes=[
            pltpu.VMEM((GLA_HEADS, GLA_DK, GLA_DK), F32),
            pltpu.VMEM((tile, GLA_WIDTH), F32),
        ],
        compiler_params=_params("arbitrary", "arbitrary", "arbitrary"),
        name="gla_scan",
    )(gla, gla, gla, gd, wup, bias, tri, mask)


def _gla_constants():
    i = np.arange(GLA_CHUNK)[:, None]
    j = np.arange(GLA_CHUNK)[None, :]
    lower = (j <= i).astype(np.float32)
    upper = (j >= i).astype(np.float32)
    tri = np.zeros((2, GLA_CHUNK + TOT_ROWS, 2 * GLA_CHUNK), np.float32)
    tri[0, :GLA_CHUNK] = np.concatenate([lower, lower], axis=1)
    tri[1, :GLA_CHUNK] = np.concatenate([upper, upper], axis=1)
    tri[:, GLA_CHUNK:] = 1.0
    mask = np.stack([(j <= i), (j > i)]).astype(np.float32)
    return jnp.asarray(tri, BF16), jnp.asarray(mask, F32)


def _rel_tables():
    i = np.arange(BLOCK)[:, None]
    j = np.arange(3 * BLOCK)[None, :]
    rel = j - BLOCK - i
    half = REL_BUCKETS // 2
    max_exact = half // 2
    n = np.abs(rel)
    large = max_exact + (np.log(np.maximum(n, 1) / max_exact) / np.log(REL_MAX_DIST / max_exact)
                         * (half - max_exact)).astype(np.int32)
    large = np.minimum(large, half - 1)
    bucket = (rel > 0).astype(np.int32) * half + np.where(n < max_exact, n, large)
    band = np.abs(rel) <= WINDOW
    col = np.broadcast_to(j, rel.shape)
    valid = np.stack([band & (col >= BLOCK), band, band & (col < 2 * BLOCK)])
    return bucket.astype(np.int32), valid.astype(np.int32)


def _bias_kernel(rb_ref, bucket_ref, valid_ref, out_ref):
    h = pl.program_id(1)
    bucket = bucket_ref[...]
    acc = jnp.zeros(bucket.shape, F32)
    for kk in range(REL_BUCKETS):
        acc = jnp.where(bucket == kk, rb_ref[kk, h], acc)
    out_ref[0, 0] = jnp.where(valid_ref[0] > 0, acc, NEG_BIG)


def _bias_table(rel_bias):
    bucket, valid = _rel_tables()
    return pl.pallas_call(
        _bias_kernel,
        grid=(3, SWA_HEADS),
        in_specs=[
            pl.BlockSpec(memory_space=pltpu.SMEM),
            pl.BlockSpec((BLOCK, 3 * BLOCK), lambda e, h: (0, 0)),
            pl.BlockSpec((1, BLOCK, 3 * BLOCK), lambda e, h: (e, 0, 0)),
        ],
        out_specs=pl.BlockSpec((1, 1, BLOCK, 3 * BLOCK), lambda e, h: (e, h, 0, 0)),
        out_shape=jax.ShapeDtypeStruct((3, SWA_HEADS, BLOCK, 3 * BLOCK), F32),
        compiler_params=_params("arbitrary", "arbitrary"),
        name="swa_bias_table",
    )(rel_bias.astype(F32), jnp.asarray(bucket), jnp.asarray(valid))


def _swa_kernel(sink_ref, q_ref, z_ref, kp_ref, kc_ref, kn_ref, vp_ref, vc_ref, vn_ref, bias_ref, o_ref):
    kcat = jnp.concatenate([kp_ref[0], kc_ref[0], kn_ref[0]], axis=0)
    vcat = jnp.concatenate([vp_ref[0], vc_ref[0], vn_ref[0]], axis=0)
    kswap = jnp.concatenate([kcat[:, SWA_HD:], kcat[:, :SWA_HD]], axis=1)
    vswap = jnp.concatenate([vcat[:, SWA_HD:], vcat[:, :SWA_HD]], axis=1)
    lane = lax.broadcasted_iota(jnp.int32, kcat.shape, 1)
    low = lane < SWA_HD
    zero = jnp.zeros_like(kcat)
    kmat = {(0, 0): jnp.where(low, kcat, zero), (0, 1): jnp.where(low, zero, kswap),
            (1, 0): jnp.where(low, kswap, zero), (1, 1): jnp.where(low, zero, kcat)}
    vmat = {(0, 0): jnp.where(low, vcat, zero), (0, 1): jnp.where(low, zero, vswap),
            (1, 0): jnp.where(low, vswap, zero), (1, 1): jnp.where(low, zero, vcat)}
    for p in range(SWA_HEADS // 2):
        g = p // (SWA_HEADS // SWA_KV_HEADS // 2)
        ps = slice(p * 128, (p + 1) * 128)
        qp = q_ref[0, :, ps]
        acc = None
        for e in range(2):
            h = 2 * p + e
            sink = sink_ref[0, h]
            s = _dot_nt(qp, kmat[(g, e)]) + bias_ref[0, h]
            m = jnp.maximum(jnp.max(s, axis=-1, keepdims=True), sink)
            ex = jnp.exp(s - m)
            denom = jnp.sum(ex, axis=-1, keepdims=True) + jnp.exp(sink - m)
            oe = _dot(ex.astype(BF16), vmat[(g, e)]) / denom
            acc = oe if acc is None else acc + oe
        o_ref[0, :, ps] = (acc * _silu(z_ref[0, :, ps].astype(F32))).astype(BF16)


def _swa(swa, bias, sink):
    bn, ln, _ = swa.shape
    nb = ln // BLOCK
    assert nb >= 2
    kcol = 2 * SWA_WIDTH // SWA_KVW
    vcol = kcol + 1

    def edge(n):
        return jnp.where(n == 0, 0, jnp.where(n == nb - 1, 2, 1))

    kv = lambda col, off: pl.BlockSpec(
        (1, BLOCK, SWA_KVW), lambda b, n: (b, jnp.clip(n + off, 0, nb - 1), col))
    return pl.pallas_call(
        _swa_kernel,
        grid=(bn, nb),
        in_specs=[
            pl.BlockSpec(memory_space=pltpu.SMEM),
            pl.BlockSpec((1, BLOCK, SWA_WIDTH), lambda b, n: (b, n, 0)),
            pl.BlockSpec((1, BLOCK, SWA_WIDTH), lambda b, n: (b, n, 1)),
            kv(kcol, -1), kv(kcol, 0), kv(kcol, 1),
            kv(vcol, -1), kv(vcol, 0), kv(vcol, 1),
            pl.BlockSpec((1, SWA_HEADS, BLOCK, 3 * BLOCK), lambda b, n: (edge(n), 0, 0, 0)),
        ],
        out_specs=pl.BlockSpec((1, BLOCK, SWA_WIDTH), lambda b, n: (b, n, 0)),
        out_shape=jax.ShapeDtypeStruct((bn, ln, SWA_WIDTH), BF16),
        compiler_params=_params("parallel", "arbitrary"),
        name="swa_attention",
    )(sink.reshape(1, SWA_HEADS).astype(F32), swa, swa, swa, swa, swa, swa, swa, swa, bias)


def _outproj_even_kernel(of_ref, ob_ref, z_ref, yb_ref, x_ref, wa_ref, wb_ref, ng_ref, lg_ref, lb_ref, out_ref):
    o = of_ref[0, 0] + ob_ref[0, 0]
    parts = []
    for h in range(GLA_HEADS):
        oh = o[:, h * GLA_DK:(h + 1) * GLA_DK]
        parts.append(oh * lax.rsqrt(jnp.mean(oh * oh, axis=-1, keepdims=True) + NORM_EPS))
    on = jnp.concatenate(parts, axis=1) * ng_ref[...]
    ya = (on * _silu(z_ref[0].astype(F32))).astype(BF16)
    sub = _dot(ya, wa_ref[...]) + _dot(yb_ref[0], wb_ref[...])
    out_ref[0] = _layer_norm(DN_ALPHA * x_ref[0] + sub, lg_ref[...], lb_ref[...])


def _outproj_even(o, gla, yb, x, wa, wb, ng, lg, lb, tm):
    bn, ln, _ = x.shape
    row = lambda cols: pl.BlockSpec((1, cols), lambda b, t: (0, 0))
    return pl.pallas_call(
        _outproj_even_kernel,
        grid=(bn, ln // tm),
        in_specs=[
            pl.BlockSpec((1, 1, tm, GLA_WIDTH), lambda b, t: (0, b, t, 0)),
            pl.BlockSpec((1, 1, tm, GLA_WIDTH), lambda b, t: (1, b, t, 0)),
            pl.BlockSpec((1, tm, GLA_WIDTH), lambda b, t: (b, t, 3)),
            pl.BlockSpec((1, tm, SWA_WIDTH), lambda b, t: (b, t, 0)),
            pl.BlockSpec((1, tm, D_MODEL), lambda b, t: (b, t, 0)),
            pl.BlockSpec((GLA_WIDTH, D_MODEL), lambda b, t: (0, 0)),
            pl.BlockSpec((SWA_WIDTH, D_MODEL), lambda b, t: (0, 0)),
            row(GLA_WIDTH), row(D_MODEL), row(D_MODEL),
        ],
        out_specs=pl.BlockSpec((1, tm, D_MODEL), lambda b, t: (b, t, 0)),
        out_shape=jax.ShapeDtypeStruct((bn, ln, D_MODEL), F32),
        compiler_params=_params("parallel", "parallel"),
        name="outproj_even",
    )(o, o, gla, yb, x, wa, wb, ng, lg, lb)


ODD_CHUNK = 256
HALO = 8


def _odd_kernel(xp_ref, x_ref, xn_ref, win_ref, cw_ref, wout_ref, lg_ref, lb_ref, out_ref, *, tm):
    t = pl.program_id(1)
    nt = pl.num_programs(1)
    x = x_ref[0]
    xcat = jnp.concatenate([xp_ref[0].astype(BF16), x.astype(BF16), xn_ref[0].astype(BF16)], axis=0)
    rows = tm + 2 * HALO
    ridx = lax.broadcasted_iota(jnp.int32, (rows, 1), 0)
    pad_row = ((ridx == HALO - 1) & (t == 0)) | ((ridx == HALO + tm) & (t == nt - 1))
    acc = jnp.zeros((tm, D_MODEL), F32)
    for j in range(CONV_WIDTH // ODD_CHUNK):
        u = _dot(xcat, win_ref[j])
        bg = u[HALO:HALO + tm, :ODD_CHUNK]
        z = u[HALO:HALO + tm, 3 * ODD_CHUNK:]
        th = jnp.where(pad_row, 0.0, u[:, ODD_CHUNK:2 * ODD_CHUNK] * u[:, 2 * ODD_CHUNK:3 * ODD_CHUNK])
        cw = cw_ref[j]
        conv = (cw[0:1] * pltpu.roll(th, 1, axis=0)[HALO:HALO + tm]
                + cw[1:2] * th[HALO:HALO + tm]
                + cw[2:3] * pltpu.roll(th, rows - 1, axis=0)[HALO:HALO + tm])
        mixed = (_silu(z) * bg * conv).astype(BF16)
        acc = acc + _dot(mixed, wout_ref[j])
    out_ref[0] = _layer_norm(DN_ALPHA * x + acc, lg_ref[...], lb_ref[...])


def _odd_layer(x, win, cw, wout, lg, lb, tm):
    bn, ln, _ = x.shape
    nj = CONV_WIDTH // ODD_CHUNK
    hb = tm // HALO
    nh = ln // HALO
    row = lambda cols: pl.BlockSpec((1, cols), lambda b, t: (0, 0))
    return pl.pallas_call(
        functools.partial(_odd_kernel, tm=tm),
        grid=(bn, ln // tm),
        in_specs=[
            pl.BlockSpec((1, HALO, D_MODEL), lambda b, t: (b, jnp.maximum(t * hb - 1, 0), 0)),
            pl.BlockSpec((1, tm, D_MODEL), lambda b, t: (b, t, 0)),
            pl.BlockSpec((1, HALO, D_MODEL), lambda b, t: (b, jnp.minimum((t + 1) * hb, nh - 1), 0)),
            pl.BlockSpec((nj, D_MODEL, 4 * ODD_CHUNK), lambda b, t: (0, 0, 0)),
            pl.BlockSpec((nj, 3, ODD_CHUNK), lambda b, t: (0, 0, 0)),
            pl.BlockSpec((nj, ODD_CHUNK, D_MODEL), lambda b, t: (0, 0, 0)),
            row(D_MODEL), row(D_MODEL),
        ],
        out_specs=pl.BlockSpec((1, tm, D_MODEL), lambda b, t: (b, t, 0)),
        out_shape=jax.ShapeDtypeStruct((bn, ln, D_MODEL), F32),
        compiler_params=_params("parallel", "arbitrary"),
        name="odd_layer",
    )(x, x, x, win, cw, wout, lg, lb)


def _prep_even(w_in, w_up_f, b_f, w_up_b, b_b, norm_g, w_out):
    qa, ka, va, za, gd, qb, kb, vb, zb = jnp.split(
        w_in, np.cumsum([512, 512, 512, 512, GD_COLS, 512, 128, 128])[:].tolist(), axis=1)
    w = jnp.concatenate([
        qa * (GLA_DK ** -0.5), ka, va, za,
        qb * (SWA_HD ** -0.5), zb, kb, vb,
        gd, jnp.zeros((D_MODEL, GD_PAD - GD_COLS), w_in.dtype)], axis=1).astype(BF16)
    zr = jnp.zeros_like(w_up_f)
    wup = jnp.stack([jnp.concatenate([w_up_f, zr], axis=0), jnp.concatenate([zr, w_up_b], axis=0)]).astype(BF16)
    bias = jnp.stack([b_f, b_b]).reshape(2, 1, GLA_WIDTH).astype(F32)
    ng = jnp.tile(norm_g.astype(F32), GLA_HEADS).reshape(1, GLA_WIDTH)
    wa = w_out[:GLA_WIDTH].astype(BF16)
    wb = w_out[GLA_WIDTH:].astype(BF16)
    return w, wup, bias, ng, wa, wb


def _prep_odd(w_in, conv_w, w_out):
    nj = CONV_WIDTH // ODD_CHUNK
    win = w_in.reshape(D_MODEL, 4, nj, ODD_CHUNK).transpose(2, 0, 1, 3).reshape(nj, D_MODEL, 4 * ODD_CHUNK).astype(BF16)
    cw = conv_w.reshape(3, nj, ODD_CHUNK).transpose(1, 0, 2).astype(F32)
    wout = w_out.reshape(nj, ODD_CHUNK, D_MODEL).astype(BF16)
    return win, cw, wout


def _trunk(x, even, odd, bias_tab, sink, ln_g, ln_b, consts, tm=512, gla_tile=1024):
    w, wup, bias, ng, wa, wb = even
    win, cw, wout = odd
    tri, mask = consts
    gla, swa, gd = _inproj_even(x, w, tm)
    o = _gla(gla, gd, wup, bias, tri, mask, gla_tile)
    yb = _swa(swa, bias_tab, sink)
    lg = ln_g.astype(F32).reshape(DEPTH, 1, D_MODEL)
    lb = ln_b.astype(F32).reshape(DEPTH, 1, D_MODEL)
    x1 = _outproj_even(o, gla, yb, x, wa, wb, ng, lg[0], lb[0], tm)
    return _odd_layer(x1, win, cw, wout, lg[1], lb[1], tm)


def kernel(x_prompt, x_sample, w_in_even, gla_w_up_fwd, gla_b_fwd, gla_w_up_bwd, gla_b_bwd, gla_norm_g, swa_sink,
           rel_bias, w_out_even, w_in_odd, conv_w, w_out_odd, ln_g, ln_b):
    even = _prep_even(w_in_even[0], gla_w_up_fwd[0], gla_b_fwd[0], gla_w_up_bwd[0], gla_b_bwd[0], gla_norm_g[0],
                      w_out_even[0])
    odd = _prep_odd(w_in_odd[0], conv_w[0], w_out_odd[0])
    bias_tab = _bias_table(rel_bias)
    consts = _gla_constants()
    run = lambda x: _trunk(x, even, odd, bias_tab, swa_sink[0], ln_g, ln_b, consts)
    return (run(x_prompt), run(x_sample))
```

```python
import functools
import math

import numpy as np
import jax
import jax.numpy as jnp
from jax import lax
from jax.experimental import pallas as pl
from jax.experimental.pallas import tpu as pltpu

F32 = jnp.float32
BF16 = jnp.bfloat16

D_MODEL = 1024
DEPTH = 2
GLA_HEADS = 4
GLA_DK = 128
GLA_WIDTH = 512
GLA_RANK = 16
GLA_TAU = 16.0
SWA_HEADS = 8
SWA_KV_HEADS = 2
SWA_HD = 64
SWA_WIDTH = 512
SWA_KVW = 128
WINDOW = 128
BLOCK = 128
REL_BUCKETS = 32
REL_MAX_DIST = 128
CONV_WIDTH = 1024
DN_ALPHA = (2 * DEPTH) ** 0.25
LN_EPS = 1e-5
NORM_EPS = 1e-6
NEG_BIG = -1e30
LOG2E = math.log2(math.e)
SWA_ONES_ROWS = 16

GLA_COLS = 4 * GLA_WIDTH
SWA_COLS = 2 * SWA_WIDTH + 2 * SWA_KVW
GD_COLS = 2 * GLA_RANK
GD_PAD = 128
EVEN_COLS = GLA_COLS + SWA_COLS + GD_PAD

GLA_CHUNK = 128
TOT_ROWS = 16
VMEM_LIMIT = 56 * 1024 * 1024


def _dot(a, b):
    return jnp.dot(a, b, preferred_element_type=F32)


def _dot_nt(a, b):
    return lax.dot_general(a, b, (((1,), (1,)), ((), ())), preferred_element_type=F32)


def _dot_tn(a, b):
    return lax.dot_general(a, b, (((0,), (0,)), ((), ())), preferred_element_type=F32)


def _silu(z):
    return z / (1.0 + jnp.exp(-z))


def _layer_norm(y, g, b):
    mu = jnp.mean(y, axis=-1, keepdims=True)
    yc = y - mu
    var = jnp.mean(yc * yc, axis=-1, keepdims=True)
    return yc * lax.rsqrt(var + LN_EPS) * g + b


def _params(*sem):
    return pltpu.CompilerParams(dimension_semantics=sem, vmem_limit_bytes=VMEM_LIMIT)


def _inproj_even_kernel(x_ref, w_ref, gla_ref, swa_ref, gd_ref):
    xb = x_ref[0].astype(BF16)
    for c0 in range(0, GLA_COLS, 512):
        gla_ref[0, :, c0:c0 + 512] = _dot(xb, w_ref[:, c0:c0 + 512]).astype(BF16)
    for c0 in range(0, SWA_COLS, 256):
        swa_ref[0, :, c0:c0 + 256] = _dot(xb, w_ref[:, GLA_COLS + c0:GLA_COLS + c0 + 256]).astype(BF16)
    gd = _dot(xb, w_ref[:, GLA_COLS + SWA_COLS:])
    gd_ref[0] = gd[:, :GD_COLS]


def _inproj_even(x, w, tm):
    bn, ln, _ = x.shape
    return pl.pallas_call(
        _inproj_even_kernel,
        grid=(bn, ln // tm),
        in_specs=[
            pl.BlockSpec((1, tm, D_MODEL), lambda b, t: (b, t, 0)),
            pl.BlockSpec((D_MODEL, EVEN_COLS), lambda b, t: (0, 0)),
        ],
        out_specs=[
            pl.BlockSpec((1, tm, GLA_COLS), lambda b, t: (b, t, 0)),
            pl.BlockSpec((1, tm, SWA_COLS), lambda b, t: (b, t, 0)),
            pl.BlockSpec((1, tm, GD_COLS), lambda b, t: (b, t, 0)),
        ],
        out_shape=[
            jax.ShapeDtypeStruct((bn, ln, GLA_COLS), BF16),
            jax.ShapeDtypeStruct((bn, ln, SWA_COLS), BF16),
            jax.ShapeDtypeStruct((bn, ln, GD_COLS), F32),
        ],
        compiler_params=_params("parallel", "parallel"),
        name="inproj_even",
    )(x, w)


def _gla_kernel(q_ref, k_ref, v_ref, gd_ref, wup_ref, bias_ref, tri_ref, mask_ref, o_ref,
                st_ref, logd_ref, qt_ref, kt_ref, kd_ref, u_ref, et_ref, *, tile):
    d = pl.program_id(0)
    t = pl.program_id(2)
    nchunk = tile // GLA_CHUNK

    @pl.when(t == 0)
    def _():
        st_ref[...] = jnp.zeros_like(st_ref)

    a = _dot(gd_ref[0].astype(BF16), wup_ref[0]) + bias_ref[0]
    logd_ref[...] = ((jnp.minimum(a, 0.0) - jnp.log(1.0 + jnp.exp(-jnp.abs(a)))) * (1.0 / GLA_TAU)).astype(BF16)

    tri = tri_ref[0]
    keep = mask_ref[0] > 0.0
    heads = [slice(h * GLA_DK, (h + 1) * GLA_DK) for h in range(GLA_HEADS)]

    for c in range(nchunk):
        rows = slice(c * GLA_CHUNK, (c + 1) * GLA_CHUNK)
        for h, hs in enumerate(heads):
            res = _dot(tri, logd_ref[rows, hs])
            b = res[:GLA_CHUNK]
            etot = jnp.exp(res[GLA_CHUNK:GLA_CHUNK + 1])
            kt = k_ref[0, rows, hs].astype(F32) * jnp.exp(-b)
            qt_ref[rows, hs] = (q_ref[0, rows, hs].astype(F32) * jnp.exp(b)).astype(BF16)
            kt_ref[rows, hs] = kt.astype(BF16)
            kd_ref[rows, hs] = (kt * etot).astype(BF16)
            et_ref[c, :, hs] = etot

    for c in range(nchunk):
        rows = slice(c * GLA_CHUNK, (c + 1) * GLA_CHUNK)
        for h, hs in enumerate(heads):
            v = v_ref[0, rows, hs]
            s = jnp.where(keep, _dot_nt(qt_ref[rows, hs], kt_ref[rows, hs]), 0.0).astype(BF16)
            o_ref[0, 0, rows, hs] = _dot(s, v)
            u_ref[c, h] = _dot_tn(v, kd_ref[rows, hs])

    for c in range(nchunk):
        cc = c + d * (nchunk - 1 - 2 * c)
        rows = pl.ds(pl.multiple_of(cc * GLA_CHUNK, GLA_CHUNK), GLA_CHUNK)
        for h, hs in enumerate(heads):
            st = st_ref[h]
            o_ref[0, 0, rows, hs] += _dot_nt(qt_ref[rows, hs], st.astype(BF16))
            st_ref[h] = st * et_ref[cc, :, hs] + u_ref[cc, h]


def _gla(gla, gd, wup, bias, tri, mask, tile):
    bn, ln, _ = gla.shape
    nt = ln // tile

    def tok(d, b, t):
        return t + d * (nt - 1 - 2 * t)

    return pl.pallas_call(
        functools.partial(_gla_kernel, tile=tile),
        grid=(2, bn, nt),
        in_specs=[
            pl.BlockSpec((1, tile, GLA_WIDTH), lambda d, b, t: (b, tok(d, b, t), 0)),
            pl.BlockSpec((1, tile, GLA_WIDTH), lambda d, b, t: (b, tok(d, b, t), 1)),
            pl.BlockSpec((1, tile, GLA_WIDTH), lambda d, b, t: (b, tok(d, b, t), 2)),
            pl.BlockSpec((1, tile, GD_COLS), lambda d, b, t: (b, tok(d, b, t), 0)),
            pl.BlockSpec((1, GD_COLS, GLA_WIDTH), lambda d, b, t: (d, 0, 0)),
            pl.BlockSpec((1, 1, GLA_WIDTH), lambda d, b, t: (d, 0, 0)),
            pl.BlockSpec((1, GLA_CHUNK + TOT_ROWS, GLA_CHUNK), lambda d, b, t: (d, 0, 0)),
            pl.BlockSpec((1, GLA_CHUNK, GLA_CHUNK), lambda d, b, t: (d, 0, 0)),
        ],
        out_specs=pl.BlockSpec((1, 1, tile, GLA_WIDTH), lambda d, b, t: (d, b, tok(d, b, t), 0)),
        out_shape=jax.ShapeDtypeStruct((2, bn, ln, GLA_WIDTH), F32),
        scratch_shapes=[
            pltpu.VMEM((GLA_HEADS, GLA_DK, GLA_DK), F32),
            pltpu.VMEM((tile, GLA_WIDTH), BF16),
            pltpu.VMEM((tile, GLA_WIDTH), BF16),
            pltpu.VMEM((tile, GLA_WIDTH), BF16),
            pltpu.VMEM((tile, GLA_WIDTH), BF16),
            pltpu.VMEM((tile // GLA_CHUNK, GLA_HEADS, GLA_DK, GLA_DK), F32),
            pltpu.VMEM((tile // GLA_CHUNK, 1, GLA_WIDTH), F32),
        ],
        compiler_params=_params("arbitrary", "arbitrary", "arbitrary"),
        name="gla_scan",
    )(gla, gla, gla, gd, wup, bias, tri, mask)


def _gla_constants():
    i = np.arange(GLA_CHUNK)[:, None]
    j = np.arange(GLA_CHUNK)[None, :]
    lower = (j <= i).astype(np.float32)
    upper = (j >= i).astype(np.float32)
    tri = np.zeros((2, GLA_CHUNK + TOT_ROWS, GLA_CHUNK), np.float32)
    tri[0, :GLA_CHUNK] = lower
    tri[1, :GLA_CHUNK] = upper
    tri[:, GLA_CHUNK:] = 1.0
    mask = np.stack([(j <= i), (j > i)]).astype(np.float32)
    return jnp.asarray(tri, BF16), jnp.asarray(mask, F32)


def _rel_tables():
    i = np.arange(BLOCK)[:, None]
    j = np.arange(3 * BLOCK)[None, :]
    rel = j - BLOCK - i
    half = REL_BUCKETS // 2
    max_exact = half // 2
    n = np.abs(rel)
    large = max_exact + (np.log(np.maximum(n, 1) / max_exact) / np.log(REL_MAX_DIST / max_exact)
                         * (half - max_exact)).astype(np.int32)
    large = np.minimum(large, half - 1)
    bucket = (rel > 0).astype(np.int32) * half + np.where(n < max_exact, n, large)
    band = np.abs(rel) <= WINDOW
    col = np.broadcast_to(j, rel.shape)
    valid = np.stack([band & (col >= BLOCK), band, band & (col < 2 * BLOCK)])
    return np.ascontiguousarray(bucket.T).astype(np.int32), np.ascontiguousarray(valid.transpose(0, 2, 1)).astype(np.int32)


def _bias_kernel(rb_ref, bucket_ref, valid_ref, out_ref):
    h = pl.program_id(1)
    bucket = bucket_ref[...]
    acc = jnp.zeros(bucket.shape, F32)
    for kk in range(REL_BUCKETS):
        acc = jnp.where(bucket == kk, rb_ref[kk, h] * LOG2E, acc)
    out_ref[0, 0] = jnp.where(valid_ref[0] > 0, acc, NEG_BIG)


def _bias_table(rel_bias):
    bucket, valid = _rel_tables()
    return pl.pallas_call(
        _bias_kernel,
        grid=(3, SWA_HEADS),
        in_specs=[
            pl.BlockSpec(memory_space=pltpu.SMEM),
            pl.BlockSpec((3 * BLOCK, BLOCK), lambda e, h: (0, 0)),
            pl.BlockSpec((1, 3 * BLOCK, BLOCK), lambda e, h: (e, 0, 0)),
        ],
        out_specs=pl.BlockSpec((1, 1, 3 * BLOCK, BLOCK), lambda e, h: (e, h, 0, 0)),
        out_shape=jax.ShapeDtypeStruct((3, SWA_HEADS, 3 * BLOCK, BLOCK), F32),
        compiler_params=_params("arbitrary", "arbitrary"),
        name="swa_bias_table",
    )(rel_bias.astype(F32), jnp.asarray(bucket), jnp.asarray(valid))


def _swa_kernel(sink_ref, q_ref, z_ref, kp_ref, kc_ref, kn_ref, vp_ref, vc_ref, vn_ref, bias_ref, o_ref,
                st_ref, pt_ref):
    kcat = jnp.concatenate([kp_ref[0], kc_ref[0], kn_ref[0]], axis=0)
    vcat = jnp.concatenate([vp_ref[0], vc_ref[0], vn_ref[0]], axis=0)
    kswap = jnp.concatenate([kcat[:, SWA_HD:], kcat[:, :SWA_HD]], axis=1)
    lane = lax.broadcasted_iota(jnp.int32, kcat.shape, 1)
    low = lane < SWA_HD
    zero = jnp.zeros_like(kcat)
    kmat = {(0, 0): jnp.where(low, kcat, zero), (0, 1): jnp.where(low, zero, kswap),
            (1, 0): jnp.where(low, kswap, zero), (1, 1): jnp.where(low, zero, kcat)}
    vt = vcat.astype(F32).T.astype(BF16)
    ones = jnp.ones((SWA_ONES_ROWS, 3 * BLOCK), BF16)
    half = lax.broadcasted_iota(jnp.int32, (1, 2 * BLOCK), 1) < BLOCK
    pairs_per_kv = SWA_HEADS // SWA_KV_HEADS // 2
    pairs = [[slice((pairs_per_kv * g + i) * 128, (pairs_per_kv * g + i + 1) * 128) for i in range(pairs_per_kv)]
             for g in range(SWA_KV_HEADS)]
    combos = [(g, e) for g in range(SWA_KV_HEADS) for e in range(2)]
    for c, (g, e) in enumerate(combos):
        h0 = 2 * pairs_per_kv * g + e
        qg = jnp.concatenate([q_ref[0, :, ps] for ps in pairs[g]], axis=0)
        st_ref[c] = _dot_nt(kmat[(g, e)], qg) + jnp.concatenate([bias_ref[0, h0], bias_ref[0, h0 + 2]], axis=1)
    stats = []
    for c, (g, e) in enumerate(combos):
        h0 = 2 * pairs_per_kv * g + e
        sink = jnp.where(half, sink_ref[0, h0], sink_ref[0, h0 + 2]) * LOG2E
        m = jnp.maximum(jnp.max(st_ref[c], axis=0, keepdims=True), sink)
        pt_ref[c] = jnp.exp2(st_ref[c] - m).astype(BF16)
        stats.append(jnp.exp2(sink - m))
    outs = {}
    for c, (g, e) in enumerate(combos):
        vaug = jnp.concatenate([vt[g * SWA_HD:(g + 1) * SWA_HD], ones], axis=0)
        ot = _dot(vaug, pt_ref[c])
        outs[(g, e)] = ot[:SWA_HD] * (1.0 / (ot[SWA_HD:SWA_HD + 1] + stats[c]))
    for g in range(SWA_KV_HEADS):
        for i, ps in enumerate(pairs[g]):
            cs = slice(i * BLOCK, (i + 1) * BLOCK)
            o = jnp.concatenate([outs[(g, 0)][:, cs], outs[(g, 1)][:, cs]], axis=0).T
            o_ref[0, :, ps] = (o * _silu(z_ref[0, :, ps].astype(F32))).astype(BF16)


def _swa(swa, bias, sink):
    bn, ln, _ = swa.shape
    nb = ln // BLOCK
    assert nb >= 2
    kcol = 2 * SWA_WIDTH // SWA_KVW
    vcol = kcol + 1

    def edge(n):
        return jnp.where(n == 0, 0, jnp.where(n == nb - 1, 2, 1))

    kv = lambda col, off: pl.BlockSpec(
        (1, BLOCK, SWA_KVW), lambda b, n: (b, jnp.clip(n + off, 0, nb - 1), col))
    return pl.pallas_call(
        _swa_kernel,
        grid=(bn, nb),
        in_specs=[
            pl.BlockSpec(memory_space=pltpu.SMEM),
            pl.BlockSpec((1, BLOCK, SWA_WIDTH), lambda b, n: (b, n, 0)),
            pl.BlockSpec((1, BLOCK, SWA_WIDTH), lambda b, n: (b, n, 1)),
            kv(kcol, -1), kv(kcol, 0), kv(kcol, 1),
            kv(vcol, -1), kv(vcol, 0), kv(vcol, 1),
            pl.BlockSpec((1, SWA_HEADS, 3 * BLOCK, BLOCK), lambda b, n: (edge(n), 0, 0, 0)),
        ],
        out_specs=pl.BlockSpec((1, BLOCK, SWA_WIDTH), lambda b, n: (b, n, 0)),
        out_shape=jax.ShapeDtypeStruct((bn, ln, SWA_WIDTH), BF16),
        scratch_shapes=[
            pltpu.VMEM((2 * SWA_KV_HEADS, 3 * BLOCK, 2 * BLOCK), F32),
            pltpu.VMEM((2 * SWA_KV_HEADS, 3 * BLOCK, 2 * BLOCK), BF16),
        ],
        compiler_params=_params("parallel", "arbitrary"),
        name="swa_attention",
    )(sink.reshape(1, SWA_HEADS).astype(F32), swa, swa, swa, swa, swa, swa, swa, swa, bias)


def _outproj_even_kernel(of_ref, ob_ref, z_ref, yb_ref, x_ref, wa_ref, wb_ref, ng_ref, lg_ref, lb_ref, out_ref):
    o = of_ref[0, 0] + ob_ref[0, 0]
    parts = []
    for h in range(GLA_HEADS):
        oh = o[:, h * GLA_DK:(h + 1) * GLA_DK]
        parts.append(oh * lax.rsqrt(jnp.mean(oh * oh, axis=-1, keepdims=True) + NORM_EPS))
    on = jnp.concatenate(parts, axis=1) * ng_ref[...]
    ya = (on * _silu(z_ref[0].astype(F32))).astype(BF16)
    sub = _dot(ya, wa_ref[...]) + _dot(yb_ref[0], wb_ref[...])
    out_ref[0] = _layer_norm(DN_ALPHA * x_ref[0] + sub, lg_ref[...], lb_ref[...])


def _outproj_even(o, gla, yb, x, wa, wb, ng, lg, lb, tm):
    bn, ln, _ = x.shape
    row = lambda cols: pl.BlockSpec((1, cols), lambda b, t: (0, 0))
    return pl.pallas_call(
        _outproj_even_kernel,
        grid=(bn, ln // tm),
        in_specs=[
            pl.BlockSpec((1, 1, tm, GLA_WIDTH), lambda b, t: (0, b, t, 0)),
            pl.BlockSpec((1, 1, tm, GLA_WIDTH), lambda b, t: (1, b, t, 0)),
            pl.BlockSpec((1, tm, GLA_WIDTH), lambda b, t: (b, t, 3)),
            pl.BlockSpec((1, tm, SWA_WIDTH), lambda b, t: (b, t, 0)),
            pl.BlockSpec((1, tm, D_MODEL), lambda b, t: (b, t, 0)),
            pl.BlockSpec((GLA_WIDTH, D_MODEL), lambda b, t: (0, 0)),
            pl.BlockSpec((SWA_WIDTH, D_MODEL), lambda b, t: (0, 0)),
            row(GLA_WIDTH), row(D_MODEL), row(D_MODEL),
        ],
        out_specs=pl.BlockSpec((1, tm, D_MODEL), lambda b, t: (b, t, 0)),
        out_shape=jax.ShapeDtypeStruct((bn, ln, D_MODEL), F32),
        compiler_params=_params("parallel", "parallel"),
        name="outproj_even",
    )(o, o, gla, yb, x, wa, wb, ng, lg, lb)


ODD_CHUNK = 256
HALO = 8


def _odd_kernel(xp_ref, x_ref, xn_ref, win_ref, cw_ref, wout_ref, lg_ref, lb_ref, out_ref, *, tm):
    t = pl.program_id(1)
    nt = pl.num_programs(1)
    x = x_ref[0]
    xcat = jnp.concatenate([xp_ref[0].astype(BF16), x.astype(BF16), xn_ref[0].astype(BF16)], axis=0)
    rows = tm + 2 * HALO
    ridx = lax.broadcasted_iota(jnp.int32, (rows, 1), 0)
    pad_row = ((ridx == HALO - 1) & (t == 0)) | ((ridx == HALO + tm) & (t == nt - 1))
    acc = jnp.zeros((tm, D_MODEL), F32)
    for j in range(CONV_WIDTH // ODD_CHUNK):
        u = _dot(xcat, win_ref[j])
        bg = u[HALO:HALO + tm, :ODD_CHUNK]
        z = u[HALO:HALO + tm, 3 * ODD_CHUNK:]
        th = jnp.where(pad_row, 0.0, u[:, ODD_CHUNK:2 * ODD_CHUNK] * u[:, 2 * ODD_CHUNK:3 * ODD_CHUNK])
        cw = cw_ref[j]
        conv = (cw[0:1] * pltpu.roll(th, 1, axis=0)[HALO:HALO + tm]
                + cw[1:2] * th[HALO:HALO + tm]
                + cw[2:3] * pltpu.roll(th, rows - 1, axis=0)[HALO:HALO + tm])
        mixed = (_silu(z) * bg * conv).astype(BF16)
        acc = acc + _dot(mixed, wout_ref[j])
    out_ref[0] = _layer_norm(DN_ALPHA * x + acc, lg_ref[...], lb_ref[...])


def _odd_layer(x, win, cw, wout, lg, lb, tm):
    bn, ln, _ = x.shape
    nj = CONV_WIDTH // ODD_CHUNK
    hb = tm // HALO
    nh = ln // HALO
    row = lambda cols: pl.BlockSpec((1, cols), lambda b, t: (0, 0))
    return pl.pallas_call(
        functools.partial(_odd_kernel, tm=tm),
        grid=(bn, ln // tm),
        in_specs=[
            pl.BlockSpec((1, HALO, D_MODEL), lambda b, t: (b, jnp.maximum(t * hb - 1, 0), 0)),
            pl.BlockSpec((1, tm, D_MODEL), lambda b, t: (b, t, 0)),
            pl.BlockSpec((1, HALO, D_MODEL), lambda b, t: (b, jnp.minimum((t + 1) * hb, nh - 1), 0)),
            pl.BlockSpec((nj, D_MODEL, 4 * ODD_CHUNK), lambda b, t: (0, 0, 0)),
            pl.BlockSpec((nj, 3, ODD_CHUNK), lambda b, t: (0, 0, 0)),
            pl.BlockSpec((nj, ODD_CHUNK, D_MODEL), lambda b, t: (0, 0, 0)),
            row(D_MODEL), row(D_MODEL),
        ],
        out_specs=pl.BlockSpec((1, tm, D_MODEL), lambda b, t: (b, t, 0)),
        out_shape=jax.ShapeDtypeStruct((bn, ln, D_MODEL), F32),
        compiler_params=_params("parallel", "arbitrary"),
        name="odd_layer",
    )(x, x, x, win, cw, wout, lg, lb)


def _prep_even(w_in, w_up_f, b_f, w_up_b, b_b, norm_g, w_out):
    qa, ka, va, za, gd, qb, kb, vb, zb = jnp.split(
        w_in, np.cumsum([512, 512, 512, 512, GD_COLS, 512, 128, 128])[:].tolist(), axis=1)
    w = jnp.concatenate([
        qa * (GLA_DK ** -0.5), ka, va, za,
        qb * (SWA_HD ** -0.5 * LOG2E), zb, kb, vb,
        gd, jnp.zeros((D_MODEL, GD_PAD - GD_COLS), w_in.dtype)], axis=1).astype(BF16)
    zr = jnp.zeros_like(w_up_f)
    wup = jnp.stack([jnp.concatenate([w_up_f, zr], axis=0), jnp.concatenate([zr, w_up_b], axis=0)]).astype(BF16)
    bias = jnp.stack([b_f, b_b]).reshape(2, 1, GLA_WIDTH).astype(F32)
    ng = jnp.tile(norm_g.astype(F32), GLA_HEADS).reshape(1, GLA_WIDTH)
    wa = w_out[:GLA_WIDTH].astype(BF16)
    wb = w_out[GLA_WIDTH:].astype(BF16)
    return w, wup, bias, ng, wa, wb


def _prep_odd(w_in, conv_w, w_out):
    nj = CONV_WIDTH // ODD_CHUNK
    win = w_in.reshape(D_MODEL, 4, nj, ODD_CHUNK).transpose(2, 0, 1, 3).reshape(nj, D_MODEL, 4 * ODD_CHUNK).astype(BF16)
    cw = conv_w.reshape(3, nj, ODD_CHUNK).transpose(1, 0, 2).astype(F32)
    wout = w_out.reshape(nj, ODD_CHUNK, D_MODEL).astype(BF16)
    return win, cw, wout


def _trunk(x, even, odd, bias_tab, sink, ln_g, ln_b, consts, tm=512, gla_tile=1024):
    w, wup, bias, ng, wa, wb = even
    win, cw, wout = odd
    tri, mask = consts
    gla, swa, gd = _inproj_even(x, w, tm)
    o = _gla(gla, gd, wup, bias, tri, mask, gla_tile)
    yb = _swa(swa, bias_tab, sink)
    lg = ln_g.astype(F32).reshape(DEPTH, 1, D_MODEL)
    lb = ln_b.astype(F32).reshape(DEPTH, 1, D_MODEL)
    x1 = _outproj_even(o, gla, yb, x, wa, wb, ng, lg[0], lb[0], tm)
    return _odd_layer(x1, win, cw, wout, lg[1], lb[1], tm)


def kernel(x_prompt, x_sample, w_in_even, gla_w_up_fwd, gla_b_fwd, gla_w_up_bwd, gla_b_bwd, gla_norm_g, swa_sink,
           rel_bias, w_out_even, w_in_odd, conv_w, w_out_odd, ln_g, ln_b):
    even = _prep_even(w_in_even[0], gla_w_up_fwd[0], gla_b_fwd[0], gla_w_up_bwd[0], gla_b_bwd[0], gla_norm_g[0],
                      w_out_even[0])
    odd = _prep_odd(w_in_odd[0], conv_w[0], w_out_odd[0])
    bias_tab = _bias_table(rel_bias)
    consts = _gla_constants()
    run = lambda x: _trunk(x, even, odd, bias_tab, swa_sink[0], ln_g, ln_b, consts)
    return (run(x_prompt), run(x_sample))
```

```python
import functools
import math

import numpy as np
import jax
import jax.numpy as jnp
from jax import lax
from jax.experimental import pallas as pl
from jax.experimental.pallas import tpu as pltpu

F32 = jnp.float32
BF16 = jnp.bfloat16

D_MODEL = 1024
DEPTH = 2
GLA_HEADS = 4
GLA_DK = 128
GLA_WIDTH = 512
GLA_RANK = 16
GLA_TAU = 16.0
SWA_HEADS = 8
SWA_KV_HEADS = 2
SWA_HD = 64
SWA_WIDTH = 512
SWA_KVW = 128
WINDOW = 128
BLOCK = 128
REL_BUCKETS = 32
REL_MAX_DIST = 128
CONV_WIDTH = 1024
DN_ALPHA = (2 * DEPTH) ** 0.25
LN_EPS = 1e-5
NORM_EPS = 1e-6
NEG_BIG = -1e30
LOG2E = math.log2(math.e)
SWA_ONES_ROWS = 16

GLA_COLS = 4 * GLA_WIDTH
SWA_COLS = 2 * SWA_WIDTH + 2 * SWA_KVW
GD_COLS = 2 * GLA_RANK
GD_PAD = 128
EVEN_COLS = GLA_COLS + SWA_COLS + GD_PAD

GLA_CHUNK = 128
TOT_ROWS = 16
VMEM_LIMIT = 56 * 1024 * 1024


def _dot(a, b):
    return jnp.dot(a, b, preferred_element_type=F32)


def _dot_nt(a, b):
    return lax.dot_general(a, b, (((1,), (1,)), ((), ())), preferred_element_type=F32)


def _dot_tn(a, b):
    return lax.dot_general(a, b, (((0,), (0,)), ((), ())), preferred_element_type=F32)


def _silu(z):
    return z / (1.0 + jnp.exp(-z))


def _layer_norm(y, g, b):
    mu = jnp.mean(y, axis=-1, keepdims=True)
    yc = y - mu
    var = jnp.mean(yc * yc, axis=-1, keepdims=True)
    return yc * lax.rsqrt(var + LN_EPS) * g + b


def _params(*sem):
    return pltpu.CompilerParams(dimension_semantics=sem, vmem_limit_bytes=VMEM_LIMIT)


def _inproj_even_kernel(x_ref, w_ref, gla_ref, swa_ref, gd_ref):
    xb = x_ref[0].astype(BF16)
    for c0 in range(0, GLA_COLS, 512):
        gla_ref[0, :, c0:c0 + 512] = _dot(xb, w_ref[:, c0:c0 + 512]).astype(BF16)
    for c0 in range(0, SWA_COLS, 256):
        swa_ref[0, :, c0:c0 + 256] = _dot(xb, w_ref[:, GLA_COLS + c0:GLA_COLS + c0 + 256]).astype(BF16)
    gd = _dot(xb, w_ref[:, GLA_COLS + SWA_COLS:])
    gd_ref[0] = gd[:, :GD_COLS]


def _inproj_even(x, w, tm):
    bn, ln, _ = x.shape
    return pl.pallas_call(
        _inproj_even_kernel,
        grid=(bn, ln // tm),
        in_specs=[
            pl.BlockSpec((1, tm, D_MODEL), lambda b, t: (b, t, 0)),
            pl.BlockSpec((D_MODEL, EVEN_COLS), lambda b, t: (0, 0)),
        ],
        out_specs=[
            pl.BlockSpec((1, tm, GLA_COLS), lambda b, t: (b, t, 0)),
            pl.BlockSpec((1, tm, SWA_COLS), lambda b, t: (b, t, 0)),
            pl.BlockSpec((1, tm, GD_COLS), lambda b, t: (b, t, 0)),
        ],
        out_shape=[
            jax.ShapeDtypeStruct((bn, ln, GLA_COLS), BF16),
            jax.ShapeDtypeStruct((bn, ln, SWA_COLS), BF16),
            jax.ShapeDtypeStruct((bn, ln, GD_COLS), F32),
        ],
        compiler_params=_params("parallel", "parallel"),
        name="inproj_even",
    )(x, w)


def _gla_kernel(q_ref, k_ref, v_ref, gd_ref, wup_ref, bias_ref, tri_ref, mask_ref, o_ref,
                st_ref, logd_ref, qt_ref, kt_ref, kd_ref, u_ref, et_ref, cum_ref, oi_ref, *, tile):
    d = pl.program_id(0)
    t = pl.program_id(2)
    nchunk = tile // GLA_CHUNK

    @pl.when(t == 0)
    def _():
        st_ref[...] = jnp.zeros_like(st_ref)

    a = _dot(gd_ref[0].astype(BF16), wup_ref[0]) + bias_ref[0]
    logd_ref[...] = ((jnp.minimum(a, 0.0) - jnp.log(1.0 + jnp.exp(-jnp.abs(a)))) * (1.0 / GLA_TAU)).astype(BF16)

    tri = tri_ref[0]
    keep = mask_ref[0] > 0.0
    heads = [slice(h * GLA_DK, (h + 1) * GLA_DK) for h in range(GLA_HEADS)]

    for c in range(nchunk):
        rows = slice(c * GLA_CHUNK, (c + 1) * GLA_CHUNK)
        cum_ref[c % 2] = _dot(tri, logd_ref[rows, :])
        for h, hs in enumerate(heads):
            b = cum_ref[c % 2, :GLA_CHUNK, hs]
            etot = jnp.exp(cum_ref[c % 2, GLA_CHUNK:GLA_CHUNK + 1, hs])
            kt = k_ref[0, rows, hs].astype(F32) * jnp.exp(-b)
            qt_ref[rows, hs] = (q_ref[0, rows, hs].astype(F32) * jnp.exp(b)).astype(BF16)
            kt_ref[rows, hs] = kt.astype(BF16)
            kd_ref[rows, hs] = (kt * etot).astype(BF16)
            et_ref[c, :, hs] = etot

    for c in range(nchunk):
        rows = slice(c * GLA_CHUNK, (c + 1) * GLA_CHUNK)
        for h, hs in enumerate(heads):
            v = v_ref[0, rows, hs]
            s = jnp.where(keep, _dot_nt(qt_ref[rows, hs], kt_ref[rows, hs]), 0.0).astype(BF16)
            oi_ref[rows, hs] = _dot(s, v)
            u_ref[c, h] = _dot_tn(v, kd_ref[rows, hs])

    for c in range(nchunk):
        cc = c + d * (nchunk - 1 - 2 * c)
        rows = pl.ds(pl.multiple_of(cc * GLA_CHUNK, GLA_CHUNK), GLA_CHUNK)
        for h, hs in enumerate(heads):
            st = st_ref[h]
            o_ref[0, 0, rows, hs] = (oi_ref[rows, hs] + _dot_nt(qt_ref[rows, hs], st.astype(BF16))).astype(BF16)
            st_ref[h] = st * et_ref[cc, :, hs] + u_ref[cc, h]


def _gla(gla, gd, wup, bias, tri, mask, tile):
    bn, ln, _ = gla.shape
    nt = ln // tile

    def tok(d, b, t):
        return t + d * (nt - 1 - 2 * t)

    return pl.pallas_call(
        functools.partial(_gla_kernel, tile=tile),
        grid=(2, bn, nt),
        in_specs=[
            pl.BlockSpec((1, tile, GLA_WIDTH), lambda d, b, t: (b, tok(d, b, t), 0)),
            pl.BlockSpec((1, tile, GLA_WIDTH), lambda d, b, t: (b, tok(d, b, t), 1)),
            pl.BlockSpec((1, tile, GLA_WIDTH), lambda d, b, t: (b, tok(d, b, t), 2)),
            pl.BlockSpec((1, tile, GD_COLS), lambda d, b, t: (b, tok(d, b, t), 0)),
            pl.BlockSpec((1, GD_COLS, GLA_WIDTH), lambda d, b, t: (d, 0, 0)),
            pl.BlockSpec((1, 1, GLA_WIDTH), lambda d, b, t: (d, 0, 0)),
            pl.BlockSpec((1, GLA_CHUNK + TOT_ROWS, GLA_CHUNK), lambda d, b, t: (d, 0, 0)),
            pl.BlockSpec((1, GLA_CHUNK, GLA_CHUNK), lambda d, b, t: (d, 0, 0)),
        ],
        out_specs=pl.BlockSpec((1, 1, tile, GLA_WIDTH), lambda d, b, t: (d, b, tok(d, b, t), 0)),
        out_shape=jax.ShapeDtypeStruct((2, bn, ln, GLA_WIDTH), BF16),
        scratch_shapes=[
            pltpu.VMEM((GLA_HEADS, GLA_DK, GLA_DK), F32),
            pltpu.VMEM((tile, GLA_WIDTH), BF16),
            pltpu.VMEM((tile, GLA_WIDTH), BF16),
            pltpu.VMEM((tile, GLA_WIDTH), BF16),
            pltpu.VMEM((tile, GLA_WIDTH), BF16),
            pltpu.VMEM((tile // GLA_CHUNK, GLA_HEADS, GLA_DK, GLA_DK), F32),
            pltpu.VMEM((tile // GLA_CHUNK, 1, GLA_WIDTH), F32),
            pltpu.VMEM((2, GLA_CHUNK + TOT_ROWS, GLA_WIDTH), F32),
            pltpu.VMEM((tile, GLA_WIDTH), F32),
        ],
        compiler_params=_params("arbitrary", "arbitrary", "arbitrary"),
        name="gla_scan",
    )(gla, gla, gla, gd, wup, bias, tri, mask)


def _gla_constants():
    i = np.arange(GLA_CHUNK)[:, None]
    j = np.arange(GLA_CHUNK)[None, :]
    lower = (j <= i).astype(np.float32)
    upper = (j >= i).astype(np.float32)
    tri = np.zeros((2, GLA_CHUNK + TOT_ROWS, GLA_CHUNK), np.float32)
    tri[0, :GLA_CHUNK] = lower
    tri[1, :GLA_CHUNK] = upper
    tri[:, GLA_CHUNK:] = 1.0
    mask = np.stack([(j <= i), (j > i)]).astype(np.float32)
    return jnp.asarray(tri, BF16), jnp.asarray(mask, F32)


def _rel_tables():
    i = np.arange(BLOCK)[:, None]
    j = np.arange(3 * BLOCK)[None, :]
    rel = j - BLOCK - i
    half = REL_BUCKETS // 2
    max_exact = half // 2
    n = np.abs(rel)
    large = max_exact + (np.log(np.maximum(n, 1) / max_exact) / np.log(REL_MAX_DIST / max_exact)
                         * (half - max_exact)).astype(np.int32)
    large = np.minimum(large, half - 1)
    bucket = (rel > 0).astype(np.int32) * half + np.where(n < max_exact, n, large)
    band = np.abs(rel) <= WINDOW
    col = np.broadcast_to(j, rel.shape)
    valid = np.stack([band & (col >= BLOCK), band, band & (col < 2 * BLOCK)])
    return np.ascontiguousarray(bucket.T).astype(np.int32), np.ascontiguousarray(valid.transpose(0, 2, 1)).astype(np.int32)


def _bias_kernel(rb_ref, bucket_ref, valid_ref, out_ref):
    h = pl.program_id(1)
    bucket = bucket_ref[...]
    acc = jnp.zeros(bucket.shape, F32)
    for kk in range(REL_BUCKETS):
        acc = jnp.where(bucket == kk, rb_ref[kk, h] * LOG2E, acc)
    out_ref[0, 0] = jnp.where(valid_ref[0] > 0, acc, NEG_BIG)


def _bias_table(rel_bias):
    bucket, valid = _rel_tables()
    return pl.pallas_call(
        _bias_kernel,
        grid=(3, SWA_HEADS),
        in_specs=[
            pl.BlockSpec(memory_space=pltpu.SMEM),
            pl.BlockSpec((3 * BLOCK, BLOCK), lambda e, h: (0, 0)),
            pl.BlockSpec((1, 3 * BLOCK, BLOCK), lambda e, h: (e, 0, 0)),
        ],
        out_specs=pl.BlockSpec((1, 1, 3 * BLOCK, BLOCK), lambda e, h: (e, h, 0, 0)),
        out_shape=jax.ShapeDtypeStruct((3, SWA_HEADS, 3 * BLOCK, BLOCK), F32),
        compiler_params=_params("arbitrary", "arbitrary"),
        name="swa_bias_table",
    )(rel_bias.astype(F32), jnp.asarray(bucket), jnp.asarray(valid))


def _swa_kernel(sink_ref, q_ref, z_ref, kp_ref, kc_ref, kn_ref, vp_ref, vc_ref, vn_ref, bias_ref, o_ref,
                st_ref, pt_ref, *, nq):
    n = pl.program_id(1)
    nsteps = pl.num_programs(1)
    kcat = jnp.concatenate([kp_ref[0], kc_ref[0], kn_ref[0]], axis=0)
    vcat = jnp.concatenate([vp_ref[0], vc_ref[0], vn_ref[0]], axis=0)
    kswap = jnp.concatenate([kcat[:, SWA_HD:], kcat[:, :SWA_HD]], axis=1)
    lane = lax.broadcasted_iota(jnp.int32, kcat.shape, 1)
    low = lane < SWA_HD
    zero = jnp.zeros_like(kcat)
    kmat = {(0, 0): jnp.where(low, kcat, zero), (0, 1): jnp.where(low, zero, kswap),
            (1, 0): jnp.where(low, kswap, zero), (1, 1): jnp.where(low, zero, kcat)}
    vt = vcat.astype(F32).T.astype(BF16)
    ones = jnp.ones((SWA_ONES_ROWS, 3 * BLOCK), BF16)
    half = lax.broadcasted_iota(jnp.int32, (1, 2 * BLOCK), 1) < BLOCK
    pairs_per_kv = SWA_HEADS // SWA_KV_HEADS // 2
    pairs = [[slice((pairs_per_kv * g + i) * 128, (pairs_per_kv * g + i + 1) * 128) for i in range(pairs_per_kv)]
             for g in range(SWA_KV_HEADS)]
    combos = [(g, e) for g in range(SWA_KV_HEADS) for e in range(2)]
    ncomb = len(combos)
    for qb in range(nq):
        qrows = slice(qb * BLOCK, (qb + 1) * BLOCK)
        keys = slice(qb * BLOCK, (qb + 3) * BLOCK)
        kind = 1
        if qb == 0:
            kind = jnp.where(n == 0, 0, kind)
        if qb == nq - 1:
            kind = jnp.where(n == nsteps - 1, 2, kind)
        slot = (qb % 2) * ncomb
        for c, (g, e) in enumerate(combos):
            h0 = 2 * pairs_per_kv * g + e
            qg = jnp.concatenate([q_ref[0, qrows, ps] for ps in pairs[g]], axis=0)
            st_ref[slot + c] = (_dot_nt(kmat[(g, e)][keys], qg)
                                + jnp.concatenate([bias_ref[kind, h0], bias_ref[kind, h0 + 2]], axis=1))
        stats = []
        for c, (g, e) in enumerate(combos):
            h0 = 2 * pairs_per_kv * g + e
            sink = jnp.where(half, sink_ref[0, h0], sink_ref[0, h0 + 2]) * LOG2E
            m = jnp.maximum(jnp.max(st_ref[slot + c], axis=0, keepdims=True), sink)
            pt_ref[slot + c] = jnp.exp2(st_ref[slot + c] - m).astype(BF16)
            stats.append(jnp.exp2(sink - m))
        outs = {}
        for c, (g, e) in enumerate(combos):
            vaug = jnp.concatenate([vt[g * SWA_HD:(g + 1) * SWA_HD, keys], ones], axis=0)
            ot = _dot(vaug, pt_ref[slot + c])
            outs[(g, e)] = ot[:SWA_HD] * (1.0 / (ot[SWA_HD:SWA_HD + 1] + stats[c]))
        for g in range(SWA_KV_HEADS):
            for i, ps in enumerate(pairs[g]):
                cs = slice(i * BLOCK, (i + 1) * BLOCK)
                o = jnp.concatenate([outs[(g, 0)][:, cs], outs[(g, 1)][:, cs]], axis=0).T
                o_ref[0, qrows, ps] = (o * _silu(z_ref[0, qrows, ps].astype(F32))).astype(BF16)


def _swa(swa, bias, sink, nq):
    bn, ln, _ = swa.shape
    nb = ln // BLOCK
    assert nb % nq == 0 and nb >= 2
    kcol = 2 * SWA_WIDTH // SWA_KVW
    vcol = kcol + 1
    halo = lambda col, fn: pl.BlockSpec((1, BLOCK, SWA_KVW), lambda b, n: (b, fn(n), col))
    prev = lambda n: jnp.maximum(n * nq - 1, 0)
    nxt = lambda n: jnp.minimum((n + 1) * nq, nb - 1)
    return pl.pallas_call(
        functools.partial(_swa_kernel, nq=nq),
        grid=(bn, nb // nq),
        in_specs=[
            pl.BlockSpec(memory_space=pltpu.SMEM),
            pl.BlockSpec((1, nq * BLOCK, SWA_WIDTH), lambda b, n: (b, n, 0)),
            pl.BlockSpec((1, nq * BLOCK, SWA_WIDTH), lambda b, n: (b, n, 1)),
            halo(kcol, prev), pl.BlockSpec((1, nq * BLOCK, SWA_KVW), lambda b, n: (b, n, kcol)), halo(kcol, nxt),
            halo(vcol, prev), pl.BlockSpec((1, nq * BLOCK, SWA_KVW), lambda b, n: (b, n, vcol)), halo(vcol, nxt),
            pl.BlockSpec((3, SWA_HEADS, 3 * BLOCK, BLOCK), lambda b, n: (0, 0, 0, 0)),
        ],
        out_specs=pl.BlockSpec((1, nq * BLOCK, SWA_WIDTH), lambda b, n: (b, n, 0)),
        out_shape=jax.ShapeDtypeStruct((bn, ln, SWA_WIDTH), BF16),
        scratch_shapes=[
            pltpu.VMEM((4 * SWA_KV_HEADS, 3 * BLOCK, 2 * BLOCK), F32),
            pltpu.VMEM((4 * SWA_KV_HEADS, 3 * BLOCK, 2 * BLOCK), BF16),
        ],
        compiler_params=_params("parallel", "arbitrary"),
        name="swa_attention",
    )(sink.reshape(1, SWA_HEADS).astype(F32), swa, swa, swa, swa, swa, swa, swa, swa, bias)


def _outproj_even_kernel(of_ref, ob_ref, z_ref, yb_ref, x_ref, wa_ref, wb_ref, ng_ref, lg_ref, lb_ref, out_ref):
    o = of_ref[0, 0].astype(F32) + ob_ref[0, 0].astype(F32)
    parts = []
    for h in range(GLA_HEADS):
        oh = o[:, h * GLA_DK:(h + 1) * GLA_DK]
        parts.append(oh * lax.rsqrt(jnp.mean(oh * oh, axis=-1, keepdims=True) + NORM_EPS))
    on = jnp.concatenate(parts, axis=1) * ng_ref[...]
    ya = (on * _silu(z_ref[0].astype(F32))).astype(BF16)
    sub = _dot(ya, wa_ref[...]) + _dot(yb_ref[0], wb_ref[...])
    out_ref[0] = _layer_norm(DN_ALPHA * x_ref[0] + sub, lg_ref[...], lb_ref[...])


def _outproj_even(o, gla, yb, x, wa, wb, ng, lg, lb, tm):
    bn, ln, _ = x.shape
    row = lambda cols: pl.BlockSpec((1, cols), lambda b, t: (0, 0))
    return pl.pallas_call(
        _outproj_even_kernel,
        grid=(bn, ln // tm),
        in_specs=[
            pl.BlockSpec((1, 1, tm, GLA_WIDTH), lambda b, t: (0, b, t, 0)),
            pl.BlockSpec((1, 1, tm, GLA_WIDTH), lambda b, t: (1, b, t, 0)),
            pl.BlockSpec((1, tm, GLA_WIDTH), lambda b, t: (b, t, 3)),
            pl.BlockSpec((1, tm, SWA_WIDTH), lambda b, t: (b, t, 0)),
            pl.BlockSpec((1, tm, D_MODEL), lambda b, t: (b, t, 0)),
            pl.BlockSpec((GLA_WIDTH, D_MODEL), lambda b, t: (0, 0)),
            pl.BlockSpec((SWA_WIDTH, D_MODEL), lambda b, t: (0, 0)),
            row(GLA_WIDTH), row(D_MODEL), row(D_MODEL),
        ],
        out_specs=pl.BlockSpec((1, tm, D_MODEL), lambda b, t: (b, t, 0)),
        out_shape=jax.ShapeDtypeStruct((bn, ln, D_MODEL), F32),
        compiler_params=_params("parallel", "parallel"),
        name="outproj_even",
    )(o, o, gla, yb, x, wa, wb, ng, lg, lb)


ODD_CHUNK = 256
HALO = 8
ODD_OUT_ROWS = 256


def _odd_kernel(xp_ref, x_ref, xn_ref, win_ref, cw_ref, wout_ref, lg_ref, lb_ref, out_ref,
                xcat_ref, u_ref, th_ref, mixed_ref, *, tm):
    t = pl.program_id(1)
    nt = pl.num_programs(1)
    main = slice(HALO, HALO + tm)
    xcat_ref[...] = jnp.concatenate([xp_ref[0], x_ref[0], xn_ref[0]], axis=0).astype(BF16)
    for j in range(CONV_WIDTH // ODD_CHUNK):
        cols = slice(j * ODD_CHUNK, (j + 1) * ODD_CHUNK)
        ub = u_ref.at[j % 2]
        for i in range(4):
            ub[:, i * ODD_CHUNK:(i + 1) * ODD_CHUNK] = _dot(
                xcat_ref[...], win_ref[:, i * CONV_WIDTH + j * ODD_CHUNK:i * CONV_WIDTH + (j + 1) * ODD_CHUNK])
        th_ref[...] = ub[:, ODD_CHUNK:2 * ODD_CHUNK] * ub[:, 2 * ODD_CHUNK:3 * ODD_CHUNK]
        first = pl.ds(HALO - 1, 1)
        last = pl.ds(HALO + tm, 1)
        th_ref[first, :] = jnp.where(t == 0, 0.0, th_ref[first, :])
        th_ref[last, :] = jnp.where(t == nt - 1, 0.0, th_ref[last, :])
        conv = (cw_ref[0:1, cols] * th_ref[HALO - 1:HALO - 1 + tm, :]
                + cw_ref[1:2, cols] * th_ref[main, :]
                + cw_ref[2:3, cols] * th_ref[HALO + 1:HALO + 1 + tm, :])
        mixed_ref[:, cols] = (_silu(ub[main, 3 * ODD_CHUNK:]) * ub[main, :ODD_CHUNK] * conv).astype(BF16)
    for r0 in range(0, tm, ODD_OUT_ROWS):
        rs = slice(r0, r0 + ODD_OUT_ROWS)
        acc = _dot(mixed_ref[rs, :], wout_ref[...])
        out_ref[0, rs, :] = _layer_norm(DN_ALPHA * x_ref[0, rs, :] + acc, lg_ref[...], lb_ref[...])


def _odd_layer(x, win, cw, wout, lg, lb, tm):
    bn, ln, _ = x.shape
    hb = tm // HALO
    nh = ln // HALO
    rows = tm + 2 * HALO
    row = lambda cols: pl.BlockSpec((1, cols), lambda b, t: (0, 0))
    return pl.pallas_call(
        functools.partial(_odd_kernel, tm=tm),
        grid=(bn, ln // tm),
        in_specs=[
            pl.BlockSpec((1, HALO, D_MODEL), lambda b, t: (b, jnp.maximum(t * hb - 1, 0), 0)),
            pl.BlockSpec((1, tm, D_MODEL), lambda b, t: (b, t, 0)),
            pl.BlockSpec((1, HALO, D_MODEL), lambda b, t: (b, jnp.minimum((t + 1) * hb, nh - 1), 0)),
            pl.BlockSpec((D_MODEL, 4 * CONV_WIDTH), lambda b, t: (0, 0)),
            pl.BlockSpec((3, CONV_WIDTH), lambda b, t: (0, 0)),
            pl.BlockSpec((CONV_WIDTH, D_MODEL), lambda b, t: (0, 0)),
            row(D_MODEL), row(D_MODEL),
        ],
        out_specs=pl.BlockSpec((1, tm, D_MODEL), lambda b, t: (b, t, 0)),
        out_shape=jax.ShapeDtypeStruct((bn, ln, D_MODEL), F32),
        scratch_shapes=[
            pltpu.VMEM((rows, D_MODEL), BF16),
            pltpu.VMEM((2, rows, 4 * ODD_CHUNK), F32),
            pltpu.VMEM((rows, ODD_CHUNK), F32),
            pltpu.VMEM((tm, CONV_WIDTH), BF16),
        ],
        compiler_params=_params("parallel", "arbitrary"),
        name="odd_layer",
    )(x, x, x, win, cw, wout, lg, lb)


def _prep_even(w_in, w_up_f, b_f, w_up_b, b_b, norm_g, w_out):
    qa, ka, va, za, gd, qb, kb, vb, zb = jnp.split(
        w_in, np.cumsum([512, 512, 512, 512, GD_COLS, 512, 128, 128])[:].tolist(), axis=1)
    w = jnp.concatenate([
        qa * (GLA_DK ** -0.5), ka, va, za,
        qb * (SWA_HD ** -0.5 * LOG2E), zb, kb, vb,
        gd, jnp.zeros((D_MODEL, GD_PAD - GD_COLS), w_in.dtype)], axis=1).astype(BF16)
    zr = jnp.zeros_like(w_up_f)
    wup = jnp.stack([jnp.concatenate([w_up_f, zr], axis=0), jnp.concatenate([zr, w_up_b], axis=0)]).astype(BF16)
    bias = jnp.stack([b_f, b_b]).reshape(2, 1, GLA_WIDTH).astype(F32)
    ng = jnp.tile(norm_g.astype(F32), GLA_HEADS).reshape(1, GLA_WIDTH)
    wa = w_out[:GLA_WIDTH].astype(BF16)
    wb = w_out[GLA_WIDTH:].astype(BF16)
    return w, wup, bias, ng, wa, wb


def _prep_odd(w_in, conv_w, w_out):
    win = w_in.astype(BF16)
    cw = conv_w.astype(F32)
    wout = w_out.astype(BF16)
    return win, cw, wout


def _trunk(x, even, odd, bias_tab, sink, ln_g, ln_b, consts, tm=512, gla_tile=1024, swa_nq=4):
    w, wup, bias, ng, wa, wb = even
    win, cw, wout = odd
    tri, mask = consts
    gla, swa, gd = _inproj_even(x, w, tm)
    o = _gla(gla, gd, wup, bias, tri, mask, gla_tile)
    yb = _swa(swa, bias_tab, sink, swa_nq)
    lg = ln_g.astype(F32).reshape(DEPTH, 1, D_MODEL)
    lb = ln_b.astype(F32).reshape(DEPTH, 1, D_MODEL)
    x1 = _outproj_even(o, gla, yb, x, wa, wb, ng, lg[0], lb[0], tm)
    return _odd_layer(x1, win, cw, wout, lg[1], lb[1], tm)


def kernel(x_prompt, x_sample, w_in_even, gla_w_up_fwd, gla_b_fwd, gla_w_up_bwd, gla_b_bwd, gla_norm_g, swa_sink,
           rel_bias, w_out_even, w_in_odd, conv_w, w_out_odd, ln_g, ln_b):
    even = _prep_even(w_in_even[0], gla_w_up_fwd[0], gla_b_fwd[0], gla_w_up_bwd[0], gla_b_bwd[0], gla_norm_g[0],
                      w_out_even[0])
    odd = _prep_odd(w_in_odd[0], conv_w[0], w_out_odd[0])
    bias_tab = _bias_table(rel_bias)
    consts = _gla_constants()
    run = lambda x: _trunk(x, even, odd, bias_tab, swa_sink[0], ln_g, ln_b, consts)
    return (run(x_prompt), run(x_sample))
```

```python
import functools
import math

import numpy as np
import jax
import jax.numpy as jnp
from jax import lax
from jax.experimental import pallas as pl
from jax.experimental.pallas import tpu as pltpu

F32 = jnp.float32
BF16 = jnp.bfloat16

D_MODEL = 1024
DEPTH = 2
GLA_HEADS = 4
GLA_DK = 128
GLA_WIDTH = 512
GLA_RANK = 16
GLA_TAU = 16.0
SWA_HEADS = 8
SWA_KV_HEADS = 2
SWA_HD = 64
SWA_WIDTH = 512
SWA_KVW = 128
WINDOW = 128
BLOCK = 128
REL_BUCKETS = 32
REL_MAX_DIST = 128
CONV_WIDTH = 1024
DN_ALPHA = (2 * DEPTH) ** 0.25
LN_EPS = 1e-5
NORM_EPS = 1e-6
NEG_BIG = -1e30
LOG2E = math.log2(math.e)
SWA_ONES_ROWS = 16

GLA_COLS = 4 * GLA_WIDTH
SWA_COLS = 2 * SWA_WIDTH + 2 * SWA_KVW
GD_COLS = 2 * GLA_RANK
GD_PAD = 128
EVEN_COLS = GLA_COLS + SWA_COLS + GD_PAD

GLA_CHUNK = 128
TOT_ROWS = 16
VMEM_LIMIT = 56 * 1024 * 1024


def _dot(a, b):
    return jnp.dot(a, b, preferred_element_type=F32)


def _dot_nt(a, b):
    return lax.dot_general(a, b, (((1,), (1,)), ((), ())), preferred_element_type=F32)


def _dot_tn(a, b):
    return lax.dot_general(a, b, (((0,), (0,)), ((), ())), preferred_element_type=F32)


def _silu(z):
    return z / (1.0 + jnp.exp(-z))


def _layer_norm(y, g, b):
    mu = jnp.mean(y, axis=-1, keepdims=True)
    yc = y - mu
    var = jnp.mean(yc * yc, axis=-1, keepdims=True)
    return yc * lax.rsqrt(var + LN_EPS) * g + b


def _params(*sem):
    return pltpu.CompilerParams(dimension_semantics=sem, vmem_limit_bytes=VMEM_LIMIT)


def _inproj_even_kernel(x_ref, w_ref, gla_ref, swa_ref, gd_ref):
    xb = x_ref[0].astype(BF16)
    for c0 in range(0, GLA_COLS, 512):
        gla_ref[0, :, c0:c0 + 512] = _dot(xb, w_ref[:, c0:c0 + 512]).astype(BF16)
    for c0 in range(0, SWA_COLS, 256):
        swa_ref[0, :, c0:c0 + 256] = _dot(xb, w_ref[:, GLA_COLS + c0:GLA_COLS + c0 + 256]).astype(BF16)
    gd = _dot(xb, w_ref[:, GLA_COLS + SWA_COLS:])
    gd_ref[0] = gd[:, :GD_COLS]


def _inproj_even(x, w, tm):
    bn, ln, _ = x.shape
    return pl.pallas_call(
        _inproj_even_kernel,
        grid=(bn, ln // tm),
        in_specs=[
            pl.BlockSpec((1, tm, D_MODEL), lambda b, t: (b, t, 0)),
            pl.BlockSpec((D_MODEL, EVEN_COLS), lambda b, t: (0, 0), pipeline_mode=pl.Buffered(1)),
        ],
        out_specs=[
            pl.BlockSpec((1, tm, GLA_COLS), lambda b, t: (b, t, 0)),
            pl.BlockSpec((1, tm, SWA_COLS), lambda b, t: (b, t, 0)),
            pl.BlockSpec((1, tm, GD_COLS), lambda b, t: (b, t, 0)),
        ],
        out_shape=[
            jax.ShapeDtypeStruct((bn, ln, GLA_COLS), BF16),
            jax.ShapeDtypeStruct((bn, ln, SWA_COLS), BF16),
            jax.ShapeDtypeStruct((bn, ln, GD_COLS), F32),
        ],
        compiler_params=_params("parallel", "parallel"),
        name="inproj_even",
    )(x, w)


def _gla_kernel(q_ref, k_ref, v_ref, gd_ref, wup_ref, bias_ref, tri_ref, mask_ref, o_ref,
                st_ref, logd_ref, sq_ref, kt_ref, kd_ref, u_ref, et_ref, cum_ref, *, tile):
    d = pl.program_id(0)
    t = pl.program_id(2)
    nchunk = tile // GLA_CHUNK

    @pl.when(t == 0)
    def _():
        st_ref[...] = jnp.zeros_like(st_ref)

    a = _dot(gd_ref[0].astype(BF16), wup_ref[0]) + bias_ref[0]
    logd_ref[...] = ((jnp.minimum(a, 0.0) - jnp.log(1.0 + jnp.exp(-jnp.abs(a)))) * (1.0 / GLA_TAU)).astype(BF16)

    tri = tri_ref[0]
    keep = mask_ref[0] > 0.0
    heads = [slice(h * GLA_DK, (h + 1) * GLA_DK) for h in range(GLA_HEADS)]
    s_cols = [slice(2 * h * GLA_DK, (2 * h + 1) * GLA_DK) for h in range(GLA_HEADS)]
    q_cols = [slice((2 * h + 1) * GLA_DK, (2 * h + 2) * GLA_DK) for h in range(GLA_HEADS)]
    sq_cols = [slice(2 * h * GLA_DK, (2 * h + 2) * GLA_DK) for h in range(GLA_HEADS)]

    for c in range(nchunk):
        rows = slice(c * GLA_CHUNK, (c + 1) * GLA_CHUNK)
        cum_ref[c % 2] = _dot(tri, logd_ref[rows, :])
        for h, hs in enumerate(heads):
            b = cum_ref[c % 2, :GLA_CHUNK, hs]
            etot = jnp.exp(cum_ref[c % 2, GLA_CHUNK:GLA_CHUNK + 1, hs])
            kt = k_ref[0, rows, hs].astype(F32) * jnp.exp(-b)
            sq_ref[rows, q_cols[h]] = (q_ref[0, rows, hs].astype(F32) * jnp.exp(b)).astype(BF16)
            kt_ref[rows, hs] = kt.astype(BF16)
            kd_ref[rows, hs] = (kt * etot).astype(BF16)
            et_ref[c, :, hs] = etot

    for c in range(nchunk):
        rows = slice(c * GLA_CHUNK, (c + 1) * GLA_CHUNK)
        for h, hs in enumerate(heads):
            sq_ref[rows, s_cols[h]] = jnp.where(
                keep, _dot_nt(sq_ref[rows, q_cols[h]], kt_ref[rows, hs]), 0.0).astype(BF16)
            u_ref[c, h] = _dot_tn(v_ref[0, rows, hs], kd_ref[rows, hs])

    for c in range(nchunk):
        cc = c + d * (nchunk - 1 - 2 * c)
        rows = pl.ds(pl.multiple_of(cc * GLA_CHUNK, GLA_CHUNK), GLA_CHUNK)
        for h, hs in enumerate(heads):
            st = st_ref[h]
            vs = jnp.concatenate([v_ref[0, rows, hs], st.T.astype(BF16)], axis=0)
            o_ref[0, 0, rows, hs] = _dot(sq_ref[rows, sq_cols[h]], vs).astype(BF16)
            st_ref[h] = st * et_ref[cc, :, hs] + u_ref[cc, h]


def _gla(gla, gd, wup, bias, tri, mask, tile):
    bn, ln, _ = gla.shape
    nt = ln // tile

    def tok(d, b, t):
        return t + d * (nt - 1 - 2 * t)

    return pl.pallas_call(
        functools.partial(_gla_kernel, tile=tile),
        grid=(2, bn, nt),
        in_specs=[
            pl.BlockSpec((1, tile, GLA_WIDTH), lambda d, b, t: (b, tok(d, b, t), 0)),
            pl.BlockSpec((1, tile, GLA_WIDTH), lambda d, b, t: (b, tok(d, b, t), 1)),
            pl.BlockSpec((1, tile, GLA_WIDTH), lambda d, b, t: (b, tok(d, b, t), 2)),
            pl.BlockSpec((1, tile, GD_COLS), lambda d, b, t: (b, tok(d, b, t), 0)),
            pl.BlockSpec((1, GD_COLS, GLA_WIDTH), lambda d, b, t: (d, 0, 0)),
            pl.BlockSpec((1, 1, GLA_WIDTH), lambda d, b, t: (d, 0, 0)),
            pl.BlockSpec((1, GLA_CHUNK + TOT_ROWS, GLA_CHUNK), lambda d, b, t: (d, 0, 0)),
            pl.BlockSpec((1, GLA_CHUNK, GLA_CHUNK), lambda d, b, t: (d, 0, 0)),
        ],
        out_specs=pl.BlockSpec((1, 1, tile, GLA_WIDTH), lambda d, b, t: (d, b, tok(d, b, t), 0)),
        out_shape=jax.ShapeDtypeStruct((2, bn, ln, GLA_WIDTH), BF16),
        scratch_shapes=[
            pltpu.VMEM((GLA_HEADS, GLA_DK, GLA_DK), F32),
            pltpu.VMEM((tile, GLA_WIDTH), BF16),
            pltpu.VMEM((tile, 2 * GLA_WIDTH), BF16),
            pltpu.VMEM((tile, GLA_WIDTH), BF16),
            pltpu.VMEM((tile, GLA_WIDTH), BF16),
            pltpu.VMEM((tile // GLA_CHUNK, GLA_HEADS, GLA_DK, GLA_DK), F32),
            pltpu.VMEM((tile // GLA_CHUNK, 1, GLA_WIDTH), F32),
            pltpu.VMEM((2, GLA_CHUNK + TOT_ROWS, GLA_WIDTH), F32),
        ],
        compiler_params=_params("arbitrary", "arbitrary", "arbitrary"),
        name="gla_scan",
    )(gla, gla, gla, gd, wup, bias, tri, mask)


def _gla_constants():
    i = np.arange(GLA_CHUNK)[:, None]
    j = np.arange(GLA_CHUNK)[None, :]
    lower = (j <= i).astype(np.float32)
    upper = (j >= i).astype(np.float32)
    tri = np.zeros((2, GLA_CHUNK + TOT_ROWS, GLA_CHUNK), np.float32)
    tri[0, :GLA_CHUNK] = lower
    tri[1, :GLA_CHUNK] = upper
    tri[:, GLA_CHUNK:] = 1.0
    mask = np.stack([(j <= i), (j > i)]).astype(np.float32)
    return jnp.asarray(tri, BF16), jnp.asarray(mask, F32)


def _rel_tables():
    i = np.arange(BLOCK)[:, None]
    j = np.arange(3 * BLOCK)[None, :]
    rel = j - BLOCK - i
    half = REL_BUCKETS // 2
    max_exact = half // 2
    n = np.abs(rel)
    large = max_exact + (np.log(np.maximum(n, 1) / max_exact) / np.log(REL_MAX_DIST / max_exact)
                         * (half - max_exact)).astype(np.int32)
    large = np.minimum(large, half - 1)
    bucket = (rel > 0).astype(np.int32) * half + np.where(n < max_exact, n, large)
    band = np.abs(rel) <= WINDOW
    col = np.broadcast_to(j, rel.shape)
    valid = np.stack([band & (col >= BLOCK), band, band & (col < 2 * BLOCK)])
    return np.ascontiguousarray(bucket.T).astype(np.int32), np.ascontiguousarray(valid.transpose(0, 2, 1)).astype(np.int32)


def _bias_kernel(rb_ref, bucket_ref, valid_ref, out_ref):
    h = pl.program_id(1)
    bucket = bucket_ref[...]
    acc = jnp.zeros(bucket.shape, F32)
    for kk in range(REL_BUCKETS):
        acc = jnp.where(bucket == kk, rb_ref[kk, h] * LOG2E, acc)
    out_ref[0, 0] = jnp.where(valid_ref[0] > 0, acc, NEG_BIG)


def _bias_table(rel_bias):
    bucket, valid = _rel_tables()
    return pl.pallas_call(
        _bias_kernel,
        grid=(3, SWA_HEADS),
        in_specs=[
            pl.BlockSpec(memory_space=pltpu.SMEM),
            pl.BlockSpec((3 * BLOCK, BLOCK), lambda e, h: (0, 0)),
            pl.BlockSpec((1, 3 * BLOCK, BLOCK), lambda e, h: (e, 0, 0)),
        ],
        out_specs=pl.BlockSpec((1, 1, 3 * BLOCK, BLOCK), lambda e, h: (e, h, 0, 0)),
        out_shape=jax.ShapeDtypeStruct((3, SWA_HEADS, 3 * BLOCK, BLOCK), F32),
        compiler_params=_params("arbitrary", "arbitrary"),
        name="swa_bias_table",
    )(rel_bias.astype(F32), jnp.asarray(bucket), jnp.asarray(valid))


def _swa_kernel(sink_ref, q_ref, z_ref, kp_ref, kc_ref, kn_ref, vp_ref, vc_ref, vn_ref, bias_ref, o_ref,
                st_ref, pt_ref, *, nq):
    n = pl.program_id(1)
    nsteps = pl.num_programs(1)
    kcat = jnp.concatenate([kp_ref[0], kc_ref[0], kn_ref[0]], axis=0)
    vcat = jnp.concatenate([vp_ref[0], vc_ref[0], vn_ref[0]], axis=0)
    kswap = jnp.concatenate([kcat[:, SWA_HD:], kcat[:, :SWA_HD]], axis=1)
    lane = lax.broadcasted_iota(jnp.int32, kcat.shape, 1)
    low = lane < SWA_HD
    zero = jnp.zeros_like(kcat)
    kmat = {(0, 0): jnp.where(low, kcat, zero), (0, 1): jnp.where(low, zero, kswap),
            (1, 0): jnp.where(low, kswap, zero), (1, 1): jnp.where(low, zero, kcat)}
    vt = vcat.astype(F32).T.astype(BF16)
    ones = jnp.ones((SWA_ONES_ROWS, 3 * BLOCK), BF16)
    half = lax.broadcasted_iota(jnp.int32, (1, 2 * BLOCK), 1) < BLOCK
    pairs_per_kv = SWA_HEADS // SWA_KV_HEADS // 2
    pairs = [[slice((pairs_per_kv * g + i) * 128, (pairs_per_kv * g + i + 1) * 128) for i in range(pairs_per_kv)]
             for g in range(SWA_KV_HEADS)]
    combos = [(g, e) for g in range(SWA_KV_HEADS) for e in range(2)]
    ncomb = len(combos)
    for qb in range(nq):
        qrows = slice(qb * BLOCK, (qb + 1) * BLOCK)
        keys = slice(qb * BLOCK, (qb + 3) * BLOCK)
        kind = 1
        if qb == 0:
            kind = jnp.where(n == 0, 0, kind)
        if qb == nq - 1:
            kind = jnp.where(n == nsteps - 1, 2, kind)
        slot = (qb % 2) * ncomb
        for c, (g, e) in enumerate(combos):
            h0 = 2 * pairs_per_kv * g + e
            qg = jnp.concatenate([q_ref[0, qrows, ps] for ps in pairs[g]], axis=0)
            st_ref[slot + c] = (_dot_nt(kmat[(g, e)][keys], qg)
                                + jnp.concatenate([bias_ref[kind, h0], bias_ref[kind, h0 + 2]], axis=1))
        stats = []
        for c, (g, e) in enumerate(combos):
            h0 = 2 * pairs_per_kv * g + e
            sink = jnp.where(half, sink_ref[0, h0], sink_ref[0, h0 + 2]) * LOG2E
            m = jnp.maximum(jnp.max(st_ref[slot + c], axis=0, keepdims=True), sink)
            pt_ref[slot + c] = jnp.exp2(st_ref[slot + c] - m).astype(BF16)
            stats.append(jnp.exp2(sink - m))
        outs = {}
        for c, (g, e) in enumerate(combos):
            vaug = jnp.concatenate([vt[g * SWA_HD:(g + 1) * SWA_HD, keys], ones], axis=0)
            ot = _dot(vaug, pt_ref[slot + c])
            outs[(g, e)] = ot[:SWA_HD] * (1.0 / (ot[SWA_HD:SWA_HD + 1] + stats[c]))
        for g in range(SWA_KV_HEADS):
            for i, ps in enumerate(pairs[g]):
                cs = slice(i * BLOCK, (i + 1) * BLOCK)
                o = jnp.concatenate([outs[(g, 0)][:, cs], outs[(g, 1)][:, cs]], axis=0).T
                o_ref[0, qrows, ps] = (o * _silu(z_ref[0, qrows, ps].astype(F32))).astype(BF16)


def _swa(swa, bias, sink, nq):
    bn, ln, _ = swa.shape
    nb = ln // BLOCK
    assert nb % nq == 0 and nb >= 2
    kcol = 2 * SWA_WIDTH // SWA_KVW
    vcol = kcol + 1
    halo = lambda col, fn: pl.BlockSpec((1, BLOCK, SWA_KVW), lambda b, n: (b, fn(n), col))
    prev = lambda n: jnp.maximum(n * nq - 1, 0)
    nxt = lambda n: jnp.minimum((n + 1) * nq, nb - 1)
    return pl.pallas_call(
        functools.partial(_swa_kernel, nq=nq),
        grid=(bn, nb // nq),
        in_specs=[
            pl.BlockSpec(memory_space=pltpu.SMEM),
            pl.BlockSpec((1, nq * BLOCK, SWA_WIDTH), lambda b, n: (b, n, 0)),
            pl.BlockSpec((1, nq * BLOCK, SWA_WIDTH), lambda b, n: (b, n, 1)),
            halo(kcol, prev), pl.BlockSpec((1, nq * BLOCK, SWA_KVW), lambda b, n: (b, n, kcol)), halo(kcol, nxt),
            halo(vcol, prev), pl.BlockSpec((1, nq * BLOCK, SWA_KVW), lambda b, n: (b, n, vcol)), halo(vcol, nxt),
            pl.BlockSpec((3, SWA_HEADS, 3 * BLOCK, BLOCK), lambda b, n: (0, 0, 0, 0)),
        ],
        out_specs=pl.BlockSpec((1, nq * BLOCK, SWA_WIDTH), lambda b, n: (b, n, 0)),
        out_shape=jax.ShapeDtypeStruct((bn, ln, SWA_WIDTH), BF16),
        scratch_shapes=[
            pltpu.VMEM((4 * SWA_KV_HEADS, 3 * BLOCK, 2 * BLOCK), F32),
            pltpu.VMEM((4 * SWA_KV_HEADS, 3 * BLOCK, 2 * BLOCK), BF16),
        ],
        compiler_params=_params("parallel", "arbitrary"),
        name="swa_attention",
    )(sink.reshape(1, SWA_HEADS).astype(F32), swa, swa, swa, swa, swa, swa, swa, swa, bias)


ODD_CHUNK = 256
HALO = 8
LN_ROWS = 256


def _tail_kernel(of_ref, ob_ref, z_ref, yb_ref, x_ref, wa_ref, wb_ref, ng_ref, lg0_ref, lb0_ref,
                 win_ref, cw_ref, wout_ref, lg1_ref, lb1_ref, out_ref,
                 x1_ref, halo_ref, xcat_ref, u_ref, th_ref, mixed_ref, *, tm, nt):
    s = pl.program_id(0)
    last_step = pl.num_programs(0) - 1

    @pl.when(s == 0)
    def _():
        halo_ref[...] = jnp.zeros_like(halo_ref)

    @pl.when(s < last_step)
    def _even_tail():
        slot = lax.rem(s, 2)
        for r0 in range(0, tm, LN_ROWS):
            rs = slice(r0, r0 + LN_ROWS)
            o = of_ref[0, 0, rs, :].astype(F32) + ob_ref[0, 0, rs, :].astype(F32)
            parts = []
            for h in range(GLA_HEADS):
                oh = o[:, h * GLA_DK:(h + 1) * GLA_DK]
                parts.append(oh * lax.rsqrt(jnp.mean(oh * oh, axis=-1, keepdims=True) + NORM_EPS))
            on = jnp.concatenate(parts, axis=1) * ng_ref[...]
            ya = (on * _silu(z_ref[0, rs, :].astype(F32))).astype(BF16)
            sub = _dot(ya, wa_ref[...]) + _dot(yb_ref[0, rs, :], wb_ref[...])
            x1_ref[slot, rs, :] = _layer_norm(DN_ALPHA * x_ref[0, rs, :] + sub, lg0_ref[...], lb0_ref[...])

    @pl.when(s > 0)
    def _odd_layer():
        t = lax.rem(s - 1, nt)
        cur = x1_ref.at[lax.rem(s - 1, 2)]
        nxt = x1_ref.at[lax.rem(s, 2)]
        _odd_body(t, nt, halo_ref, cur, nxt, win_ref, cw_ref, wout_ref, lg1_ref, lb1_ref, out_ref,
                  xcat_ref, u_ref, th_ref, mixed_ref, tm)
        halo_ref[...] = cur[tm - HALO:tm, :]


def _odd_body(t, nt, prev_ref, x_ref, next_ref, win_ref, cw_ref, wout_ref, lg_ref, lb_ref, out_ref,
              xcat_ref, u_ref, th_ref, mixed_ref, tm):
    main = slice(HALO, HALO + tm)
    xcat_ref[...] = jnp.concatenate([prev_ref[...], x_ref[...], next_ref[0:HALO, :]], axis=0).astype(BF16)
    for j in range(CONV_WIDTH // ODD_CHUNK):
        cols = slice(j * ODD_CHUNK, (j + 1) * ODD_CHUNK)
        ub = u_ref.at[j % 2]
        for i in range(4):
            ub[:, i * ODD_CHUNK:(i + 1) * ODD_CHUNK] = _dot(
                xcat_ref[...], win_ref[:, i * CONV_WIDTH + j * ODD_CHUNK:i * CONV_WIDTH + (j + 1) * ODD_CHUNK])
        th_ref[...] = ub[:, ODD_CHUNK:2 * ODD_CHUNK] * ub[:, 2 * ODD_CHUNK:3 * ODD_CHUNK]
        first = pl.ds(HALO - 1, 1)
        last = pl.ds(HALO + tm, 1)
        th_ref[first, :] = jnp.where(t == 0, 0.0, th_ref[first, :])
        th_ref[last, :] = jnp.where(t == nt - 1, 0.0, th_ref[last, :])
        conv = (cw_ref[0:1, cols] * th_ref[HALO - 1:HALO - 1 + tm, :]
                + cw_ref[1:2, cols] * th_ref[main, :]
                + cw_ref[2:3, cols] * th_ref[HALO + 1:HALO + 1 + tm, :])
        mixed_ref[:, cols] = (_silu(ub[main, 3 * ODD_CHUNK:]) * ub[main, :ODD_CHUNK] * conv).astype(BF16)
    for r0 in range(0, tm, LN_ROWS):
        rs = slice(r0, r0 + LN_ROWS)
        acc = _dot(mixed_ref[rs, :], wout_ref[...])
        out_ref[0, rs, :] = _layer_norm(DN_ALPHA * x_ref[rs, :] + acc, lg_ref[...], lb_ref[...])


def _tail(o, gla, yb, x, wa, wb, ng, lg0, lb0, win, cw, wout, lg1, lb1, tm):
    bn, ln, _ = x.shape
    nt = ln // tm
    ntiles = bn * nt
    rows = tm + 2 * HALO

    def tile_in(s):
        g = jnp.minimum(s, ntiles - 1)
        return g // nt, g % nt

    def tile_out(s):
        g = jnp.maximum(s - 1, 0)
        return g // nt, g % nt

    whole = lambda *shape: pl.BlockSpec(shape, lambda s: (0,) * len(shape), pipeline_mode=pl.Buffered(1))
    return pl.pallas_call(
        functools.partial(_tail_kernel, tm=tm, nt=nt),
        grid=(ntiles + 1,),
        in_specs=[
            pl.BlockSpec((1, 1, tm, GLA_WIDTH), lambda s: (0, *tile_in(s), 0)),
            pl.BlockSpec((1, 1, tm, GLA_WIDTH), lambda s: (1, *tile_in(s), 0)),
            pl.BlockSpec((1, tm, GLA_WIDTH), lambda s: (*tile_in(s), 3)),
            pl.BlockSpec((1, tm, SWA_WIDTH), lambda s: (*tile_in(s), 0)),
            pl.BlockSpec((1, tm, D_MODEL), lambda s: (*tile_in(s), 0)),
            whole(GLA_WIDTH, D_MODEL), whole(SWA_WIDTH, D_MODEL),
            whole(1, GLA_WIDTH), whole(1, D_MODEL), whole(1, D_MODEL),
            whole(D_MODEL, 4 * CONV_WIDTH), whole(3, CONV_WIDTH), whole(CONV_WIDTH, D_MODEL),
            whole(1, D_MODEL), whole(1, D_MODEL),
        ],
        out_specs=pl.BlockSpec((1, tm, D_MODEL), lambda s: (*tile_out(s), 0)),
        out_shape=jax.ShapeDtypeStruct((bn, ln, D_MODEL), F32),
        scratch_shapes=[
            pltpu.VMEM((2, tm, D_MODEL), F32),
            pltpu.VMEM((HALO, D_MODEL), F32),
            pltpu.VMEM((rows, D_MODEL), BF16),
            pltpu.VMEM((2, rows, 4 * ODD_CHUNK), F32),
            pltpu.VMEM((rows, ODD_CHUNK), F32),
            pltpu.VMEM((tm, CONV_WIDTH), BF16),
        ],
        compiler_params=_params("arbitrary"),
        name="even_tail_odd_layer",
    )(o, o, gla, yb, x, wa, wb, ng, lg0, lb0, win, cw, wout, lg1, lb1)


def _prep_even(w_in, w_up_f, b_f, w_up_b, b_b, norm_g, w_out):
    qa, ka, va, za, gd, qb, kb, vb, zb = jnp.split(
        w_in, np.cumsum([512, 512, 512, 512, GD_COLS, 512, 128, 128])[:].tolist(), axis=1)
    w = jnp.concatenate([
        qa * (GLA_DK ** -0.5), ka, va, za,
        qb * (SWA_HD ** -0.5 * LOG2E), zb, kb, vb,
        gd, jnp.zeros((D_MODEL, GD_PAD - GD_COLS), w_in.dtype)], axis=1).astype(BF16)
    zr = jnp.zeros_like(w_up_f)
    wup = jnp.stack([jnp.concatenate([w_up_f, zr], axis=0), jnp.concatenate([zr, w_up_b], axis=0)]).astype(BF16)
    bias = jnp.stack([b_f, b_b]).reshape(2, 1, GLA_WIDTH).astype(F32)
    ng = jnp.tile(norm_g.astype(F32), GLA_HEADS).reshape(1, GLA_WIDTH)
    wa = w_out[:GLA_WIDTH].astype(BF16)
    wb = w_out[GLA_WIDTH:].astype(BF16)
    return w, wup, bias, ng, wa, wb


def _prep_odd(w_in, conv_w, w_out):
    win = w_in.astype(BF16)
    cw = conv_w.astype(F32)
    wout = w_out.astype(BF16)
    return win, cw, wout


def _trunk(x, even, odd, bias_tab, sink, ln_g, ln_b, consts, tm=512, tm_in=1024, gla_tile=1024, swa_nq=8):
    w, wup, bias, ng, wa, wb = even
    win, cw, wout = odd
    tri, mask = consts
    gla, swa, gd = _inproj_even(x, w, tm_in)
    o = _gla(gla, gd, wup, bias, tri, mask, gla_tile)
    yb = _swa(swa, bias_tab, sink, swa_nq)
    lg = ln_g.astype(F32).reshape(DEPTH, 1, D_MODEL)
    lb = ln_b.astype(F32).reshape(DEPTH, 1, D_MODEL)
    return _tail(o, gla, yb, x, wa, wb, ng, lg[0], lb[0], win, cw, wout, lg[1], lb[1], tm)


def kernel(x_prompt, x_sample, w_in_even, gla_w_up_fwd, gla_b_fwd, gla_w_up_bwd, gla_b_bwd, gla_norm_g, swa_sink,
           rel_bias, w_out_even, w_in_odd, conv_w, w_out_odd, ln_g, ln_b):
    even = _prep_even(w_in_even[0], gla_w_up_fwd[0], gla_b_fwd[0], gla_w_up_bwd[0], gla_b_bwd[0], gla_norm_g[0],
                      w_out_even[0])
    odd = _prep_odd(w_in_odd[0], conv_w[0], w_out_odd[0])
    bias_tab = _bias_table(rel_bias)
    consts = _gla_constants()
    run = lambda x: _trunk(x, even, odd, bias_tab, swa_sink[0], ln_g, ln_b, consts)
    return (run(x_prompt), run(x_sample))
```

```python
import functools
import math

import numpy as np
import jax
import jax.numpy as jnp
from jax import lax
from jax.experimental import pallas as pl
from jax.experimental.pallas import tpu as pltpu

F32 = jnp.float32
BF16 = jnp.bfloat16

D_MODEL = 1024
DEPTH = 2
GLA_HEADS = 4
GLA_DK = 128
GLA_WIDTH = 512
GLA_RANK = 16
GLA_TAU = 16.0
SWA_HEADS = 8
SWA_KV_HEADS = 2
SWA_HD = 64
SWA_WIDTH = 512
SWA_KVW = 128
WINDOW = 128
BLOCK = 128
REL_BUCKETS = 32
REL_MAX_DIST = 128
CONV_WIDTH = 1024
DN_ALPHA = (2 * DEPTH) ** 0.25
LN_EPS = 1e-5
NORM_EPS = 1e-6
NEG_BIG = -1e30
LOG2E = math.log2(math.e)
SWA_ONES_ROWS = 16

GLA_COLS = 4 * GLA_WIDTH
SWA_COLS = 2 * SWA_WIDTH + 2 * SWA_KVW
GD_COLS = 2 * GLA_RANK
GD_PAD = 128
EVEN_COLS = GLA_COLS + SWA_COLS + GD_PAD

GLA_CHUNK = 128
TOT_ROWS = 16
GLA_SAFE_LOGIT = -8.0
VMEM_LIMIT = 56 * 1024 * 1024


def _dot(a, b):
    return jnp.dot(a, b, preferred_element_type=F32)


def _dot_nt(a, b):
    return lax.dot_general(a, b, (((1,), (1,)), ((), ())), preferred_element_type=F32)


def _dot_tn(a, b):
    return lax.dot_general(a, b, (((0,), (0,)), ((), ())), preferred_element_type=F32)


def _silu(z):
    return z / (1.0 + jnp.exp(-z))


def _layer_norm(y, g, b):
    mu = jnp.mean(y, axis=-1, keepdims=True)
    yc = y - mu
    var = jnp.mean(yc * yc, axis=-1, keepdims=True)
    return yc * lax.rsqrt(var + LN_EPS) * g + b


def _params(*sem):
    return pltpu.CompilerParams(dimension_semantics=sem, vmem_limit_bytes=VMEM_LIMIT)


def _inproj_even_kernel(x_ref, w_ref, gla_ref, swa_ref, gd_ref):
    xb = x_ref[0].astype(BF16)
    for c0 in range(0, GLA_COLS, 512):
        gla_ref[0, :, c0:c0 + 512] = _dot(xb, w_ref[:, c0:c0 + 512]).astype(BF16)
    for c0 in range(0, SWA_COLS, 256):
        swa_ref[0, :, c0:c0 + 256] = _dot(xb, w_ref[:, GLA_COLS + c0:GLA_COLS + c0 + 256]).astype(BF16)
    gd = _dot(xb, w_ref[:, GLA_COLS + SWA_COLS:])
    gd_ref[0] = gd[:, :GD_COLS]


def _inproj_even(x, w, tm):
    bn, ln, _ = x.shape
    return pl.pallas_call(
        _inproj_even_kernel,
        grid=(bn, ln // tm),
        in_specs=[
            pl.BlockSpec((1, tm, D_MODEL), lambda b, t: (b, t, 0)),
            pl.BlockSpec((D_MODEL, EVEN_COLS), lambda b, t: (0, 0), pipeline_mode=pl.Buffered(1)),
        ],
        out_specs=[
            pl.BlockSpec((1, tm, GLA_COLS), lambda b, t: (b, t, 0)),
            pl.BlockSpec((1, tm, SWA_COLS), lambda b, t: (b, t, 0)),
            pl.BlockSpec((1, tm, GD_COLS), lambda b, t: (b, t, 0)),
        ],
        out_shape=[
            jax.ShapeDtypeStruct((bn, ln, GLA_COLS), BF16),
            jax.ShapeDtypeStruct((bn, ln, SWA_COLS), BF16),
            jax.ShapeDtypeStruct((bn, ln, GD_COLS), F32),
        ],
        compiler_params=_params("parallel", "parallel"),
        name="inproj_even",
    )(x, w)


def _gla_kernel(q_ref, k_ref, v_ref, gd_ref, gdn_ref, wup_ref, bias_ref, tri_ref, mask_ref, o_ref,
                st_ref, a_ref, flag_ref, logd_ref, sq_ref, kt_ref, kd_ref, u_ref, et_ref, cum_ref, qf_ref, kf_ref,
                *, tile):
    d = pl.program_id(0)
    t = pl.program_id(2)
    slot = lax.rem(t, 2)

    def gate(gd_blk, sl):
        a = _dot(gd_blk.astype(BF16), wup_ref[0]) + bias_ref[0]
        a_ref[sl] = a
        flag_ref[sl] = (jnp.min(a) < GLA_SAFE_LOGIT).astype(jnp.int32)

    @pl.when(t == 0)
    def _():
        st_ref[...] = jnp.zeros_like(st_ref)
        gate(gd_ref[0], slot)

    unsafe = flag_ref[slot] != 0

    @pl.when(jnp.logical_not(unsafe))
    def _():
        _gla_fast_tile(d, slot, q_ref, k_ref, v_ref, tri_ref, mask_ref, o_ref, st_ref, a_ref, logd_ref, sq_ref,
                       kt_ref, kd_ref, u_ref, et_ref, cum_ref, tile)
        gate(gdn_ref[0], 1 - slot)

    @pl.when(unsafe)
    def _():
        _gla_pairwise_tile(d, slot, q_ref, k_ref, v_ref, tri_ref, o_ref, st_ref, a_ref, logd_ref, cum_ref,
                           qf_ref, kf_ref, tile)
        gate(gdn_ref[0], 1 - slot)


def _log_decay(a):
    return (jnp.minimum(a, 0.0) - jnp.log(1.0 + jnp.exp(-jnp.abs(a)))) * (1.0 / GLA_TAU)


def _gla_pairwise_tile(d, slot, q_ref, k_ref, v_ref, tri_ref, o_ref, st_ref, a_ref, logd_ref, cum_ref,
                       qf_ref, kf_ref, tile):
    nchunk = tile // GLA_CHUNK
    logd_ref[...] = _log_decay(a_ref[slot]).astype(BF16)
    tri = tri_ref[0]
    jrow = lax.broadcasted_iota(jnp.int32, (GLA_CHUNK, GLA_CHUNK), 0)
    icol = lax.broadcasted_iota(jnp.int32, (GLA_CHUNK, GLA_CHUNK), 1)

    def chunk(c, carry):
        cc = c + d * (nchunk - 1 - 2 * c)
        rows = pl.ds(pl.multiple_of(cc * GLA_CHUNK, GLA_CHUNK), GLA_CHUNK)
        cum_ref[0] = _dot(tri, logd_ref[rows, :])
        for h in range(GLA_HEADS):
            hs = slice(h * GLA_DK, (h + 1) * GLA_DK)
            b = cum_ref[0, :GLA_CHUNK, hs]
            tot = cum_ref[0, GLA_CHUNK:GLA_CHUNK + 1, hs]
            qf_ref[...] = q_ref[0, rows, hs].astype(F32)
            kf_ref[...] = k_ref[0, rows, hs].astype(F32)
            v = v_ref[0, rows, hs]

            def pair_rows(g, st_t):
                base = pl.multiple_of(g * 8, 8)
                b8 = cum_ref[0, pl.ds(base, 8), hs]
                q8 = qf_ref[pl.ds(base, 8), :]
                for r in range(8):
                    i = base + r
                    w = jnp.exp(jnp.minimum(b8[r:r + 1] - cum_ref[0, :GLA_CHUNK, hs], 0.0))
                    col = jnp.sum(q8[r:r + 1] * kf_ref[...] * w, axis=1, keepdims=True)
                    valid = (1 - 2 * d) * (jrow - i) <= -d
                    st_t = jnp.where((icol == i) & valid, col, st_t)
                return st_t

            s = lax.fori_loop(0, GLA_CHUNK // 8, pair_rows, jnp.zeros((GLA_CHUNK, GLA_CHUNK), F32)).T.astype(BF16)
            qt = (qf_ref[...] * jnp.exp(b)).astype(BF16)
            kd = (kf_ref[...] * jnp.exp(tot - b)).astype(BF16)
            st = st_ref[h]
            vs = jnp.concatenate([v, st.T.astype(BF16)], axis=0)
            o_ref[0, 0, rows, hs] = _dot(jnp.concatenate([s, qt], axis=1), vs).astype(BF16)
            st_ref[h] = st * jnp.exp(tot) + _dot_tn(v, kd)
        return carry

    lax.fori_loop(0, nchunk, chunk, 0)


def _gla_fast_tile(d, slot, q_ref, k_ref, v_ref, tri_ref, mask_ref, o_ref, st_ref, a_ref, logd_ref, sq_ref,
                   kt_ref, kd_ref, u_ref, et_ref, cum_ref, tile):
    nchunk = tile // GLA_CHUNK
    logd_ref[...] = _log_decay(a_ref[slot]).astype(BF16)

    tri = tri_ref[0]
    keep = mask_ref[0] > 0.0
    heads = [slice(h * GLA_DK, (h + 1) * GLA_DK) for h in range(GLA_HEADS)]
    s_cols = [slice(2 * h * GLA_DK, (2 * h + 1) * GLA_DK) for h in range(GLA_HEADS)]
    q_cols = [slice((2 * h + 1) * GLA_DK, (2 * h + 2) * GLA_DK) for h in range(GLA_HEADS)]
    sq_cols = [slice(2 * h * GLA_DK, (2 * h + 2) * GLA_DK) for h in range(GLA_HEADS)]

    for c in range(nchunk):
        rows = slice(c * GLA_CHUNK, (c + 1) * GLA_CHUNK)
        cum_ref[c % 2] = _dot(tri, logd_ref[rows, :])
        for h, hs in enumerate(heads):
            b = cum_ref[c % 2, :GLA_CHUNK, hs]
            etot = jnp.exp(cum_ref[c % 2, GLA_CHUNK:GLA_CHUNK + 1, hs])
            kt = k_ref[0, rows, hs].astype(F32) * jnp.exp(-b)
            sq_ref[rows, q_cols[h]] = (q_ref[0, rows, hs].astype(F32) * jnp.exp(b)).astype(BF16)
            kt_ref[rows, hs] = kt.astype(BF16)
            kd_ref[rows, hs] = (kt * etot).astype(BF16)
            et_ref[c, :, hs] = etot

    for c in range(nchunk):
        rows = slice(c * GLA_CHUNK, (c + 1) * GLA_CHUNK)
        for h, hs in enumerate(heads):
            sq_ref[rows, s_cols[h]] = jnp.where(
                keep, _dot_nt(sq_ref[rows, q_cols[h]], kt_ref[rows, hs]), 0.0).astype(BF16)
            u_ref[c, h] = _dot_tn(v_ref[0, rows, hs], kd_ref[rows, hs])

    for c in range(nchunk):
        cc = c + d * (nchunk - 1 - 2 * c)
        rows = pl.ds(pl.multiple_of(cc * GLA_CHUNK, GLA_CHUNK), GLA_CHUNK)
        for h, hs in enumerate(heads):
            st = st_ref[h]
            vs = jnp.concatenate([v_ref[0, rows, hs], st.T.astype(BF16)], axis=0)
            o_ref[0, 0, rows, hs] = _dot(sq_ref[rows, sq_cols[h]], vs).astype(BF16)
            st_ref[h] = st * et_ref[cc, :, hs] + u_ref[cc, h]


def _gla(gla, gd, wup, bias, tri, mask, tile):
    bn, ln, _ = gla.shape
    nt = ln // tile

    def tok(d, b, t):
        return t + d * (nt - 1 - 2 * t)

    return pl.pallas_call(
        functools.partial(_gla_kernel, tile=tile),
        grid=(2, bn, nt),
        in_specs=[
            pl.BlockSpec((1, tile, GLA_WIDTH), lambda d, b, t: (b, tok(d, b, t), 0)),
            pl.BlockSpec((1, tile, GLA_WIDTH), lambda d, b, t: (b, tok(d, b, t), 1)),
            pl.BlockSpec((1, tile, GLA_WIDTH), lambda d, b, t: (b, tok(d, b, t), 2)),
            pl.BlockSpec((1, tile, GD_COLS), lambda d, b, t: (b, tok(d, b, t), 0)),
            pl.BlockSpec((1, tile, GD_COLS), lambda d, b, t: (b, tok(d, b, jnp.minimum(t + 1, nt - 1)), 0)),
            pl.BlockSpec((1, GD_COLS, GLA_WIDTH), lambda d, b, t: (d, 0, 0)),
            pl.BlockSpec((1, 1, GLA_WIDTH), lambda d, b, t: (d, 0, 0)),
            pl.BlockSpec((1, GLA_CHUNK + TOT_ROWS, GLA_CHUNK), lambda d, b, t: (d, 0, 0)),
            pl.BlockSpec((1, GLA_CHUNK, GLA_CHUNK), lambda d, b, t: (d, 0, 0)),
        ],
        out_specs=pl.BlockSpec((1, 1, tile, GLA_WIDTH), lambda d, b, t: (d, b, tok(d, b, t), 0)),
        out_shape=jax.ShapeDtypeStruct((2, bn, ln, GLA_WIDTH), BF16),
        scratch_shapes=[
            pltpu.VMEM((GLA_HEADS, GLA_DK, GLA_DK), F32),
            pltpu.VMEM((2, tile, GLA_WIDTH), F32),
            pltpu.SMEM((2,), jnp.int32),
            pltpu.VMEM((tile, GLA_WIDTH), BF16),
            pltpu.VMEM((tile, 2 * GLA_WIDTH), BF16),
            pltpu.VMEM((tile, GLA_WIDTH), BF16),
            pltpu.VMEM((tile, GLA_WIDTH), BF16),
            pltpu.VMEM((tile // GLA_CHUNK, GLA_HEADS, GLA_DK, GLA_DK), F32),
            pltpu.VMEM((tile // GLA_CHUNK, 1, GLA_WIDTH), F32),
            pltpu.VMEM((2, GLA_CHUNK + TOT_ROWS, GLA_WIDTH), F32),
            pltpu.VMEM((GLA_CHUNK, GLA_DK), F32),
            pltpu.VMEM((GLA_CHUNK, GLA_DK), F32),
        ],
        compiler_params=_params("arbitrary", "arbitrary", "arbitrary"),
        name="gla_scan",
    )(gla, gla, gla, gd, gd, wup, bias, tri, mask)


def _gla_constants():
    i = np.arange(GLA_CHUNK)[:, None]
    j = np.arange(GLA_CHUNK)[None, :]
    lower = (j <= i).astype(np.float32)
    upper = (j >= i).astype(np.float32)
    tri = np.zeros((2, GLA_CHUNK + TOT_ROWS, GLA_CHUNK), np.float32)
    tri[0, :GLA_CHUNK] = lower
    tri[1, :GLA_CHUNK] = upper
    tri[:, GLA_CHUNK:] = 1.0
    mask = np.stack([(j <= i), (j > i)]).astype(np.float32)
    return jnp.asarray(tri, BF16), jnp.asarray(mask, F32)


def _rel_tables():
    i = np.arange(BLOCK)[:, None]
    j = np.arange(3 * BLOCK)[None, :]
    rel = j - BLOCK - i
    half = REL_BUCKETS // 2
    max_exact = half // 2
    n = np.abs(rel)
    large = max_exact + (np.log(np.maximum(n, 1) / max_exact) / np.log(REL_MAX_DIST / max_exact)
                         * (half - max_exact)).astype(np.int32)
    large = np.minimum(large, half - 1)
    bucket = (rel > 0).astype(np.int32) * half + np.where(n < max_exact, n, large)
    band = np.abs(rel) <= WINDOW
    col = np.broadcast_to(j, rel.shape)
    valid = np.stack([band & (col >= BLOCK), band, band & (col < 2 * BLOCK)])
    return np.ascontiguousarray(bucket.T).astype(np.int32), np.ascontiguousarray(valid.transpose(0, 2, 1)).astype(np.int32)


def _bias_kernel(rb_ref, bucket_ref, valid_ref, out_ref):
    h = pl.program_id(0)
    bucket = bucket_ref[...]
    acc = jnp.zeros(bucket.shape, F32)
    for kk in range(REL_BUCKETS):
        acc = jnp.where(bucket == kk, rb_ref[kk, h] * LOG2E, acc)
    for kind in range(3):
        out_ref[kind, 0] = jnp.where(valid_ref[kind] > 0, acc, NEG_BIG)


def _bias_table(rel_bias):
    bucket, valid = _rel_tables()
    return pl.pallas_call(
        _bias_kernel,
        grid=(SWA_HEADS,),
        in_specs=[
            pl.BlockSpec(memory_space=pltpu.SMEM),
            pl.BlockSpec((3 * BLOCK, BLOCK), lambda h: (0, 0)),
            pl.BlockSpec((3, 3 * BLOCK, BLOCK), lambda h: (0, 0, 0)),
        ],
        out_specs=pl.BlockSpec((3, 1, 3 * BLOCK, BLOCK), lambda h: (0, h, 0, 0)),
        out_shape=jax.ShapeDtypeStruct((3, SWA_HEADS, 3 * BLOCK, BLOCK), F32),
        compiler_params=_params("arbitrary"),
        name="swa_bias_table",
    )(rel_bias.astype(F32), jnp.asarray(bucket), jnp.asarray(valid))


def _swa_kernel(sink_ref, q_ref, z_ref, kp_ref, kc_ref, kn_ref, vp_ref, vc_ref, vn_ref, bias_ref, o_ref,
                st_ref, pt_ref, *, nq):
    n = pl.program_id(1)
    nsteps = pl.num_programs(1)
    kcat = jnp.concatenate([kp_ref[0], kc_ref[0], kn_ref[0]], axis=0)
    vcat = jnp.concatenate([vp_ref[0], vc_ref[0], vn_ref[0]], axis=0)
    kswap = jnp.concatenate([kcat[:, SWA_HD:], kcat[:, :SWA_HD]], axis=1)
    lane = lax.broadcasted_iota(jnp.int32, kcat.shape, 1)
    low = lane < SWA_HD
    zero = jnp.zeros_like(kcat)
    kmat = {(0, 0): jnp.where(low, kcat, zero), (0, 1): jnp.where(low, zero, kswap),
            (1, 0): jnp.where(low, kswap, zero), (1, 1): jnp.where(low, zero, kcat)}
    vt = vcat.astype(F32).T.astype(BF16)
    ones = jnp.ones((SWA_ONES_ROWS, 3 * BLOCK), BF16)
    half = lax.broadcasted_iota(jnp.int32, (1, 2 * BLOCK), 1) < BLOCK
    pairs_per_kv = SWA_HEADS // SWA_KV_HEADS // 2
    pairs = [[slice((pairs_per_kv * g + i) * 128, (pairs_per_kv * g + i + 1) * 128) for i in range(pairs_per_kv)]
             for g in range(SWA_KV_HEADS)]
    combos = [(g, e) for g in range(SWA_KV_HEADS) for e in range(2)]
    ncomb = len(combos)
    for qb in range(nq):
        qrows = slice(qb * BLOCK, (qb + 1) * BLOCK)
        keys = slice(qb * BLOCK, (qb + 3) * BLOCK)
        kind = 1
        if qb == 0:
            kind = jnp.where(n == 0, 0, kind)
        if qb == nq - 1:
            kind = jnp.where(n == nsteps - 1, 2, kind)
        slot = (qb % 2) * ncomb
        for c, (g, e) in enumerate(combos):
            h0 = 2 * pairs_per_kv * g + e
            qg = jnp.concatenate([q_ref[0, qrows, ps] for ps in pairs[g]], axis=0)
            st_ref[slot + c] = (_dot_nt(kmat[(g, e)][keys], qg)
                                + jnp.concatenate([bias_ref[kind, h0], bias_ref[kind, h0 + 2]], axis=1))
        stats = []
        for c, (g, e) in enumerate(combos):
            h0 = 2 * pairs_per_kv * g + e
            sink = jnp.where(half, sink_ref[0, h0], sink_ref[0, h0 + 2]) * LOG2E
            m = jnp.maximum(jnp.max(st_ref[slot + c], axis=0, keepdims=True), sink)
            pt_ref[slot + c] = jnp.exp2(st_ref[slot + c] - m).astype(BF16)
            stats.append(jnp.exp2(sink - m))
        outs = {}
        for c, (g, e) in enumerate(combos):
            vaug = jnp.concatenate([vt[g * SWA_HD:(g + 1) * SWA_HD, keys], ones], axis=0)
            ot = _dot(vaug, pt_ref[slot + c])
            outs[(g, e)] = ot[:SWA_HD] * (1.0 / (ot[SWA_HD:SWA_HD + 1] + stats[c]))
        for g in range(SWA_KV_HEADS):
            for i, ps in enumerate(pairs[g]):
                cs = slice(i * BLOCK, (i + 1) * BLOCK)
                o = jnp.concatenate([outs[(g, 0)][:, cs], outs[(g, 1)][:, cs]], axis=0).T
                o_ref[0, qrows, ps] = (o * _silu(z_ref[0, qrows, ps].astype(F32))).astype(BF16)


def _swa(swa, bias, sink, nq):
    bn, ln, _ = swa.shape
    nb = ln // BLOCK
    assert nb % nq == 0 and nb >= 2
    kcol = 2 * SWA_WIDTH // SWA_KVW
    vcol = kcol + 1
    halo = lambda col, fn: pl.BlockSpec((1, BLOCK, SWA_KVW), lambda b, n: (b, fn(n), col))
    prev = lambda n: jnp.maximum(n * nq - 1, 0)
    nxt = lambda n: jnp.minimum((n + 1) * nq, nb - 1)
    return pl.pallas_call(
        functools.partial(_swa_kernel, nq=nq),
        grid=(bn, nb // nq),
        in_specs=[
            pl.BlockSpec(memory_space=pltpu.SMEM),
            pl.BlockSpec((1, nq * BLOCK, SWA_WIDTH), lambda b, n: (b, n, 0)),
            pl.BlockSpec((1, nq * BLOCK, SWA_WIDTH), lambda b, n: (b, n, 1)),
            halo(kcol, prev), pl.BlockSpec((1, nq * BLOCK, SWA_KVW), lambda b, n: (b, n, kcol)), halo(kcol, nxt),
            halo(vcol, prev), pl.BlockSpec((1, nq * BLOCK, SWA_KVW), lambda b, n: (b, n, vcol)), halo(vcol, nxt),
            pl.BlockSpec((3, SWA_HEADS, 3 * BLOCK, BLOCK), lambda b, n: (0, 0, 0, 0)),
        ],
        out_specs=pl.BlockSpec((1, nq * BLOCK, SWA_WIDTH), lambda b, n: (b, n, 0)),
        out_shape=jax.ShapeDtypeStruct((bn, ln, SWA_WIDTH), BF16),
        scratch_shapes=[
            pltpu.VMEM((4 * SWA_KV_HEADS, 3 * BLOCK, 2 * BLOCK), F32),
            pltpu.VMEM((4 * SWA_KV_HEADS, 3 * BLOCK, 2 * BLOCK), BF16),
        ],
        compiler_params=_params("parallel", "arbitrary"),
        name="swa_attention",
    )(sink.reshape(1, SWA_HEADS).astype(F32), swa, swa, swa, swa, swa, swa, swa, swa, bias)


ODD_CHUNK = 256
HALO = 8
LN_ROWS = 256


def _tail_kernel(of_ref, ob_ref, z_ref, yb_ref, x_ref, wa_ref, wb_ref, ng_ref, lg0_ref, lb0_ref,
                 win_ref, cw_ref, wout_ref, lg1_ref, lb1_ref, out_ref,
                 x1_ref, halo_ref, xcat_ref, u_ref, th_ref, mixed_ref, *, tm, nt):
    s = pl.program_id(0)
    last_step = pl.num_programs(0) - 1

    @pl.when(s == 0)
    def _():
        halo_ref[...] = jnp.zeros_like(halo_ref)

    @pl.when(s < last_step)
    def _even_tail():
        slot = lax.rem(s, 2)
        for r0 in range(0, tm, LN_ROWS):
            rs = slice(r0, r0 + LN_ROWS)
            o = of_ref[0, 0, rs, :].astype(F32) + ob_ref[0, 0, rs, :].astype(F32)
            parts = []
            for h in range(GLA_HEADS):
                oh = o[:, h * GLA_DK:(h + 1) * GLA_DK]
                parts.append(oh * lax.rsqrt(jnp.mean(oh * oh, axis=-1, keepdims=True) + NORM_EPS))
            on = jnp.concatenate(parts, axis=1) * ng_ref[...]
            ya = (on * _silu(z_ref[0, rs, :].astype(F32))).astype(BF16)
            sub = _dot(ya, wa_ref[...]) + _dot(yb_ref[0, rs, :], wb_ref[...])
            x1_ref[slot, rs, :] = _layer_norm(DN_ALPHA * x_ref[0, rs, :] + sub, lg0_ref[...], lb0_ref[...])

    @pl.when(s > 0)
    def _odd_layer():
        t = lax.rem(s - 1, nt)
        cur = x1_ref.at[lax.rem(s - 1, 2)]
        nxt = x1_ref.at[lax.rem(s, 2)]
        _odd_body(t, nt, halo_ref, cur, nxt, win_ref, cw_ref, wout_ref, lg1_ref, lb1_ref, out_ref,
                  xcat_ref, u_ref, th_ref, mixed_ref, tm)
        halo_ref[...] = cur[tm - HALO:tm, :]


def _odd_body(t, nt, prev_ref, x_ref, next_ref, win_ref, cw_ref, wout_ref, lg_ref, lb_ref, out_ref,
              xcat_ref, u_ref, th_ref, mixed_ref, tm):
    main = slice(HALO, HALO + tm)
    xcat_ref[...] = jnp.concatenate([prev_ref[...], x_ref[...], next_ref[0:HALO, :]], axis=0).astype(BF16)
    for j in range(CONV_WIDTH // ODD_CHUNK):
        cols = slice(j * ODD_CHUNK, (j + 1) * ODD_CHUNK)
        ub = u_ref.at[j % 2]
        for i in range(4):
            ub[:, i * ODD_CHUNK:(i + 1) * ODD_CHUNK] = _dot(
                xcat_ref[...], win_ref[:, i * CONV_WIDTH + j * ODD_CHUNK:i * CONV_WIDTH + (j + 1) * ODD_CHUNK])
        th_ref[...] = ub[:, ODD_CHUNK:2 * ODD_CHUNK] * ub[:, 2 * ODD_CHUNK:3 * ODD_CHUNK]
        first = pl.ds(HALO - 1, 1)
        last = pl.ds(HALO + tm, 1)
        th_ref[first, :] = jnp.where(t == 0, 0.0, th_ref[first, :])
        th_ref[last, :] = jnp.where(t == nt - 1, 0.0, th_ref[last, :])
        conv = (cw_ref[0:1, cols] * th_ref[HALO - 1:HALO - 1 + tm, :]
                + cw_ref[1:2, cols] * th_ref[main, :]
                + cw_ref[2:3, cols] * th_ref[HALO + 1:HALO + 1 + tm, :])
        mixed_ref[:, cols] = (_silu(ub[main, 3 * ODD_CHUNK:]) * ub[main, :ODD_CHUNK] * conv).astype(BF16)
    for r0 in range(0, tm, LN_ROWS):
        rs = slice(r0, r0 + LN_ROWS)
        acc = _dot(mixed_ref[rs, :], wout_ref[...])
        out_ref[0, rs, :] = _layer_norm(DN_ALPHA * x_ref[rs, :] + acc, lg_ref[...], lb_ref[...])


def _tail(o, gla, yb, x, wa, wb, ng, lg0, lb0, win, cw, wout, lg1, lb1, tm):
    bn, ln, _ = x.shape
    nt = ln // tm
    ntiles = bn * nt
    rows = tm + 2 * HALO

    def tile_in(s):
        g = jnp.minimum(s, ntiles - 1)
        return g // nt, g % nt

    def tile_out(s):
        g = jnp.maximum(s - 1, 0)
        return g // nt, g % nt

    whole = lambda *shape: pl.BlockSpec(shape, lambda s: (0,) * len(shape), pipeline_mode=pl.Buffered(1))
    return pl.pallas_call(
        functools.partial(_tail_kernel, tm=tm, nt=nt),
        grid=(ntiles + 1,),
        in_specs=[
            pl.BlockSpec((1, 1, tm, GLA_WIDTH), lambda s: (0, *tile_in(s), 0)),
            pl.BlockSpec((1, 1, tm, GLA_WIDTH), lambda s: (1, *tile_in(s), 0)),
            pl.BlockSpec((1, tm, GLA_WIDTH), lambda s: (*tile_in(s), 3)),
            pl.BlockSpec((1, tm, SWA_WIDTH), lambda s: (*tile_in(s), 0)),
            pl.BlockSpec((1, tm, D_MODEL), lambda s: (*tile_in(s), 0)),
            whole(GLA_WIDTH, D_MODEL), whole(SWA_WIDTH, D_MODEL),
            whole(1, GLA_WIDTH), whole(1, D_MODEL), whole(1, D_MODEL),
            whole(D_MODEL, 4 * CONV_WIDTH), whole(3, CONV_WIDTH), whole(CONV_WIDTH, D_MODEL),
            whole(1, D_MODEL), whole(1, D_MODEL),
        ],
        out_specs=pl.BlockSpec((1, tm, D_MODEL), lambda s: (*tile_out(s), 0)),
        out_shape=jax.ShapeDtypeStruct((bn, ln, D_MODEL), F32),
        scratch_shapes=[
            pltpu.VMEM((2, tm, D_MODEL), F32),
            pltpu.VMEM((HALO, D_MODEL), F32),
            pltpu.VMEM((rows, D_MODEL), BF16),
            pltpu.VMEM((2, rows, 4 * ODD_CHUNK), F32),
            pltpu.VMEM((rows, ODD_CHUNK), F32),
            pltpu.VMEM((tm, CONV_WIDTH), BF16),
        ],
        compiler_params=_params("arbitrary"),
        name="even_tail_odd_layer",
    )(o, o, gla, yb, x, wa, wb, ng, lg0, lb0, win, cw, wout, lg1, lb1)


def _prep_even(w_in, w_up_f, b_f, w_up_b, b_b, norm_g, w_out):
    qa, ka, va, za, gd, qb, kb, vb, zb = jnp.split(
        w_in, np.cumsum([512, 512, 512, 512, GD_COLS, 512, 128, 128])[:].tolist(), axis=1)
    w = jnp.concatenate([
        qa * (GLA_DK ** -0.5), ka, va, za,
        qb * (SWA_HD ** -0.5 * LOG2E), zb, kb, vb,
        gd, jnp.zeros((D_MODEL, GD_PAD - GD_COLS), w_in.dtype)], axis=1).astype(BF16)
    zr = jnp.zeros_like(w_up_f)
    wup = jnp.stack([jnp.concatenate([w_up_f, zr], axis=0), jnp.concatenate([zr, w_up_b], axis=0)]).astype(BF16)
    bias = jnp.stack([b_f, b_b]).reshape(2, 1, GLA_WIDTH).astype(F32)
    ng = jnp.tile(norm_g.astype(F32), GLA_HEADS).reshape(1, GLA_WIDTH)
    wa = w_out[:GLA_WIDTH].astype(BF16)
    wb = w_out[GLA_WIDTH:].astype(BF16)
    return w, wup, bias, ng, wa, wb


def _prep_odd(w_in, conv_w, w_out):
    win = w_in.astype(BF16)
    cw = conv_w.astype(F32)
    wout = w_out.astype(BF16)
    return win, cw, wout


def _trunk(x, even, odd, bias_tab, sink, ln_g, ln_b, consts, tm=512, tm_in=1024, gla_tile=1024, swa_nq=8):
    w, wup, bias, ng, wa, wb = even
    win, cw, wout = odd
    tri, mask = consts
    gla, swa, gd = _inproj_even(x, w, tm_in)
    o = _gla(gla, gd, wup, bias, tri, mask, gla_tile)
    yb = _swa(swa, bias_tab, sink, swa_nq)
    lg = ln_g.astype(F32).reshape(DEPTH, 1, D_MODEL)
    lb = ln_b.astype(F32).reshape(DEPTH, 1, D_MODEL)
    return _tail(o, gla, yb, x, wa, wb, ng, lg[0], lb[0], win, cw, wout, lg[1], lb[1], tm)


def kernel(x_prompt, x_sample, w_in_even, gla_w_up_fwd, gla_b_fwd, gla_w_up_bwd, gla_b_bwd, gla_norm_g, swa_sink,
           rel_bias, w_out_even, w_in_odd, conv_w, w_out_odd, ln_g, ln_b):
    even = _prep_even(w_in_even[0], gla_w_up_fwd[0], gla_b_fwd[0], gla_w_up_bwd[0], gla_b_bwd[0], gla_norm_g[0],
                      w_out_even[0])
    odd = _prep_odd(w_in_odd[0], conv_w[0], w_out_odd[0])
    bias_tab = _bias_table(rel_bias)
    consts = _gla_constants()
    run = lambda x: _trunk(x, even, odd, bias_tab, swa_sink[0], ln_g, ln_b, consts)
    return (run(x_prompt), run(x_sample))
```

```python
import functools
import math

import numpy as np
import jax
import jax.numpy as jnp
from jax import lax
from jax.experimental import pallas as pl
from jax.experimental.pallas import tpu as pltpu

F32 = jnp.float32
BF16 = jnp.bfloat16

D_MODEL = 1024
DEPTH = 2
GLA_HEADS = 4
GLA_DK = 128
GLA_WIDTH = 512
GLA_RANK = 16
GLA_TAU = 16.0
SWA_HEADS = 8
SWA_KV_HEADS = 2
SWA_HD = 64
SWA_WIDTH = 512
SWA_KVW = 128
WINDOW = 128
BLOCK = 128
REL_BUCKETS = 32
REL_MAX_DIST = 128
CONV_WIDTH = 1024
DN_ALPHA = (2 * DEPTH) ** 0.25
LN_EPS = 1e-5
NORM_EPS = 1e-6
NEG_BIG = -1e30
LOG2E = math.log2(math.e)
SWA_ONES_ROWS = 16

GLA_COLS = 4 * GLA_WIDTH
SWA_COLS = 2 * SWA_WIDTH + 2 * SWA_KVW
GD_COLS = 2 * GLA_RANK
GD_PAD = 128
EVEN_COLS = GLA_COLS + SWA_COLS + GD_PAD

GLA_CHUNK = 128
TOT_ROWS = 16
GLA_SAFE_LOGIT = -8.0
VMEM_LIMIT = 56 * 1024 * 1024


def _dot(a, b):
    return jnp.dot(a, b, preferred_element_type=F32)


def _dot_nt(a, b):
    return lax.dot_general(a, b, (((1,), (1,)), ((), ())), preferred_element_type=F32)


def _dot_tn(a, b):
    return lax.dot_general(a, b, (((0,), (0,)), ((), ())), preferred_element_type=F32)


def _silu(z):
    return z / (1.0 + jnp.exp(-z))


def _layer_norm(y, g, b):
    mu = jnp.mean(y, axis=-1, keepdims=True)
    yc = y - mu
    var = jnp.mean(yc * yc, axis=-1, keepdims=True)
    return yc * lax.rsqrt(var + LN_EPS) * g + b


def _params(*sem):
    return pltpu.CompilerParams(dimension_semantics=sem, vmem_limit_bytes=VMEM_LIMIT)


def _inproj_even_kernel(x_ref, w_ref, gla_ref, swa_ref, gd_ref):
    xb = x_ref[0].astype(BF16)
    for c0 in range(0, GLA_COLS, 512):
        gla_ref[0, :, c0:c0 + 512] = _dot(xb, w_ref[:, c0:c0 + 512]).astype(BF16)
    for c0 in range(0, SWA_COLS, 256):
        swa_ref[0, :, c0:c0 + 256] = _dot(xb, w_ref[:, GLA_COLS + c0:GLA_COLS + c0 + 256]).astype(BF16)
    gd_ref[0] = _dot(xb, w_ref[:, GLA_COLS + SWA_COLS:]).astype(BF16)


def _inproj_even(x, w, tm):
    bn, ln, _ = x.shape
    return pl.pallas_call(
        _inproj_even_kernel,
        grid=(bn, ln // tm),
        in_specs=[
            pl.BlockSpec((1, tm, D_MODEL), lambda b, t: (b, t, 0)),
            pl.BlockSpec((D_MODEL, EVEN_COLS), lambda b, t: (0, 0), pipeline_mode=pl.Buffered(1)),
        ],
        out_specs=[
            pl.BlockSpec((1, tm, GLA_COLS), lambda b, t: (b, t, 0)),
            pl.BlockSpec((1, tm, SWA_COLS), lambda b, t: (b, t, 0)),
            pl.BlockSpec((1, tm, GD_PAD), lambda b, t: (b, t, 0)),
        ],
        out_shape=[
            jax.ShapeDtypeStruct((bn, ln, GLA_COLS), BF16),
            jax.ShapeDtypeStruct((bn, ln, SWA_COLS), BF16),
            jax.ShapeDtypeStruct((bn, ln, GD_PAD), BF16),
        ],
        compiler_params=_params("parallel", "parallel"),
        name="inproj_even",
    )(x, w)


def _gla_kernel(q_ref, k_ref, v_ref, gd_ref, gdn_ref, wup_ref, bias_ref, tri_ref, mask_ref, o_ref,
                st_ref, flag_ref, logd_ref, sq_ref, kt_ref, kd_ref, u_ref, et_ref, cum_ref, qf_ref, kf_ref,
                *, tile):
    d = pl.program_id(0)
    t = pl.program_id(2)
    slot = lax.rem(t, 2)

    def gate(gd_blk, sl):
        a = _dot(gd_blk, wup_ref[0]) + bias_ref[0]
        logd_ref[sl] = ((jnp.minimum(a, 0.0) - jnp.log(1.0 + jnp.exp(-jnp.abs(a)))) * (1.0 / GLA_TAU)).astype(BF16)
        flag_ref[sl] = (jnp.min(a) < GLA_SAFE_LOGIT).astype(jnp.int32)

    @pl.when(t == 0)
    def _():
        st_ref[...] = jnp.zeros_like(st_ref)
        gate(gd_ref[0], slot)

    unsafe = flag_ref[slot] != 0

    @pl.when(jnp.logical_not(unsafe))
    def _():
        _gla_fast_tile(d, q_ref, k_ref, v_ref, tri_ref, mask_ref, o_ref, st_ref, logd_ref.at[slot], sq_ref,
                       kt_ref, kd_ref, u_ref, et_ref, cum_ref, tile)
        gate(gdn_ref[0], 1 - slot)

    @pl.when(unsafe)
    def _():
        _gla_pairwise_tile(d, q_ref, k_ref, v_ref, tri_ref, o_ref, st_ref, logd_ref.at[slot], cum_ref,
                           qf_ref, kf_ref, tile)
        gate(gdn_ref[0], 1 - slot)


def _gla_pairwise_tile(d, q_ref, k_ref, v_ref, tri_ref, o_ref, st_ref, logd_ref, cum_ref, qf_ref, kf_ref, tile):
    nchunk = tile // GLA_CHUNK
    tri = tri_ref[0]
    jrow = lax.broadcasted_iota(jnp.int32, (GLA_CHUNK, GLA_CHUNK), 0)
    icol = lax.broadcasted_iota(jnp.int32, (GLA_CHUNK, GLA_CHUNK), 1)

    def chunk(c, carry):
        cc = c + d * (nchunk - 1 - 2 * c)
        rows = pl.ds(pl.multiple_of(cc * GLA_CHUNK, GLA_CHUNK), GLA_CHUNK)
        cum_ref[0] = _dot(tri, logd_ref[rows, :])
        for h in range(GLA_HEADS):
            hs = slice(h * GLA_DK, (h + 1) * GLA_DK)
            b = cum_ref[0, :GLA_CHUNK, hs]
            tot = cum_ref[0, GLA_CHUNK:GLA_CHUNK + 1, hs]
            qf_ref[...] = q_ref[0, rows, hs].astype(F32)
            kf_ref[...] = k_ref[0, rows, hs].astype(F32)
            v = v_ref[0, rows, hs]

            def pair_rows(g, st_t):
                base = pl.multiple_of(g * 8, 8)
                b8 = cum_ref[0, pl.ds(base, 8), hs]
                q8 = qf_ref[pl.ds(base, 8), :]
                for r in range(8):
                    i = base + r
                    w = jnp.exp(jnp.minimum(b8[r:r + 1] - cum_ref[0, :GLA_CHUNK, hs], 0.0))
                    col = jnp.sum(q8[r:r + 1] * kf_ref[...] * w, axis=1, keepdims=True)
                    valid = (1 - 2 * d) * (jrow - i) <= -d
                    st_t = jnp.where((icol == i) & valid, col, st_t)
                return st_t

            s = lax.fori_loop(0, GLA_CHUNK // 8, pair_rows, jnp.zeros((GLA_CHUNK, GLA_CHUNK), F32)).T.astype(BF16)
            qt = (qf_ref[...] * jnp.exp(b)).astype(BF16)
            kd = (kf_ref[...] * jnp.exp(tot - b)).astype(BF16)
            st = st_ref[h]
            vs = jnp.concatenate([v, st.T.astype(BF16)], axis=0)
            o_ref[0, 0, rows, hs] = _dot(jnp.concatenate([s, qt], axis=1), vs).astype(BF16)
            st_ref[h] = st * jnp.exp(tot) + _dot_tn(v, kd)
        return carry

    lax.fori_loop(0, nchunk, chunk, 0)


def _gla_fast_tile(d, q_ref, k_ref, v_ref, tri_ref, mask_ref, o_ref, st_ref, logd_ref, sq_ref,
                   kt_ref, kd_ref, u_ref, et_ref, cum_ref, tile):
    nchunk = tile // GLA_CHUNK
    tri = tri_ref[0]
    keep = mask_ref[0] > 0.0
    heads = [slice(h * GLA_DK, (h + 1) * GLA_DK) for h in range(GLA_HEADS)]
    s_cols = [slice(2 * h * GLA_DK, (2 * h + 1) * GLA_DK) for h in range(GLA_HEADS)]
    q_cols = [slice((2 * h + 1) * GLA_DK, (2 * h + 2) * GLA_DK) for h in range(GLA_HEADS)]
    sq_cols = [slice(2 * h * GLA_DK, (2 * h + 2) * GLA_DK) for h in range(GLA_HEADS)]

    for c in range(nchunk):
        rows = slice(c * GLA_CHUNK, (c + 1) * GLA_CHUNK)
        cum_ref[c % 2] = _dot(tri, logd_ref[rows, :])
        for h, hs in enumerate(heads):
            b = cum_ref[c % 2, :GLA_CHUNK, hs]
            etot = jnp.exp(cum_ref[c % 2, GLA_CHUNK:GLA_CHUNK + 1, hs])
            kt = k_ref[0, rows, hs].astype(F32) * jnp.exp(-b)
            sq_ref[rows, q_cols[h]] = (q_ref[0, rows, hs].astype(F32) * jnp.exp(b)).astype(BF16)
            kt_ref[rows, hs] = kt.astype(BF16)
            kd_ref[rows, hs] = (kt * etot).astype(BF16)
            et_ref[c, :, hs] = etot

    for c in range(nchunk):
        rows = slice(c * GLA_CHUNK, (c + 1) * GLA_CHUNK)
        for h, hs in enumerate(heads):
            sq_ref[rows, s_cols[h]] = jnp.where(
                keep, _dot_nt(sq_ref[rows, q_cols[h]], kt_ref[rows, hs]), 0.0).astype(BF16)
            u_ref[c, h] = _dot_tn(v_ref[0, rows, hs], kd_ref[rows, hs])

    for c in range(nchunk):
        cc = c + d * (nchunk - 1 - 2 * c)
        rows = pl.ds(pl.multiple_of(cc * GLA_CHUNK, GLA_CHUNK), GLA_CHUNK)
        for h, hs in enumerate(heads):
            st = st_ref[h]
            vs = jnp.concatenate([v_ref[0, rows, hs], st.T.astype(BF16)], axis=0)
            o_ref[0, 0, rows, hs] = _dot(sq_ref[rows, sq_cols[h]], vs).astype(BF16)
            st_ref[h] = st * et_ref[cc, :, hs] + u_ref[cc, h]


def _gla(gla, gd, wup, bias, tri, mask, tile):
    bn, ln, _ = gla.shape
    nt = ln // tile

    def tok(d, b, t):
        return t + d * (nt - 1 - 2 * t)

    return pl.pallas_call(
        functools.partial(_gla_kernel, tile=tile),
        grid=(2, bn, nt),
        in_specs=[
            pl.BlockSpec((1, tile, GLA_WIDTH), lambda d, b, t: (b, tok(d, b, t), 0)),
            pl.BlockSpec((1, tile, GLA_WIDTH), lambda d, b, t: (b, tok(d, b, t), 1)),
            pl.BlockSpec((1, tile, GLA_WIDTH), lambda d, b, t: (b, tok(d, b, t), 2)),
            pl.BlockSpec((1, tile, GD_PAD), lambda d, b, t: (b, tok(d, b, t), 0)),
            pl.BlockSpec((1, tile, GD_PAD), lambda d, b, t: (b, tok(d, b, jnp.minimum(t + 1, nt - 1)), 0)),
            pl.BlockSpec((1, GD_PAD, GLA_WIDTH), lambda d, b, t: (d, 0, 0)),
            pl.BlockSpec((1, 1, GLA_WIDTH), lambda d, b, t: (d, 0, 0)),
            pl.BlockSpec((1, GLA_CHUNK + TOT_ROWS, GLA_CHUNK), lambda d, b, t: (d, 0, 0)),
            pl.BlockSpec((1, GLA_CHUNK, GLA_CHUNK), lambda d, b, t: (d, 0, 0)),
        ],
        out_specs=pl.BlockSpec((1, 1, tile, GLA_WIDTH), lambda d, b, t: (d, b, tok(d, b, t), 0)),
        out_shape=jax.ShapeDtypeStruct((2, bn, ln, GLA_WIDTH), BF16),
        scratch_shapes=[
            pltpu.VMEM((GLA_HEADS, GLA_DK, GLA_DK), F32),
            pltpu.SMEM((2,), jnp.int32),
            pltpu.VMEM((2, tile, GLA_WIDTH), BF16),
            pltpu.VMEM((tile, 2 * GLA_WIDTH), BF16),
            pltpu.VMEM((tile, GLA_WIDTH), BF16),
            pltpu.VMEM((tile, GLA_WIDTH), BF16),
            pltpu.VMEM((tile // GLA_CHUNK, GLA_HEADS, GLA_DK, GLA_DK), F32),
            pltpu.VMEM((tile // GLA_CHUNK, 1, GLA_WIDTH), F32),
            pltpu.VMEM((2, GLA_CHUNK + TOT_ROWS, GLA_WIDTH), F32),
            pltpu.VMEM((GLA_CHUNK, GLA_DK), F32),
            pltpu.VMEM((GLA_CHUNK, GLA_DK), F32),
        ],
        compiler_params=_params("arbitrary", "arbitrary", "arbitrary"),
        name="gla_scan",
    )(gla, gla, gla, gd, gd, wup, bias, tri, mask)


def _gla_constants():
    i = np.arange(GLA_CHUNK)[:, None]
    j = np.arange(GLA_CHUNK)[None, :]
    lower = (j <= i).astype(np.float32)
    upper = (j >= i).astype(np.float32)
    tri = np.zeros((2, GLA_CHUNK + TOT_ROWS, GLA_CHUNK), np.float32)
    tri[0, :GLA_CHUNK] = lower
    tri[1, :GLA_CHUNK] = upper
    tri[:, GLA_CHUNK:] = 1.0
    mask = np.stack([(j <= i), (j > i)]).astype(np.float32)
    return jnp.asarray(tri, BF16), jnp.asarray(mask, F32)


def _rel_tables():
    i = np.arange(BLOCK)[:, None]
    j = np.arange(3 * BLOCK)[None, :]
    rel = j - BLOCK - i
    half = REL_BUCKETS // 2
    max_exact = half // 2
    n = np.abs(rel)
    large = max_exact + (np.log(np.maximum(n, 1) / max_exact) / np.log(REL_MAX_DIST / max_exact)
                         * (half - max_exact)).astype(np.int32)
    large = np.minimum(large, half - 1)
    bucket = (rel > 0).astype(np.int32) * half + np.where(n < max_exact, n, large)
    band = np.abs(rel) <= WINDOW
    col = np.broadcast_to(j, rel.shape)
    valid = np.stack([band & (col >= BLOCK), band, band & (col < 2 * BLOCK)])
    return np.ascontiguousarray(bucket.T).astype(np.int32), np.ascontiguousarray(valid.transpose(0, 2, 1)).astype(np.int32)


def _bias_kernel(rb_ref, bucket_ref, valid_ref, out_ref):
    h = pl.program_id(0)
    bucket = bucket_ref[...]
    acc = jnp.zeros(bucket.shape, F32)
    for kk in range(REL_BUCKETS):
        acc = jnp.where(bucket == kk, rb_ref[kk, h] * LOG2E, acc)
    for kind in range(3):
        out_ref[kind, 0] = jnp.where(valid_ref[kind] > 0, acc, NEG_BIG)


def _bias_table(rel_bias):
    bucket, valid = _rel_tables()
    return pl.pallas_call(
        _bias_kernel,
        grid=(SWA_HEADS,),
        in_specs=[
            pl.BlockSpec(memory_space=pltpu.SMEM),
            pl.BlockSpec((3 * BLOCK, BLOCK), lambda h: (0, 0)),
            pl.BlockSpec((3, 3 * BLOCK, BLOCK), lambda h: (0, 0, 0)),
        ],
        out_specs=pl.BlockSpec((3, 1, 3 * BLOCK, BLOCK), lambda h: (0, h, 0, 0)),
        out_shape=jax.ShapeDtypeStruct((3, SWA_HEADS, 3 * BLOCK, BLOCK), F32),
        compiler_params=_params("arbitrary"),
        name="swa_bias_table",
    )(rel_bias.astype(F32), jnp.asarray(bucket), jnp.asarray(valid))


def _swa_kernel(sink_ref, q_ref, z_ref, kp_ref, kc_ref, kn_ref, vp_ref, vc_ref, vn_ref, bias_ref, o_ref,
                st_ref, pt_ref, *, nq):
    n = pl.program_id(1)
    nsteps = pl.num_programs(1)
    kcat = jnp.concatenate([kp_ref[0], kc_ref[0], kn_ref[0]], axis=0)
    vcat = jnp.concatenate([vp_ref[0], vc_ref[0], vn_ref[0]], axis=0)
    kswap = jnp.concatenate([kcat[:, SWA_HD:], kcat[:, :SWA_HD]], axis=1)
    lane = lax.broadcasted_iota(jnp.int32, kcat.shape, 1)
    low = lane < SWA_HD
    zero = jnp.zeros_like(kcat)
    kmat = {(0, 0): jnp.where(low, kcat, zero), (0, 1): jnp.where(low, zero, kswap),
            (1, 0): jnp.where(low, kswap, zero), (1, 1): jnp.where(low, zero, kcat)}
    vt = vcat.astype(F32).T.astype(BF16)
    ones = jnp.ones((SWA_ONES_ROWS, 3 * BLOCK), BF16)
    half = lax.broadcasted_iota(jnp.int32, (1, 2 * BLOCK), 1) < BLOCK
    pairs_per_kv = SWA_HEADS // SWA_KV_HEADS // 2
    pairs = [[slice((pairs_per_kv * g + i) * 128, (pairs_per_kv * g + i + 1) * 128) for i in range(pairs_per_kv)]
             for g in range(SWA_KV_HEADS)]
    combos = [(g, e) for g in range(SWA_KV_HEADS) for e in range(2)]
    ncomb = len(combos)
    for qb in range(nq):
        qrows = slice(qb * BLOCK, (qb + 1) * BLOCK)
        keys = slice(qb * BLOCK, (qb + 3) * BLOCK)
        kind = 1
        if qb == 0:
            kind = jnp.where(n == 0, 0, kind)
        if qb == nq - 1:
            kind = jnp.where(n == nsteps - 1, 2, kind)
        slot = (qb % 2) * ncomb
        for c, (g, e) in enumerate(combos):
            h0 = 2 * pairs_per_kv * g + e
            qg = jnp.concatenate([q_ref[0, qrows, ps] for ps in pairs[g]], axis=0)
            st_ref[slot + c] = (_dot_nt(kmat[(g, e)][keys], qg)
                                + jnp.concatenate([bias_ref[kind, h0], bias_ref[kind, h0 + 2]], axis=1))
        stats = []
        for c, (g, e) in enumerate(combos):
            h0 = 2 * pairs_per_kv * g + e
            sink = jnp.where(half, sink_ref[0, h0], sink_ref[0, h0 + 2]) * LOG2E
            m = jnp.maximum(jnp.max(st_ref[slot + c], axis=0, keepdims=True), sink)
            pt_ref[slot + c] = jnp.exp2(st_ref[slot + c] - m).astype(BF16)
            stats.append(jnp.exp2(sink - m))
        outs = {}
        for c, (g, e) in enumerate(combos):
            vaug = jnp.concatenate([vt[g * SWA_HD:(g + 1) * SWA_HD, keys], ones], axis=0)
            ot = _dot(vaug, pt_ref[slot + c])
            outs[(g, e)] = ot[:SWA_HD] * (1.0 / (ot[SWA_HD:SWA_HD + 1] + stats[c]))
        for g in range(SWA_KV_HEADS):
            for i, ps in enumerate(pairs[g]):
                cs = slice(i * BLOCK, (i + 1) * BLOCK)
                o = jnp.concatenate([outs[(g, 0)][:, cs], outs[(g, 1)][:, cs]], axis=0).T
                o_ref[0, qrows, ps] = (o * _silu(z_ref[0, qrows, ps].astype(F32))).astype(BF16)


def _swa(swa, bias, sink, nq):
    bn, ln, _ = swa.shape
    nb = ln // BLOCK
    assert nb % nq == 0 and nb >= 2
    kcol = 2 * SWA_WIDTH // SWA_KVW
    vcol = kcol + 1
    halo = lambda col, fn: pl.BlockSpec((1, BLOCK, SWA_KVW), lambda b, n: (b, fn(n), col))
    prev = lambda n: jnp.maximum(n * nq - 1, 0)
    nxt = lambda n: jnp.minimum((n + 1) * nq, nb - 1)
    return pl.pallas_call(
        functools.partial(_swa_kernel, nq=nq),
        grid=(bn, nb // nq),
        in_specs=[
            pl.BlockSpec(memory_space=pltpu.SMEM),
            pl.BlockSpec((1, nq * BLOCK, SWA_WIDTH), lambda b, n: (b, n, 0)),
            pl.BlockSpec((1, nq * BLOCK, SWA_WIDTH), lambda b, n: (b, n, 1)),
            halo(kcol, prev), pl.BlockSpec((1, nq * BLOCK, SWA_KVW), lambda b, n: (b, n, kcol)), halo(kcol, nxt),
            halo(vcol, prev), pl.BlockSpec((1, nq * BLOCK, SWA_KVW), lambda b, n: (b, n, vcol)), halo(vcol, nxt),
            pl.BlockSpec((3, SWA_HEADS, 3 * BLOCK, BLOCK), lambda b, n: (0, 0, 0, 0)),
        ],
        out_specs=pl.BlockSpec((1, nq * BLOCK, SWA_WIDTH), lambda b, n: (b, n, 0)),
        out_shape=jax.ShapeDtypeStruct((bn, ln, SWA_WIDTH), BF16),
        scratch_shapes=[
            pltpu.VMEM((4 * SWA_KV_HEADS, 3 * BLOCK, 2 * BLOCK), F32),
            pltpu.VMEM((4 * SWA_KV_HEADS, 3 * BLOCK, 2 * BLOCK), BF16),
        ],
        compiler_params=_params("parallel", "arbitrary"),
        name="swa_attention",
    )(sink.reshape(1, SWA_HEADS).astype(F32), swa, swa, swa, swa, swa, swa, swa, swa, bias)


ODD_CHUNK = 256
HALO = 8
LN_ROWS = 256


def _tail_kernel(of_ref, ob_ref, z_ref, yb_ref, x_ref, wa_ref, wb_ref, ng_ref, lg0_ref, lb0_ref,
                 win_ref, cw_ref, wout_ref, lg1_ref, lb1_ref, out_ref,
                 x1_ref, halo_ref, xcat_ref, u_ref, th_ref, mixed_ref, *, tm, nt):
    s = pl.program_id(0)
    last_step = pl.num_programs(0) - 1

    @pl.when(s == 0)
    def _():
        halo_ref[...] = jnp.zeros_like(halo_ref)

    @pl.when(s < last_step)
    def _even_tail():
        slot = lax.rem(s, 2)
        for r0 in range(0, tm, LN_ROWS):
            rs = slice(r0, r0 + LN_ROWS)
            o = of_ref[0, 0, rs, :].astype(F32) + ob_ref[0, 0, rs, :].astype(F32)
            parts = []
            for h in range(GLA_HEADS):
                oh = o[:, h * GLA_DK:(h + 1) * GLA_DK]
                parts.append(oh * lax.rsqrt(jnp.mean(oh * oh, axis=-1, keepdims=True) + NORM_EPS))
            on = jnp.concatenate(parts, axis=1) * ng_ref[...]
            ya = (on * _silu(z_ref[0, rs, :].astype(F32))).astype(BF16)
            sub = _dot(ya, wa_ref[...]) + _dot(yb_ref[0, rs, :], wb_ref[...])
            x1_ref[slot, rs, :] = _layer_norm(DN_ALPHA * x_ref[0, rs, :] + sub, lg0_ref[...], lb0_ref[...])

    @pl.when(s > 0)
    def _odd_layer():
        t = lax.rem(s - 1, nt)
        cur = x1_ref.at[lax.rem(s - 1, 2)]
        nxt = x1_ref.at[lax.rem(s, 2)]
        _odd_body(t, nt, halo_ref, cur, nxt, win_ref, cw_ref, wout_ref, lg1_ref, lb1_ref, out_ref,
                  xcat_ref, u_ref, th_ref, mixed_ref, tm)
        halo_ref[...] = cur[tm - HALO:tm, :]


def _odd_body(t, nt, prev_ref, x_ref, next_ref, win_ref, cw_ref, wout_ref, lg_ref, lb_ref, out_ref,
              xcat_ref, u_ref, th_ref, mixed_ref, tm):
    main = slice(HALO, HALO + tm)
    xcat_ref[...] = jnp.concatenate([prev_ref[...], x_ref[...], next_ref[0:HALO, :]], axis=0).astype(BF16)
    nj = CONV_WIDTH // ODD_CHUNK

    def in_proj(j):
        for i in range(4):
            u_ref[j % 2, :, i * ODD_CHUNK:(i + 1) * ODD_CHUNK] = _dot(
                xcat_ref[...], win_ref[:, i * CONV_WIDTH + j * ODD_CHUNK:i * CONV_WIDTH + (j + 1) * ODD_CHUNK])

    in_proj(0)
    for j in range(nj):
        cols = slice(j * ODD_CHUNK, (j + 1) * ODD_CHUNK)
        ub = u_ref.at[j % 2]
        if j + 1 < nj:
            in_proj(j + 1)
        th_ref[...] = ub[:, ODD_CHUNK:2 * ODD_CHUNK] * ub[:, 2 * ODD_CHUNK:3 * ODD_CHUNK]
        first = pl.ds(HALO - 1, 1)
        last = pl.ds(HALO + tm, 1)
        th_ref[first, :] = jnp.where(t == 0, 0.0, th_ref[first, :])
        th_ref[last, :] = jnp.where(t == nt - 1, 0.0, th_ref[last, :])
        conv = (cw_ref[0:1, cols] * th_ref[HALO - 1:HALO - 1 + tm, :]
                + cw_ref[1:2, cols] * th_ref[main, :]
                + cw_ref[2:3, cols] * th_ref[HALO + 1:HALO + 1 + tm, :])
        mixed_ref[:, cols] = (_silu(ub[main, 3 * ODD_CHUNK:]) * ub[main, :ODD_CHUNK] * conv).astype(BF16)
    for r0 in range(0, tm, LN_ROWS):
        rs = slice(r0, r0 + LN_ROWS)
        acc = _dot(mixed_ref[rs, :], wout_ref[...])
        out_ref[0, rs, :] = _layer_norm(DN_ALPHA * x_ref[rs, :] + acc, lg_ref[...], lb_ref[...])


def _tail(o, gla, yb, x, wa, wb, ng, lg0, lb0, win, cw, wout, lg1, lb1, tm):
    bn, ln, _ = x.shape
    nt = ln // tm
    ntiles = bn * nt
    rows = tm + 2 * HALO

    def tile_in(s):
        g = jnp.minimum(s, ntiles - 1)
        return g // nt, g % nt

    def tile_out(s):
        g = jnp.maximum(s - 1, 0)
        return g // nt, g % nt

    whole = lambda *shape: pl.BlockSpec(shape, lambda s: (0,) * len(shape), pipeline_mode=pl.Buffered(1))
    return pl.pallas_call(
        functools.partial(_tail_kernel, tm=tm, nt=nt),
        grid=(ntiles + 1,),
        in_specs=[
            pl.BlockSpec((1, 1, tm, GLA_WIDTH), lambda s: (0, *tile_in(s), 0)),
            pl.BlockSpec((1, 1, tm, GLA_WIDTH), lambda s: (1, *tile_in(s), 0)),
            pl.BlockSpec((1, tm, GLA_WIDTH), lambda s: (*tile_in(s), 3)),
            pl.BlockSpec((1, tm, SWA_WIDTH), lambda s: (*tile_in(s), 0)),
            pl.BlockSpec((1, tm, D_MODEL), lambda s: (*tile_in(s), 0)),
            whole(GLA_WIDTH, D_MODEL), whole(SWA_WIDTH, D_MODEL),
            whole(1, GLA_WIDTH), whole(1, D_MODEL), whole(1, D_MODEL),
            whole(D_MODEL, 4 * CONV_WIDTH), whole(3, CONV_WIDTH), whole(CONV_WIDTH, D_MODEL),
            whole(1, D_MODEL), whole(1, D_MODEL),
        ],
        out_specs=pl.BlockSpec((1, tm, D_MODEL), lambda s: (*tile_out(s), 0)),
        out_shape=jax.ShapeDtypeStruct((bn, ln, D_MODEL), F32),
        scratch_shapes=[
            pltpu.VMEM((2, tm, D_MODEL), F32),
            pltpu.VMEM((HALO, D_MODEL), F32),
            pltpu.VMEM((rows, D_MODEL), BF16),
            pltpu.VMEM((2, rows, 4 * ODD_CHUNK), F32),
            pltpu.VMEM((rows, ODD_CHUNK), F32),
            pltpu.VMEM((tm, CONV_WIDTH), BF16),
        ],
        compiler_params=_params("arbitrary"),
        name="even_tail_odd_layer",
    )(o, o, gla, yb, x, wa, wb, ng, lg0, lb0, win, cw, wout, lg1, lb1)


def _prep_even(w_in, w_up_f, b_f, w_up_b, b_b, norm_g, w_out):
    qa, ka, va, za, gd, qb, kb, vb, zb = jnp.split(
        w_in, np.cumsum([512, 512, 512, 512, GD_COLS, 512, 128, 128])[:].tolist(), axis=1)
    w = jnp.concatenate([
        qa * (GLA_DK ** -0.5), ka, va, za,
        qb * (SWA_HD ** -0.5 * LOG2E), zb, kb, vb,
        gd, jnp.zeros((D_MODEL, GD_PAD - GD_COLS), w_in.dtype)], axis=1).astype(BF16)
    zr = jnp.zeros_like(w_up_f)
    zpad = jnp.zeros((GD_PAD - GD_COLS, GLA_WIDTH), w_up_f.dtype)
    wup = jnp.stack([jnp.concatenate([w_up_f, zr, zpad], axis=0),
                     jnp.concatenate([zr, w_up_b, zpad], axis=0)]).astype(BF16)
    bias = jnp.stack([b_f, b_b]).reshape(2, 1, GLA_WIDTH).astype(F32)
    ng = jnp.tile(norm_g.astype(F32), GLA_HEADS).reshape(1, GLA_WIDTH)
    wa = w_out[:GLA_WIDTH].astype(BF16)
    wb = w_out[GLA_WIDTH:].astype(BF16)
    return w, wup, bias, ng, wa, wb


def _prep_odd(w_in, conv_w, w_out):
    win = w_in.astype(BF16)
    cw = conv_w.astype(F32)
    wout = w_out.astype(BF16)
    return win, cw, wout


def _trunk(x, even, odd, bias_tab, sink, ln_g, ln_b, consts, tm=512, tm_in=1024, gla_tile=1024, swa_nq=8):
    w, wup, bias, ng, wa, wb = even
    win, cw, wout = odd
    tri, mask = consts
    gla, swa, gd = _inproj_even(x, w, tm_in)
    o = _gla(gla, gd, wup, bias, tri, mask, gla_tile)
    yb = _swa(swa, bias_tab, sink, swa_nq)
    lg = ln_g.astype(F32).reshape(DEPTH, 1, D_MODEL)
    lb = ln_b.astype(F32).reshape(DEPTH, 1, D_MODEL)
    return _tail(o, gla, yb, x, wa, wb, ng, lg[0], lb[0], win, cw, wout, lg[1], lb[1], tm)


def kernel(x_prompt, x_sample, w_in_even, gla_w_up_fwd, gla_b_fwd, gla_w_up_bwd, gla_b_bwd, gla_norm_g, swa_sink,
           rel_bias, w_out_even, w_in_odd, conv_w, w_out_odd, ln_g, ln_b):
    even = _prep_even(w_in_even[0], gla_w_up_fwd[0], gla_b_fwd[0], gla_w_up_bwd[0], gla_b_bwd[0], gla_norm_g[0],
                      w_out_even[0])
    odd = _prep_odd(w_in_odd[0], conv_w[0], w_out_odd[0])
    bias_tab = _bias_table(rel_bias)
    consts = _gla_constants()
    run = lambda x: _trunk(x, even, odd, bias_tab, swa_sink[0], ln_g, ln_b, consts)
    return (run(x_prompt), run(x_sample))
```

```python
import functools
import math

import numpy as np
import jax
import jax.numpy as jnp
from jax import lax
from jax.experimental import pallas as pl
from jax.experimental.pallas import tpu as pltpu

F32 = jnp.float32
BF16 = jnp.bfloat16

D_MODEL = 1024
DEPTH = 2
GLA_HEADS = 4
GLA_DK = 128
GLA_WIDTH = 512
GLA_RANK = 16
GLA_TAU = 16.0
SWA_HEADS = 8
SWA_KV_HEADS = 2
SWA_HD = 64
SWA_WIDTH = 512
SWA_KVW = 128
WINDOW = 128
BLOCK = 128
REL_BUCKETS = 32
REL_MAX_DIST = 128
CONV_WIDTH = 1024
DN_ALPHA = (2 * DEPTH) ** 0.25
LN_EPS = 1e-5
NORM_EPS = 1e-6
NEG_BIG = -1e30
LOG2E = math.log2(math.e)
LN2 = math.log(2.0)
SWA_ONES_ROWS = 16

GLA_COLS = 4 * GLA_WIDTH
SWA_COLS = 2 * SWA_WIDTH + 2 * SWA_KVW
GD_COLS = 2 * GLA_RANK
GD_PAD = 128
EVEN_COLS = GLA_COLS + SWA_COLS + GD_PAD

GLA_CHUNK = 128
TOT_ROWS = 16
GLA_SAFE_LOGIT = -8.0
VMEM_LIMIT = 56 * 1024 * 1024


def _dot(a, b):
    return jnp.dot(a, b, preferred_element_type=F32)


def _dot_nt(a, b):
    return lax.dot_general(a, b, (((1,), (1,)), ((), ())), preferred_element_type=F32)


def _dot_tn(a, b):
    return lax.dot_general(a, b, (((0,), (0,)), ((), ())), preferred_element_type=F32)


def _silu(z):
    return z / (1.0 + jnp.exp(-z))


def _layer_norm(y, g, b):
    mu = jnp.mean(y, axis=-1, keepdims=True)
    yc = y - mu
    var = jnp.mean(yc * yc, axis=-1, keepdims=True)
    return yc * lax.rsqrt(var + LN_EPS) * g + b


def _params(*sem):
    return pltpu.CompilerParams(dimension_semantics=sem, vmem_limit_bytes=VMEM_LIMIT)


EVEN_OUT_WIDTHS = (GLA_WIDTH,) * 4 + (SWA_WIDTH, SWA_WIDTH, SWA_KVW, SWA_KVW, GD_PAD)


def _inproj_even_kernel(x_ref, w_ref, *out_refs):
    xb = x_ref[0].astype(BF16)
    c0 = 0
    for ref, width in zip(out_refs[:-3], EVEN_OUT_WIDTHS[:-3]):
        ref[0] = _dot(xb, w_ref[:, c0:c0 + width]).astype(BF16)
        c0 += width
    kvg = _dot(xb, w_ref[:, c0:])
    out_refs[-3][0] = kvg[:, :SWA_KVW].astype(BF16)
    out_refs[-2][0] = kvg[:, SWA_KVW:2 * SWA_KVW].astype(BF16)
    one_lane = (lax.broadcasted_iota(jnp.int32, (1, GD_PAD), 1) == GD_COLS).astype(F32)
    out_refs[-1][0] = (kvg[:, 2 * SWA_KVW:] + one_lane).astype(BF16)


def _inproj_even(x, w, tm):
    bn, ln, _ = x.shape
    return pl.pallas_call(
        _inproj_even_kernel,
        grid=(bn, ln // tm),
        in_specs=[
            pl.BlockSpec((1, tm, D_MODEL), lambda b, t: (b, t, 0)),
            pl.BlockSpec((D_MODEL, EVEN_COLS), lambda b, t: (0, 0), pipeline_mode=pl.Buffered(1)),
        ],
        out_specs=[pl.BlockSpec((1, tm, width), lambda b, t: (b, t, 0)) for width in EVEN_OUT_WIDTHS],
        out_shape=[jax.ShapeDtypeStruct((bn, ln, width), BF16) for width in EVEN_OUT_WIDTHS],
        compiler_params=_params("parallel", "parallel"),
        name="inproj_even",
    )(x, w)


def _gla_kernel(q_ref, k_ref, v_ref, gd_ref, gdn_ref, wup_ref, tri_ref, mask_ref, o_ref,
                st_ref, flag_ref, logd_ref, sq_ref, kt_ref, kd_ref, u_ref, et_ref, cum_ref, qf_ref, kf_ref,
                *, tile):
    d = pl.program_id(0)
    t = pl.program_id(2)
    slot = lax.rem(t, 2)

    def gate(gd_blk, sl):
        a2 = _dot(gd_blk, wup_ref[0])
        neg_abs = lax.bitcast_convert_type(
            lax.bitcast_convert_type(a2, jnp.uint32) | jnp.uint32(0x80000000), F32)
        logd_ref[sl] = ((jnp.minimum(a2, 0.0) - jnp.log2(1.0 + jnp.exp2(neg_abs))) * (LN2 / GLA_TAU)).astype(BF16)
        flag_ref[sl] = (jnp.min(a2) < GLA_SAFE_LOGIT * LOG2E).astype(jnp.int32)

    @pl.when(t == 0)
    def _():
        st_ref[...] = jnp.zeros_like(st_ref)
        gate(gd_ref[0], slot)

    unsafe = flag_ref[slot] != 0

    @pl.when(jnp.logical_not(unsafe))
    def _():
        _gla_fast_tile(d, q_ref, k_ref, v_ref, tri_ref, mask_ref, o_ref, st_ref, logd_ref.at[slot], sq_ref,
                       kt_ref, kd_ref, u_ref, et_ref, cum_ref, tile)
        gate(gdn_ref[0], 1 - slot)

    @pl.when(unsafe)
    def _():
        _gla_pairwise_tile(d, q_ref, k_ref, v_ref, tri_ref, o_ref, st_ref, logd_ref.at[slot], cum_ref,
                           qf_ref, kf_ref, tile)
        gate(gdn_ref[0], 1 - slot)


def _gla_pairwise_tile(d, q_ref, k_ref, v_ref, tri_ref, o_ref, st_ref, logd_ref, cum_ref, qf_ref, kf_ref, tile):
    nchunk = tile // GLA_CHUNK
    tri = tri_ref[0]
    jrow = lax.broadcasted_iota(jnp.int32, (GLA_CHUNK, GLA_CHUNK), 0)
    icol = lax.broadcasted_iota(jnp.int32, (GLA_CHUNK, GLA_CHUNK), 1)

    def chunk(c, carry):
        cc = c + d * (nchunk - 1 - 2 * c)
        rows = pl.ds(pl.multiple_of(cc * GLA_CHUNK, GLA_CHUNK), GLA_CHUNK)
        cum_ref[0] = _dot(tri, logd_ref[rows, :])
        for h in range(GLA_HEADS):
            hs = slice(h * GLA_DK, (h + 1) * GLA_DK)
            b = cum_ref[0, :GLA_CHUNK, hs]
            tot = cum_ref[0, GLA_CHUNK:GLA_CHUNK + 1, hs]
            qf_ref[...] = q_ref[0, rows, hs].astype(F32)
            kf_ref[...] = k_ref[0, rows, hs].astype(F32)
            v = v_ref[0, rows, hs]

            def pair_rows(g, st_t):
                base = pl.multiple_of(g * 8, 8)
                b8 = cum_ref[0, pl.ds(base, 8), hs]
                q8 = qf_ref[pl.ds(base, 8), :]
                for r in range(8):
                    i = base + r
                    w = jnp.exp(jnp.minimum(b8[r:r + 1] - cum_ref[0, :GLA_CHUNK, hs], 0.0))
                    col = jnp.sum(q8[r:r + 1] * kf_ref[...] * w, axis=1, keepdims=True)
                    valid = (1 - 2 * d) * (jrow - i) <= -d
                    st_t = jnp.where((icol == i) & valid, col, st_t)
                return st_t

            s = lax.fori_loop(0, GLA_CHUNK // 8, pair_rows, jnp.zeros((GLA_CHUNK, GLA_CHUNK), F32)).T.astype(BF16)
            qt = (qf_ref[...] * jnp.exp(b)).astype(BF16)
            kd = (kf_ref[...] * jnp.exp(tot - b)).astype(BF16)
            st = st_ref[h]
            vs = jnp.concatenate([v, st.T.astype(BF16)], axis=0)
            o_ref[0, 0, rows, hs] = _dot(jnp.concatenate([s, qt], axis=1), vs).astype(BF16)
            st_ref[h] = st * jnp.exp(tot) + _dot_tn(v, kd)
        return carry

    lax.fori_loop(0, nchunk, chunk, 0)


def _gla_fast_tile(d, q_ref, k_ref, v_ref, tri_ref, mask_ref, o_ref, st_ref, logd_ref, sq_ref,
                   kt_ref, kd_ref, u_ref, et_ref, cum_ref, tile):
    nchunk = tile // GLA_CHUNK
    tri = tri_ref[0]
    keep = mask_ref[0] > 0.0
    heads = [slice(h * GLA_DK, (h + 1) * GLA_DK) for h in range(GLA_HEADS)]
    s_cols = [slice(2 * h * GLA_DK, (2 * h + 1) * GLA_DK) for h in range(GLA_HEADS)]
    q_cols = [slice((2 * h + 1) * GLA_DK, (2 * h + 2) * GLA_DK) for h in range(GLA_HEADS)]
    sq_cols = [slice(2 * h * GLA_DK, (2 * h + 2) * GLA_DK) for h in range(GLA_HEADS)]

    for c in range(nchunk):
        rows = slice(c * GLA_CHUNK, (c + 1) * GLA_CHUNK)
        cum_ref[c % 2] = _dot(tri, logd_ref[rows, :])
        for h, hs in enumerate(heads):
            b = cum_ref[c % 2, :GLA_CHUNK, hs]
            etot = jnp.exp(cum_ref[c % 2, GLA_CHUNK:GLA_CHUNK + 1, hs])
            kt = k_ref[0, rows, hs].astype(F32) * jnp.exp(-b)
            sq_ref[rows, q_cols[h]] = (q_ref[0, rows, hs].astype(F32) * jnp.exp(b)).astype(BF16)
            kt_ref[rows, hs] = kt.astype(BF16)
            kd_ref[rows, hs] = (kt * etot).astype(BF16)
            et_ref[c, :, hs] = etot

    for c in range(nchunk):
        rows = slice(c * GLA_CHUNK, (c + 1) * GLA_CHUNK)
        for h, hs in enumerate(heads):
            sq_ref[rows, s_cols[h]] = jnp.where(
                keep, _dot_nt(sq_ref[rows, q_cols[h]], kt_ref[rows, hs]), 0.0).astype(BF16)
            u_ref[c, h] = _dot_tn(v_ref[0, rows, hs], kd_ref[rows, hs])

    for c in range(nchunk):
        cc = c + d * (nchunk - 1 - 2 * c)
        rows = pl.ds(pl.multiple_of(cc * GLA_CHUNK, GLA_CHUNK), GLA_CHUNK)
        for h, hs in enumerate(heads):
            st = st_ref[h]
            vs = jnp.concatenate([v_ref[0, rows, hs], st.T.astype(BF16)], axis=0)
            o_ref[0, 0, rows, hs] = _dot(sq_ref[rows, sq_cols[h]], vs).astype(BF16)
            st_ref[h] = st * et_ref[cc, :, hs] + u_ref[cc, h]


def _gla(q, k, v, gd, wup, tri, mask, tile):
    bn, ln, _ = q.shape
    nt = ln // tile

    def tok(d, b, t):
        return t + d * (nt - 1 - 2 * t)

    return pl.pallas_call(
        functools.partial(_gla_kernel, tile=tile),
        grid=(2, bn, nt),
        in_specs=[
            pl.BlockSpec((1, tile, GLA_WIDTH), lambda d, b, t: (b, tok(d, b, t), 0)),
            pl.BlockSpec((1, tile, GLA_WIDTH), lambda d, b, t: (b, tok(d, b, t), 0)),
            pl.BlockSpec((1, tile, GLA_WIDTH), lambda d, b, t: (b, tok(d, b, t), 0)),
            pl.BlockSpec((1, tile, GD_PAD), lambda d, b, t: (b, tok(d, b, t), 0)),
            pl.BlockSpec((1, tile, GD_PAD), lambda d, b, t: (b, tok(d, b, jnp.minimum(t + 1, nt - 1)), 0)),
            pl.BlockSpec((1, GD_PAD, GLA_WIDTH), lambda d, b, t: (d, 0, 0)),
            pl.BlockSpec((1, GLA_CHUNK + TOT_ROWS, GLA_CHUNK), lambda d, b, t: (d, 0, 0)),
            pl.BlockSpec((1, GLA_CHUNK, GLA_CHUNK), lambda d, b, t: (d, 0, 0)),
        ],
        out_specs=pl.BlockSpec((1, 1, tile, GLA_WIDTH), lambda d, b, t: (d, b, tok(d, b, t), 0)),
        out_shape=jax.ShapeDtypeStruct((2, bn, ln, GLA_WIDTH), BF16),
        scratch_shapes=[
            pltpu.VMEM((GLA_HEADS, GLA_DK, GLA_DK), F32),
            pltpu.SMEM((2,), jnp.int32),
            pltpu.VMEM((2, tile, GLA_WIDTH), BF16),
            pltpu.VMEM((tile, 2 * GLA_WIDTH), BF16),
            pltpu.VMEM((tile, GLA_WIDTH), BF16),
            pltpu.VMEM((tile, GLA_WIDTH), BF16),
            pltpu.VMEM((tile // GLA_CHUNK, GLA_HEADS, GLA_DK, GLA_DK), F32),
            pltpu.VMEM((tile // GLA_CHUNK, 1, GLA_WIDTH), F32),
            pltpu.VMEM((2, GLA_CHUNK + TOT_ROWS, GLA_WIDTH), F32),
            pltpu.VMEM((GLA_CHUNK, GLA_DK), F32),
            pltpu.VMEM((GLA_CHUNK, GLA_DK), F32),
        ],
        compiler_params=_params("arbitrary", "arbitrary", "arbitrary"),
        name="gla_scan",
    )(q, k, v, gd, gd, wup, tri, mask)


def _gla_constants():
    i = np.arange(GLA_CHUNK)[:, None]
    j = np.arange(GLA_CHUNK)[None, :]
    lower = (j <= i).astype(np.float32)
    upper = (j >= i).astype(np.float32)
    tri = np.zeros((2, GLA_CHUNK + TOT_ROWS, GLA_CHUNK), np.float32)
    tri[0, :GLA_CHUNK] = lower
    tri[1, :GLA_CHUNK] = upper
    tri[:, GLA_CHUNK:] = 1.0
    mask = np.stack([(j <= i), (j > i)]).astype(np.float32)
    return jnp.asarray(tri, BF16), jnp.asarray(mask, F32)


def _rel_tables():
    i = np.arange(BLOCK)[:, None]
    j = np.arange(3 * BLOCK)[None, :]
    rel = j - BLOCK - i
    half = REL_BUCKETS // 2
    max_exact = half // 2
    n = np.abs(rel)
    large = max_exact + (np.log(np.maximum(n, 1) / max_exact) / np.log(REL_MAX_DIST / max_exact)
                         * (half - max_exact)).astype(np.int32)
    large = np.minimum(large, half - 1)
    bucket = (rel > 0).astype(np.int32) * half + np.where(n < max_exact, n, large)
    band = np.abs(rel) <= WINDOW
    col = np.broadcast_to(j, rel.shape)
    valid = np.stack([band & (col >= BLOCK), band, band & (col < 2 * BLOCK)])
    return np.ascontiguousarray(bucket.T).astype(np.int32), np.ascontiguousarray(valid.transpose(0, 2, 1)).astype(np.int32)


def _bias_kernel(rb_ref, bucket_ref, valid_ref, out_ref):
    h = pl.program_id(0)
    bucket = bucket_ref[...]
    acc = jnp.zeros(bucket.shape, F32)
    for kk in range(REL_BUCKETS):
        acc = jnp.where(bucket == kk, rb_ref[kk, h] * LOG2E, acc)
    for kind in range(3):
        out_ref[kind, 0] = jnp.where(valid_ref[kind] > 0, acc, NEG_BIG)


def _bias_table(rel_bias):
    bucket, valid = _rel_tables()
    return pl.pallas_call(
        _bias_kernel,
        grid=(SWA_HEADS,),
        in_specs=[
            pl.BlockSpec(memory_space=pltpu.SMEM),
            pl.BlockSpec((3 * BLOCK, BLOCK), lambda h: (0, 0)),
            pl.BlockSpec((3, 3 * BLOCK, BLOCK), lambda h: (0, 0, 0)),
        ],
        out_specs=pl.BlockSpec((3, 1, 3 * BLOCK, BLOCK), lambda h: (0, h, 0, 0)),
        out_shape=jax.ShapeDtypeStruct((3, SWA_HEADS, 3 * BLOCK, BLOCK), F32),
        compiler_params=_params("arbitrary"),
        name="swa_bias_table",
    )(rel_bias.astype(F32), jnp.asarray(bucket), jnp.asarray(valid))


def _swa_kernel(sink_ref, q_ref, z_ref, kp_ref, kc_ref, kn_ref, vp_ref, vc_ref, vn_ref, bias_ref, o_ref,
                st_ref, pt_ref, *, nq):
    n = pl.program_id(1)
    nsteps = pl.num_programs(1)
    kcat = jnp.concatenate([kp_ref[0], kc_ref[0], kn_ref[0]], axis=0)
    vcat = jnp.concatenate([vp_ref[0], vc_ref[0], vn_ref[0]], axis=0)
    kswap = jnp.concatenate([kcat[:, SWA_HD:], kcat[:, :SWA_HD]], axis=1)
    lane = lax.broadcasted_iota(jnp.int32, kcat.shape, 1)
    low = lane < SWA_HD
    zero = jnp.zeros_like(kcat)
    kmat = {(0, 0): jnp.where(low, kcat, zero), (0, 1): jnp.where(low, zero, kswap),
            (1, 0): jnp.where(low, kswap, zero), (1, 1): jnp.where(low, zero, kcat)}
    vt = vcat.astype(F32).T.astype(BF16)
    ones = jnp.ones((SWA_ONES_ROWS, 3 * BLOCK), BF16)
    half = lax.broadcasted_iota(jnp.int32, (1, 2 * BLOCK), 1) < BLOCK
    pairs_per_kv = SWA_HEADS // SWA_KV_HEADS // 2
    pairs = [[slice((pairs_per_kv * g + i) * 128, (pairs_per_kv * g + i + 1) * 128) for i in range(pairs_per_kv)]
             for g in range(SWA_KV_HEADS)]
    combos = [(g, e) for g in range(SWA_KV_HEADS) for e in range(2)]
    ncomb = len(combos)
    for qb in range(nq):
        qrows = slice(qb * BLOCK, (qb + 1) * BLOCK)
        keys = slice(qb * BLOCK, (qb + 3) * BLOCK)
        kind = 1
        if qb == 0:
            kind = jnp.where(n == 0, 0, kind)
        if qb == nq - 1:
            kind = jnp.where(n == nsteps - 1, 2, kind)
        slot = (qb % 2) * ncomb
        for c, (g, e) in enumerate(combos):
            h0 = 2 * pairs_per_kv * g + e
            qg = jnp.concatenate([q_ref[0, qrows, ps] for ps in pairs[g]], axis=0)
            st_ref[slot + c] = (_dot_nt(kmat[(g, e)][keys], qg)
                                + jnp.concatenate([bias_ref[kind, h0], bias_ref[kind, h0 + 2]], axis=1))
        stats = []
        for c, (g, e) in enumerate(combos):
            h0 = 2 * pairs_per_kv * g + e
            sink = jnp.where(half, sink_ref[0, h0], sink_ref[0, h0 + 2]) * LOG2E
            m = jnp.maximum(jnp.max(st_ref[slot + c], axis=0, keepdims=True), sink)
            pt_ref[slot + c] = jnp.exp2(st_ref[slot + c] - m).astype(BF16)
            stats.append(jnp.exp2(sink - m))
        outs = {}
        for c, (g, e) in enumerate(combos):
            vaug = jnp.concatenate([vt[g * SWA_HD:(g + 1) * SWA_HD, keys], ones], axis=0)
            ot = _dot(vaug, pt_ref[slot + c])
            outs[(g, e)] = ot[:SWA_HD] * (1.0 / (ot[SWA_HD:SWA_HD + 1] + stats[c]))
        for g in range(SWA_KV_HEADS):
            for i, ps in enumerate(pairs[g]):
                cs = slice(i * BLOCK, (i + 1) * BLOCK)
                o = jnp.concatenate([outs[(g, 0)][:, cs], outs[(g, 1)][:, cs]], axis=0).T
                o_ref[0, qrows, ps] = (o * _silu(z_ref[0, qrows, ps].astype(F32))).astype(BF16)


def _swa(q, z, k, v, bias, sink, nq):
    bn, ln, _ = q.shape
    nb = ln // BLOCK
    assert nb % nq == 0 and nb >= 2
    prev = pl.BlockSpec((1, BLOCK, SWA_KVW), lambda b, n: (b, jnp.maximum(n * nq - 1, 0), 0))
    own = pl.BlockSpec((1, nq * BLOCK, SWA_KVW), lambda b, n: (b, n, 0))
    nxt = pl.BlockSpec((1, BLOCK, SWA_KVW), lambda b, n: (b, jnp.minimum((n + 1) * nq, nb - 1), 0))
    return pl.pallas_call(
        functools.partial(_swa_kernel, nq=nq),
        grid=(bn, nb // nq),
        in_specs=[
            pl.BlockSpec(memory_space=pltpu.SMEM),
            pl.BlockSpec((1, nq * BLOCK, SWA_WIDTH), lambda b, n: (b, n, 0)),
            pl.BlockSpec((1, nq * BLOCK, SWA_WIDTH), lambda b, n: (b, n, 0)),
            prev, own, nxt,
            prev, own, nxt,
            pl.BlockSpec((3, SWA_HEADS, 3 * BLOCK, BLOCK), lambda b, n: (0, 0, 0, 0)),
        ],
        out_specs=pl.BlockSpec((1, nq * BLOCK, SWA_WIDTH), lambda b, n: (b, n, 0)),
        out_shape=jax.ShapeDtypeStruct((bn, ln, SWA_WIDTH), BF16),
        scratch_shapes=[
            pltpu.VMEM((4 * SWA_KV_HEADS, 3 * BLOCK, 2 * BLOCK), F32),
            pltpu.VMEM((4 * SWA_KV_HEADS, 3 * BLOCK, 2 * BLOCK), BF16),
        ],
        compiler_params=_params("parallel", "arbitrary"),
        name="swa_attention",
    )(sink.reshape(1, SWA_HEADS).astype(F32), q, z, k, k, k, v, v, v, bias)


ODD_CHUNK = 256
HALO = 8
LN_ROWS = 256


def _tail_kernel(of_ref, ob_ref, z_ref, yb_ref, x_ref, wa_ref, wb_ref, ng_ref, lg0_ref, lb0_ref,
                 win_ref, cw_ref, wout_ref, lg1_ref, lb1_ref, out_ref,
                 x1_ref, halo_ref, xcat_ref, u_ref, th_ref, mixed_ref, *, tm, nt):
    s = pl.program_id(0)
    last_step = pl.num_programs(0) - 1

    @pl.when(s == 0)
    def _():
        halo_ref[...] = jnp.zeros_like(halo_ref)

    @pl.when(s < last_step)
    def _even_tail():
        slot = lax.rem(s, 2)
        for r0 in range(0, tm, LN_ROWS):
            rs = slice(r0, r0 + LN_ROWS)
            o = of_ref[0, 0, rs, :].astype(F32) + ob_ref[0, 0, rs, :].astype(F32)
            parts = []
            for h in range(GLA_HEADS):
                oh = o[:, h * GLA_DK:(h + 1) * GLA_DK]
                parts.append(oh * lax.rsqrt(jnp.mean(oh * oh, axis=-1, keepdims=True) + NORM_EPS))
            on = jnp.concatenate(parts, axis=1) * ng_ref[...]
            ya = (on * _silu(z_ref[0, rs, :].astype(F32))).astype(BF16)
            sub = _dot(ya, wa_ref[...]) + _dot(yb_ref[0, rs, :], wb_ref[...])
            x1_ref[slot, rs, :] = _layer_norm(DN_ALPHA * x_ref[0, rs, :] + sub, lg0_ref[...], lb0_ref[...])

    @pl.when(s > 0)
    def _odd_layer():
        t = lax.rem(s - 1, nt)
        cur = x1_ref.at[lax.rem(s - 1, 2)]
        nxt = x1_ref.at[lax.rem(s, 2)]
        _odd_body(t, nt, halo_ref, cur, nxt, win_ref, cw_ref, wout_ref, lg1_ref, lb1_ref, out_ref,
                  xcat_ref, u_ref, th_ref, mixed_ref, tm)
        halo_ref[...] = cur[tm - HALO:tm, :]


def _odd_body(t, nt, prev_ref, x_ref, next_ref, win_ref, cw_ref, wout_ref, lg_ref, lb_ref, out_ref,
              xcat_ref, u_ref, th_ref, mixed_ref, tm):
    main = slice(HALO, HALO + tm)
    xcat_ref[...] = jnp.concatenate([prev_ref[...], x_ref[...], next_ref[0:HALO, :]], axis=0).astype(BF16)
    nj = CONV_WIDTH // ODD_CHUNK

    def in_proj(j):
        for i in range(4):
            u_ref[j % 2, :, i * ODD_CHUNK:(i + 1) * ODD_CHUNK] = _dot(
                xcat_ref[...], win_ref[:, i * CONV_WIDTH + j * ODD_CHUNK:i * CONV_WIDTH + (j + 1) * ODD_CHUNK])

    in_proj(0)
    for j in range(nj):
        cols = slice(j * ODD_CHUNK, (j + 1) * ODD_CHUNK)
        ub = u_ref.at[j % 2]
        if j + 1 < nj:
            in_proj(j + 1)
        th_ref[...] = ub[:, ODD_CHUNK:2 * ODD_CHUNK] * ub[:, 2 * ODD_CHUNK:3 * ODD_CHUNK]
        first = pl.ds(HALO - 1, 1)
        last = pl.ds(HALO + tm, 1)
        th_ref[first, :] = jnp.where(t == 0, 0.0, th_ref[first, :])
        th_ref[last, :] = jnp.where(t == nt - 1, 0.0, th_ref[last, :])
        conv = (cw_ref[0:1, cols] * th_ref[HALO - 1:HALO - 1 + tm, :]
                + cw_ref[1:2, cols] * th_ref[main, :]
                + cw_ref[2:3, cols] * th_ref[HALO + 1:HALO + 1 + tm, :])
        mixed_ref[:, cols] = (_silu(ub[main, 3 * ODD_CHUNK:]) * ub[main, :ODD_CHUNK] * conv).astype(BF16)
    for r0 in range(0, tm, LN_ROWS):
        rs = slice(r0, r0 + LN_ROWS)
        acc = _dot(mixed_ref[rs, :], wout_ref[...])
        out_ref[0, rs, :] = _layer_norm(DN_ALPHA * x_ref[rs, :] + acc, lg_ref[...], lb_ref[...])


def _tail(o, za, yb, x, wa, wb, ng, lg0, lb0, win, cw, wout, lg1, lb1, tm):
    bn, ln, _ = x.shape
    nt = ln // tm
    ntiles = bn * nt
    rows = tm + 2 * HALO

    def tile_in(s):
        g = jnp.minimum(s, ntiles - 1)
        return g // nt, g % nt

    def tile_out(s):
        g = jnp.maximum(s - 1, 0)
        return g // nt, g % nt

    whole = lambda *shape: pl.BlockSpec(shape, lambda s: (0,) * len(shape), pipeline_mode=pl.Buffered(1))
    return pl.pallas_call(
        functools.partial(_tail_kernel, tm=tm, nt=nt),
        grid=(ntiles + 1,),
        in_specs=[
            pl.BlockSpec((1, 1, tm, GLA_WIDTH), lambda s: (0, *tile_in(s), 0)),
            pl.BlockSpec((1, 1, tm, GLA_WIDTH), lambda s: (1, *tile_in(s), 0)),
            pl.BlockSpec((1, tm, GLA_WIDTH), lambda s: (*tile_in(s), 0)),
            pl.BlockSpec((1, tm, SWA_WIDTH), lambda s: (*tile_in(s), 0)),
            pl.BlockSpec((1, tm, D_MODEL), lambda s: (*tile_in(s), 0)),
            whole(GLA_WIDTH, D_MODEL), whole(SWA_WIDTH, D_MODEL),
            whole(1, GLA_WIDTH), whole(1, D_MODEL), whole(1, D_MODEL),
            whole(D_MODEL, 4 * CONV_WIDTH), whole(3, CONV_WIDTH), whole(CONV_WIDTH, D_MODEL),
            whole(1, D_MODEL), whole(1, D_MODEL),
        ],
        out_specs=pl.BlockSpec((1, tm, D_MODEL), lambda s: (*tile_out(s), 0)),
        out_shape=jax.ShapeDtypeStruct((bn, ln, D_MODEL), F32),
        scratch_shapes=[
            pltpu.VMEM((2, tm, D_MODEL), F32),
            pltpu.VMEM((HALO, D_MODEL), F32),
            pltpu.VMEM((rows, D_MODEL), BF16),
            pltpu.VMEM((2, rows, 4 * ODD_CHUNK), F32),
            pltpu.VMEM((rows, ODD_CHUNK), F32),
            pltpu.VMEM((tm, CONV_WIDTH), BF16),
        ],
        compiler_params=_params("arbitrary"),
        name="even_tail_odd_layer",
    )(o, o, za, yb, x, wa, wb, ng, lg0, lb0, win, cw, wout, lg1, lb1)


def _prep_even(w_in, w_up_f, b_f, w_up_b, b_b, norm_g, w_out):
    qa, ka, va, za, gd, qb, kb, vb, zb = jnp.split(
        w_in, np.cumsum([512, 512, 512, 512, GD_COLS, 512, 128, 128])[:].tolist(), axis=1)
    w = jnp.concatenate([
        qa * (GLA_DK ** -0.5), ka, va, za,
        qb * (SWA_HD ** -0.5 * LOG2E), zb, kb, vb,
        gd, jnp.zeros((D_MODEL, GD_PAD - GD_COLS), w_in.dtype)], axis=1).astype(BF16)
    zr = jnp.zeros_like(w_up_f)
    zpad = jnp.zeros((GD_PAD - GD_COLS - 1, GLA_WIDTH), w_up_f.dtype)
    wup = (jnp.stack([jnp.concatenate([w_up_f, zr, b_f[None], zpad], axis=0),
                      jnp.concatenate([zr, w_up_b, b_b[None], zpad], axis=0)]) * LOG2E).astype(BF16)
    ng = jnp.tile(norm_g.astype(F32), GLA_HEADS).reshape(1, GLA_WIDTH)
    wa = w_out[:GLA_WIDTH].astype(BF16)
    wb = w_out[GLA_WIDTH:].astype(BF16)
    return w, wup, ng, wa, wb


def _prep_odd(w_in, conv_w, w_out):
    win = w_in.astype(BF16)
    cw = conv_w.astype(F32)
    wout = w_out.astype(BF16)
    return win, cw, wout


def _trunk(x, even, odd, bias_tab, sink, ln_g, ln_b, consts, tm=512, tm_in=1024, gla_tile=1024, swa_nq=8):
    w, wup, ng, wa, wb = even
    win, cw, wout = odd
    tri, mask = consts
    qa, ka, va, za, qb, zb, kb, vb, gd = _inproj_even(x, w, tm_in)
    o = _gla(qa, ka, va, gd, wup, tri, mask, gla_tile)
    yb = _swa(qb, zb, kb, vb, bias_tab, sink, swa_nq)
    lg = ln_g.astype(F32).reshape(DEPTH, 1, D_MODEL)
    lb = ln_b.astype(F32).reshape(DEPTH, 1, D_MODEL)
    return _tail(o, za, yb, x, wa, wb, ng, lg[0], lb[0], win, cw, wout, lg[1], lb[1], tm)


def kernel(x_prompt, x_sample, w_in_even, gla_w_up_fwd, gla_b_fwd, gla_w_up_bwd, gla_b_bwd, gla_norm_g, swa_sink,
           rel_bias, w_out_even, w_in_odd, conv_w, w_out_odd, ln_g, ln_b):
    even = _prep_even(w_in_even[0], gla_w_up_fwd[0], gla_b_fwd[0], gla_w_up_bwd[0], gla_b_bwd[0], gla_norm_g[0],
                      w_out_even[0])
    odd = _prep_odd(w_in_odd[0], conv_w[0], w_out_odd[0])
    bias_tab = _bias_table(rel_bias)
    consts = _gla_constants()
    run = lambda x: _trunk(x, even, odd, bias_tab, swa_sink[0], ln_g, ln_b, consts)
    return (run(x_prompt), run(x_sample))
```

```python
import functools
import math

import numpy as np
import jax
import jax.numpy as jnp
from jax import lax
from jax.experimental import pallas as pl
from jax.experimental.pallas import tpu as pltpu

F32 = jnp.float32
BF16 = jnp.bfloat16

D_MODEL = 1024
DEPTH = 2
GLA_HEADS = 4
GLA_DK = 128
GLA_WIDTH = 512
GLA_RANK = 16
GLA_TAU = 16.0
SWA_HEADS = 8
SWA_KV_HEADS = 2
SWA_HD = 64
SWA_WIDTH = 512
SWA_KVW = 128
WINDOW = 128
BLOCK = 128
REL_BUCKETS = 32
REL_MAX_DIST = 128
CONV_WIDTH = 1024
DN_ALPHA = (2 * DEPTH) ** 0.25
LN_EPS = 1e-5
NORM_EPS = 1e-6
NEG_BIG = -1e30
LOG2E = math.log2(math.e)
LN2 = math.log(2.0)
SWA_ONES_ROWS = 16

GLA_COLS = 4 * GLA_WIDTH
SWA_COLS = 2 * SWA_WIDTH + 2 * SWA_KVW
GD_COLS = 2 * GLA_RANK
GD_PAD = 128
EVEN_COLS = GLA_COLS + SWA_COLS + GD_PAD

GLA_CHUNK = 128
TOT_ROWS = 16
GLA_SAFE_LOGIT = -8.0
VMEM_LIMIT = 56 * 1024 * 1024


def _dot(a, b):
    return jnp.dot(a, b, preferred_element_type=F32)


def _dot_nt(a, b):
    return lax.dot_general(a, b, (((1,), (1,)), ((), ())), preferred_element_type=F32)


def _dot_tn(a, b):
    return lax.dot_general(a, b, (((0,), (0,)), ((), ())), preferred_element_type=F32)


def _silu(z):
    return z / (1.0 + jnp.exp(-z))


def _layer_norm(y, g, b):
    mu = jnp.mean(y, axis=-1, keepdims=True)
    yc = y - mu
    var = jnp.mean(yc * yc, axis=-1, keepdims=True)
    return yc * lax.rsqrt(var + LN_EPS) * g + b


def _params(*sem):
    return pltpu.CompilerParams(dimension_semantics=sem, vmem_limit_bytes=VMEM_LIMIT)


EVEN_OUT_WIDTHS = (GLA_WIDTH,) * 4 + (SWA_WIDTH, SWA_WIDTH, SWA_KVW, SWA_KVW, GD_PAD)


def _inproj_even_kernel(x_ref, w_ref, *out_refs):
    xb = x_ref[0].astype(BF16)
    c0 = 0
    for ref, width in zip(out_refs[:-3], EVEN_OUT_WIDTHS[:-3]):
        ref[0] = _dot(xb, w_ref[:, c0:c0 + width]).astype(BF16)
        c0 += width
    kvg = _dot(xb, w_ref[:, c0:])
    out_refs[-3][0] = kvg[:, :SWA_KVW].astype(BF16)
    out_refs[-2][0] = kvg[:, SWA_KVW:2 * SWA_KVW].astype(BF16)
    one_lane = (lax.broadcasted_iota(jnp.int32, (1, GD_PAD), 1) == GD_COLS).astype(F32)
    out_refs[-1][0] = (kvg[:, 2 * SWA_KVW:] + one_lane).astype(BF16)


def _inproj_even(x, w, tm):
    bn, ln, _ = x.shape
    return pl.pallas_call(
        _inproj_even_kernel,
        grid=(bn, ln // tm),
        in_specs=[
            pl.BlockSpec((1, tm, D_MODEL), lambda b, t: (b, t, 0)),
            pl.BlockSpec((D_MODEL, EVEN_COLS), lambda b, t: (0, 0), pipeline_mode=pl.Buffered(1)),
        ],
        out_specs=[pl.BlockSpec((1, tm, width), lambda b, t: (b, t, 0)) for width in EVEN_OUT_WIDTHS],
        out_shape=[jax.ShapeDtypeStruct((bn, ln, width), BF16) for width in EVEN_OUT_WIDTHS],
        compiler_params=_params("parallel", "parallel"),
        name="inproj_even",
    )(x, w)


def _gla_kernel(q_ref, k_ref, v_ref, gd_ref, gdn_ref, wup_ref, tri_ref, mask_ref, o_ref,
                st_ref, flag_ref, logd_ref, sq_ref, kt_ref, kd_ref, u_ref, et_ref, cum_ref, qf_ref, kf_ref,
                *, tile):
    d = pl.program_id(0)
    t = pl.program_id(2)
    slot = lax.rem(t, 2)

    def gate(gd_blk, sl):
        a2 = _dot(gd_blk, wup_ref[0])
        neg_abs = lax.bitcast_convert_type(
            lax.bitcast_convert_type(a2, jnp.uint32) | jnp.uint32(0x80000000), F32)
        logd_ref[sl] = ((jnp.minimum(a2, 0.0) - jnp.log2(1.0 + jnp.exp2(neg_abs))) * (LN2 / GLA_TAU)).astype(BF16)
        flag_ref[sl] = (jnp.min(a2) < GLA_SAFE_LOGIT * LOG2E).astype(jnp.int32)

    @pl.when(t == 0)
    def _():
        st_ref[...] = jnp.zeros_like(st_ref)
        gate(gd_ref[0], slot)

    unsafe = flag_ref[slot] != 0

    @pl.when(jnp.logical_not(unsafe))
    def _():
        _gla_fast_tile(d, q_ref, k_ref, v_ref, tri_ref, mask_ref, o_ref, st_ref, logd_ref.at[slot], sq_ref,
                       kt_ref, kd_ref, u_ref, et_ref, cum_ref, tile)
        gate(gdn_ref[0], 1 - slot)

    @pl.when(unsafe)
    def _():
        _gla_pairwise_tile(d, q_ref, k_ref, v_ref, tri_ref, o_ref, st_ref, logd_ref.at[slot], cum_ref,
                           qf_ref, kf_ref, tile)
        gate(gdn_ref[0], 1 - slot)


def _gla_pairwise_tile(d, q_ref, k_ref, v_ref, tri_ref, o_ref, st_ref, logd_ref, cum_ref, qf_ref, kf_ref, tile):
    nchunk = tile // GLA_CHUNK
    tri = tri_ref[0]
    jrow = lax.broadcasted_iota(jnp.int32, (GLA_CHUNK, GLA_CHUNK), 0)
    icol = lax.broadcasted_iota(jnp.int32, (GLA_CHUNK, GLA_CHUNK), 1)

    def chunk(c, carry):
        cc = c + d * (nchunk - 1 - 2 * c)
        rows = pl.ds(pl.multiple_of(cc * GLA_CHUNK, GLA_CHUNK), GLA_CHUNK)
        cum_ref[0] = _dot(tri, logd_ref[rows, :])
        for h in range(GLA_HEADS):
            hs = slice(h * GLA_DK, (h + 1) * GLA_DK)
            b = cum_ref[0, :GLA_CHUNK, hs]
            tot = cum_ref[0, GLA_CHUNK:GLA_CHUNK + 1, hs]
            qf_ref[...] = q_ref[0, rows, hs].astype(F32)
            kf_ref[...] = k_ref[0, rows, hs].astype(F32)
            v = v_ref[0, rows, hs]

            def pair_rows(g, st_t):
                base = pl.multiple_of(g * 8, 8)
                b8 = cum_ref[0, pl.ds(base, 8), hs]
                q8 = qf_ref[pl.ds(base, 8), :]
                for r in range(8):
                    i = base + r
                    w = jnp.exp(jnp.minimum(b8[r:r + 1] - cum_ref[0, :GLA_CHUNK, hs], 0.0))
                    col = jnp.sum(q8[r:r + 1] * kf_ref[...] * w, axis=1, keepdims=True)
                    valid = (1 - 2 * d) * (jrow - i) <= -d
                    st_t = jnp.where((icol == i) & valid, col, st_t)
                return st_t

            s = lax.fori_loop(0, GLA_CHUNK // 8, pair_rows, jnp.zeros((GLA_CHUNK, GLA_CHUNK), F32)).T.astype(BF16)
            qt = (qf_ref[...] * jnp.exp(b)).astype(BF16)
            kd = (kf_ref[...] * jnp.exp(tot - b)).astype(BF16)
            st = st_ref[h]
            vs = jnp.concatenate([v, st.T.astype(BF16)], axis=0)
            o_ref[0, 0, rows, hs] = _dot(jnp.concatenate([s, qt], axis=1), vs).astype(BF16)
            st_ref[h] = st * jnp.exp(tot) + _dot_tn(v, kd)
        return carry

    lax.fori_loop(0, nchunk, chunk, 0)


def _gla_fast_tile(d, q_ref, k_ref, v_ref, tri_ref, mask_ref, o_ref, st_ref, logd_ref, sq_ref,
                   kt_ref, kd_ref, u_ref, et_ref, cum_ref, tile):
    nchunk = tile // GLA_CHUNK
    tri = tri_ref[0]
    keep = mask_ref[0] > 0.0
    heads = [slice(h * GLA_DK, (h + 1) * GLA_DK) for h in range(GLA_HEADS)]
    s_cols = [slice(2 * h * GLA_DK, (2 * h + 1) * GLA_DK) for h in range(GLA_HEADS)]
    q_cols = [slice((2 * h + 1) * GLA_DK, (2 * h + 2) * GLA_DK) for h in range(GLA_HEADS)]
    sq_cols = [slice(2 * h * GLA_DK, (2 * h + 2) * GLA_DK) for h in range(GLA_HEADS)]

    for c in range(nchunk):
        rows = slice(c * GLA_CHUNK, (c + 1) * GLA_CHUNK)
        cum_ref[c % 2] = _dot(tri, logd_ref[rows, :])
        for h, hs in enumerate(heads):
            b = cum_ref[c % 2, :GLA_CHUNK, hs]
            etot = jnp.exp(cum_ref[c % 2, GLA_CHUNK:GLA_CHUNK + 1, hs])
            kt = k_ref[0, rows, hs].astype(F32) * jnp.exp(-b)
            sq_ref[rows, q_cols[h]] = (q_ref[0, rows, hs].astype(F32) * jnp.exp(b)).astype(BF16)
            kt_ref[rows, hs] = kt.astype(BF16)
            kd_ref[rows, hs] = (kt * etot).astype(BF16)
            et_ref[c, :, hs] = etot

    for c in range(nchunk):
        rows = slice(c * GLA_CHUNK, (c + 1) * GLA_CHUNK)
        for h, hs in enumerate(heads):
            sq_ref[rows, s_cols[h]] = jnp.where(
                keep, _dot_nt(sq_ref[rows, q_cols[h]], kt_ref[rows, hs]), 0.0).astype(BF16)
            u_ref[c, h] = _dot_tn(v_ref[0, rows, hs], kd_ref[rows, hs])

    for c in range(nchunk):
        cc = c + d * (nchunk - 1 - 2 * c)
        rows = pl.ds(pl.multiple_of(cc * GLA_CHUNK, GLA_CHUNK), GLA_CHUNK)
        for h, hs in enumerate(heads):
            st = st_ref[h]
            vs = jnp.concatenate([v_ref[0, rows, hs], st.T.astype(BF16)], axis=0)
            o_ref[0, 0, rows, hs] = _dot(sq_ref[rows, sq_cols[h]], vs).astype(BF16)
            st_ref[h] = st * et_ref[cc, :, hs] + u_ref[cc, h]


def _gla(q, k, v, gd, wup, tri, mask, tile):
    bn, ln, _ = q.shape
    nt = ln // tile

    def tok(d, b, t):
        return t + d * (nt - 1 - 2 * t)

    return pl.pallas_call(
        functools.partial(_gla_kernel, tile=tile),
        grid=(2, bn, nt),
        in_specs=[
            pl.BlockSpec((1, tile, GLA_WIDTH), lambda d, b, t: (b, tok(d, b, t), 0)),
            pl.BlockSpec((1, tile, GLA_WIDTH), lambda d, b, t: (b, tok(d, b, t), 0)),
            pl.BlockSpec((1, tile, GLA_WIDTH), lambda d, b, t: (b, tok(d, b, t), 0)),
            pl.BlockSpec((1, tile, GD_PAD), lambda d, b, t: (b, tok(d, b, t), 0)),
            pl.BlockSpec((1, tile, GD_PAD), lambda d, b, t: (b, tok(d, b, jnp.minimum(t + 1, nt - 1)), 0)),
            pl.BlockSpec((1, GD_PAD, GLA_WIDTH), lambda d, b, t: (d, 0, 0)),
            pl.BlockSpec((1, GLA_CHUNK + TOT_ROWS, GLA_CHUNK), lambda d, b, t: (d, 0, 0)),
            pl.BlockSpec((1, GLA_CHUNK, GLA_CHUNK), lambda d, b, t: (d, 0, 0)),
        ],
        out_specs=pl.BlockSpec((1, 1, tile, GLA_WIDTH), lambda d, b, t: (d, b, tok(d, b, t), 0)),
        out_shape=jax.ShapeDtypeStruct((2, bn, ln, GLA_WIDTH), BF16),
        scratch_shapes=[
            pltpu.VMEM((GLA_HEADS, GLA_DK, GLA_DK), F32),
            pltpu.SMEM((2,), jnp.int32),
            pltpu.VMEM((2, tile, GLA_WIDTH), BF16),
            pltpu.VMEM((tile, 2 * GLA_WIDTH), BF16),
            pltpu.VMEM((tile, GLA_WIDTH), BF16),
            pltpu.VMEM((tile, GLA_WIDTH), BF16),
            pltpu.VMEM((tile // GLA_CHUNK, GLA_HEADS, GLA_DK, GLA_DK), F32),
            pltpu.VMEM((tile // GLA_CHUNK, 1, GLA_WIDTH), F32),
            pltpu.VMEM((2, GLA_CHUNK + TOT_ROWS, GLA_WIDTH), F32),
            pltpu.VMEM((GLA_CHUNK, GLA_DK), F32),
            pltpu.VMEM((GLA_CHUNK, GLA_DK), F32),
        ],
        compiler_params=_params("arbitrary", "arbitrary", "arbitrary"),
        name="gla_scan",
    )(q, k, v, gd, gd, wup, tri, mask)


def _gla_constants():
    i = np.arange(GLA_CHUNK)[:, None]
    j = np.arange(GLA_CHUNK)[None, :]
    lower = (j <= i).astype(np.float32)
    upper = (j >= i).astype(np.float32)
    tri = np.zeros((2, GLA_CHUNK + TOT_ROWS, GLA_CHUNK), np.float32)
    tri[0, :GLA_CHUNK] = lower
    tri[1, :GLA_CHUNK] = upper
    tri[:, GLA_CHUNK:] = 1.0
    mask = np.stack([(j <= i), (j > i)]).astype(np.float32)
    return jnp.asarray(tri, BF16), jnp.asarray(mask, F32)


def _rel_tables():
    i = np.arange(BLOCK)[:, None]
    j = np.arange(3 * BLOCK)[None, :]
    rel = j - BLOCK - i
    half = REL_BUCKETS // 2
    max_exact = half // 2
    n = np.abs(rel)
    large = max_exact + (np.log(np.maximum(n, 1) / max_exact) / np.log(REL_MAX_DIST / max_exact)
                         * (half - max_exact)).astype(np.int32)
    large = np.minimum(large, half - 1)
    bucket = (rel > 0).astype(np.int32) * half + np.where(n < max_exact, n, large)
    band = np.abs(rel) <= WINDOW
    col = np.broadcast_to(j, rel.shape)
    valid = np.stack([band & (col >= BLOCK), band, band & (col < 2 * BLOCK)])
    return np.ascontiguousarray(bucket.T).astype(np.int32), np.ascontiguousarray(valid.transpose(0, 2, 1)).astype(np.int32)


def _bias_kernel(rb_ref, bucket_ref, valid_ref, out_ref):
    h = pl.program_id(0)
    bucket = bucket_ref[...]
    acc = jnp.zeros(bucket.shape, F32)
    for kk in range(REL_BUCKETS):
        acc = jnp.where(bucket == kk, rb_ref[kk, h] * LOG2E, acc)
    for kind in range(3):
        out_ref[kind, 0] = jnp.where(valid_ref[kind] > 0, acc, NEG_BIG)


def _bias_table(rel_bias):
    bucket, valid = _rel_tables()
    return pl.pallas_call(
        _bias_kernel,
        grid=(SWA_HEADS,),
        in_specs=[
            pl.BlockSpec(memory_space=pltpu.SMEM),
            pl.BlockSpec((3 * BLOCK, BLOCK), lambda h: (0, 0)),
            pl.BlockSpec((3, 3 * BLOCK, BLOCK), lambda h: (0, 0, 0)),
        ],
        out_specs=pl.BlockSpec((3, 1, 3 * BLOCK, BLOCK), lambda h: (0, h, 0, 0)),
        out_shape=jax.ShapeDtypeStruct((3, SWA_HEADS, 3 * BLOCK, BLOCK), F32),
        compiler_params=_params("arbitrary"),
        name="swa_bias_table",
    )(rel_bias.astype(F32), jnp.asarray(bucket), jnp.asarray(valid))


def _swa_kernel(sink_ref, q_ref, z_ref, kp_ref, kc_ref, kn_ref, vp_ref, vc_ref, vn_ref, bias_ref, o_ref,
                st_ref, pt_ref, *, nq):
    n = pl.program_id(1)
    nsteps = pl.num_programs(1)
    kcat = jnp.concatenate([kp_ref[0], kc_ref[0], kn_ref[0]], axis=0)
    vcat = jnp.concatenate([vp_ref[0], vc_ref[0], vn_ref[0]], axis=0)
    kswap = jnp.concatenate([kcat[:, SWA_HD:], kcat[:, :SWA_HD]], axis=1)
    lane = lax.broadcasted_iota(jnp.int32, kcat.shape, 1)
    low = lane < SWA_HD
    zero = jnp.zeros_like(kcat)
    kmat = {(0, 0): jnp.where(low, kcat, zero), (0, 1): jnp.where(low, zero, kswap),
            (1, 0): jnp.where(low, kswap, zero), (1, 1): jnp.where(low, zero, kcat)}
    vt = vcat.astype(F32).T.astype(BF16)
    ones = jnp.ones((SWA_ONES_ROWS, 3 * BLOCK), BF16)
    half = lax.broadcasted_iota(jnp.int32, (1, 2 * BLOCK), 1) < BLOCK
    pairs_per_kv = SWA_HEADS // SWA_KV_HEADS // 2
    pairs = [[slice((pairs_per_kv * g + i) * 128, (pairs_per_kv * g + i + 1) * 128) for i in range(pairs_per_kv)]
             for g in range(SWA_KV_HEADS)]
    combos = [(g, e) for g in range(SWA_KV_HEADS) for e in range(2)]
    ncomb = len(combos)
    for qb in range(nq):
        qrows = slice(qb * BLOCK, (qb + 1) * BLOCK)
        keys = slice(qb * BLOCK, (qb + 3) * BLOCK)
        kind = 1
        if qb == 0:
            kind = jnp.where(n == 0, 0, kind)
        if qb == nq - 1:
            kind = jnp.where(n == nsteps - 1, 2, kind)
        slot = (qb % 2) * ncomb
        for c, (g, e) in enumerate(combos):
            h0 = 2 * pairs_per_kv * g + e
            qg = jnp.concatenate([q_ref[0, qrows, ps] for ps in pairs[g]], axis=0)
            st_ref[slot + c] = (_dot_nt(kmat[(g, e)][keys], qg)
                                + jnp.concatenate([bias_ref[kind, h0], bias_ref[kind, h0 + 2]], axis=1))
        stats = []
        for c, (g, e) in enumerate(combos):
            h0 = 2 * pairs_per_kv * g + e
            sink = jnp.where(half, sink_ref[0, h0], sink_ref[0, h0 + 2]) * LOG2E
            m = jnp.maximum(jnp.max(st_ref[slot + c], axis=0, keepdims=True), sink)
            pt_ref[slot + c] = jnp.exp2(st_ref[slot + c] - m).astype(BF16)
            stats.append(jnp.exp2(sink - m))
        outs = {}
        for c, (g, e) in enumerate(combos):
            vaug = jnp.concatenate([vt[g * SWA_HD:(g + 1) * SWA_HD, keys], ones], axis=0)
            ot = _dot(vaug, pt_ref[slot + c])
            outs[(g, e)] = ot[:SWA_HD] * (1.0 / (ot[SWA_HD:SWA_HD + 1] + stats[c]))
        for g in range(SWA_KV_HEADS):
            for i, ps in enumerate(pairs[g]):
                cs = slice(i * BLOCK, (i + 1) * BLOCK)
                o = jnp.concatenate([outs[(g, 0)][:, cs], outs[(g, 1)][:, cs]], axis=0).T
                o_ref[0, qrows, ps] = (o * _silu(z_ref[0, qrows, ps].astype(F32))).astype(BF16)


def _swa(q, z, k, v, bias, sink, nq):
    bn, ln, _ = q.shape
    nb = ln // BLOCK
    assert nb % nq == 0 and nb >= 2
    prev = pl.BlockSpec((1, BLOCK, SWA_KVW), lambda b, n: (b, jnp.maximum(n * nq - 1, 0), 0))
    own = pl.BlockSpec((1, nq * BLOCK, SWA_KVW), lambda b, n: (b, n, 0))
    nxt = pl.BlockSpec((1, BLOCK, SWA_KVW), lambda b, n: (b, jnp.minimum((n + 1) * nq, nb - 1), 0))
    return pl.pallas_call(
        functools.partial(_swa_kernel, nq=nq),
        grid=(bn, nb // nq),
        in_specs=[
            pl.BlockSpec(memory_space=pltpu.SMEM),
            pl.BlockSpec((1, nq * BLOCK, SWA_WIDTH), lambda b, n: (b, n, 0)),
            pl.BlockSpec((1, nq * BLOCK, SWA_WIDTH), lambda b, n: (b, n, 0)),
            prev, own, nxt,
            prev, own, nxt,
            pl.BlockSpec((3, SWA_HEADS, 3 * BLOCK, BLOCK), lambda b, n: (0, 0, 0, 0)),
        ],
        out_specs=pl.BlockSpec((1, nq * BLOCK, SWA_WIDTH), lambda b, n: (b, n, 0)),
        out_shape=jax.ShapeDtypeStruct((bn, ln, SWA_WIDTH), BF16),
        scratch_shapes=[
            pltpu.VMEM((4 * SWA_KV_HEADS, 3 * BLOCK, 2 * BLOCK), F32),
            pltpu.VMEM((4 * SWA_KV_HEADS, 3 * BLOCK, 2 * BLOCK), BF16),
        ],
        compiler_params=_params("parallel", "arbitrary"),
        name="swa_attention",
    )(sink.reshape(1, SWA_HEADS).astype(F32), q, z, k, k, k, v, v, v, bias)


ODD_CHUNK = 256
HALO = 8
LN_ROWS = 256


def _tail_kernel(of_ref, ob_ref, z_ref, yb_ref, x_ref, wmix_ref, ng_ref, lg0_ref, lb0_ref,
                 win_ref, cw_ref, wout_ref, lg1_ref, lb1_ref, out_ref,
                 x1_ref, halo_ref, xcat_ref, u_ref, th_ref, mixed_ref, *, tm, nt):
    s = pl.program_id(0)
    last_step = pl.num_programs(0) - 1

    @pl.when(s == 0)
    def _():
        halo_ref[...] = jnp.zeros_like(halo_ref)

    @pl.when(s < last_step)
    def _even_tail():
        slot = lax.rem(s, 2)
        for r0 in range(0, tm, LN_ROWS):
            rs = slice(r0, r0 + LN_ROWS)
            o = of_ref[0, 0, rs, :].astype(F32) + ob_ref[0, 0, rs, :].astype(F32)
            parts = []
            for h in range(GLA_HEADS):
                oh = o[:, h * GLA_DK:(h + 1) * GLA_DK]
                parts.append(oh * lax.rsqrt(jnp.mean(oh * oh, axis=-1, keepdims=True) + NORM_EPS))
            on = jnp.concatenate(parts, axis=1) * ng_ref[...]
            ya = (on * _silu(z_ref[0, rs, :].astype(F32))).astype(BF16)
            sub = _dot(ya, wmix_ref[:GLA_WIDTH, :]) + _dot(yb_ref[0, rs, :], wmix_ref[GLA_WIDTH:, :])
            x1_ref[slot, rs, :] = _layer_norm(DN_ALPHA * x_ref[0, rs, :] + sub, lg0_ref[...], lb0_ref[...])

    @pl.when(s > 0)
    def _odd_layer():
        t = lax.rem(s - 1, nt)
        cur = x1_ref.at[lax.rem(s - 1, 2)]
        nxt = x1_ref.at[lax.rem(s, 2)]
        _odd_body(t, nt, halo_ref, cur, nxt, win_ref, cw_ref, wout_ref, lg1_ref, lb1_ref, out_ref,
                  xcat_ref, u_ref, th_ref, mixed_ref, tm)
        halo_ref[...] = cur[tm - HALO:tm, :]


def _odd_body(t, nt, prev_ref, x_ref, next_ref, win_ref, cw_ref, wout_ref, lg_ref, lb_ref, out_ref,
              xcat_ref, u_ref, th_ref, mixed_ref, tm):
    main = slice(HALO, HALO + tm)
    xcat_ref[...] = jnp.concatenate([prev_ref[...], x_ref[...], next_ref[0:HALO, :]], axis=0).astype(BF16)
    nj = CONV_WIDTH // ODD_CHUNK

    def in_proj(j):
        for i in range(4):
            u_ref[j % 2, :, i * ODD_CHUNK:(i + 1) * ODD_CHUNK] = _dot(
                xcat_ref[...], win_ref[:, i * CONV_WIDTH + j * ODD_CHUNK:i * CONV_WIDTH + (j + 1) * ODD_CHUNK])

    in_proj(0)
    for j in range(nj):
        cols = slice(j * ODD_CHUNK, (j + 1) * ODD_CHUNK)
        ub = u_ref.at[j % 2]
        if j + 1 < nj:
            in_proj(j + 1)
        th_ref[...] = ub[:, ODD_CHUNK:2 * ODD_CHUNK] * ub[:, 2 * ODD_CHUNK:3 * ODD_CHUNK]
        first = pl.ds(HALO - 1, 1)
        last = pl.ds(HALO + tm, 1)
        th_ref[first, :] = jnp.where(t == 0, 0.0, th_ref[first, :])
        th_ref[last, :] = jnp.where(t == nt - 1, 0.0, th_ref[last, :])
        conv = (cw_ref[0:1, cols] * th_ref[HALO - 1:HALO - 1 + tm, :]
                + cw_ref[1:2, cols] * th_ref[main, :]
                + cw_ref[2:3, cols] * th_ref[HALO + 1:HALO + 1 + tm, :])
        mixed_ref[:, cols] = (_silu(ub[main, 3 * ODD_CHUNK:]) * ub[main, :ODD_CHUNK] * conv).astype(BF16)
    for r0 in range(0, tm, LN_ROWS):
        rs = slice(r0, r0 + LN_ROWS)
        acc = _dot(mixed_ref[rs, :], wout_ref[...])
        out_ref[0, rs, :] = _layer_norm(DN_ALPHA * x_ref[rs, :] + acc, lg_ref[...], lb_ref[...])


def _tail(o, za, yb, x, wmix, ng, lg0, lb0, win, cw, wout, lg1, lb1, tm):
    bn, ln, _ = x.shape
    nt = ln // tm
    ntiles = bn * nt
    rows = tm + 2 * HALO

    def tile_in(s):
        g = jnp.minimum(s, ntiles - 1)
        return g // nt, g % nt

    def tile_out(s):
        g = jnp.maximum(s - 1, 0)
        return g // nt, g % nt

    whole = lambda *shape: pl.BlockSpec(shape, lambda s: (0,) * len(shape), pipeline_mode=pl.Buffered(1))
    return pl.pallas_call(
        functools.partial(_tail_kernel, tm=tm, nt=nt),
        grid=(ntiles + 1,),
        in_specs=[
            pl.BlockSpec((1, 1, tm, GLA_WIDTH), lambda s: (0, *tile_in(s), 0)),
            pl.BlockSpec((1, 1, tm, GLA_WIDTH), lambda s: (1, *tile_in(s), 0)),
            pl.BlockSpec((1, tm, GLA_WIDTH), lambda s: (*tile_in(s), 0)),
            pl.BlockSpec((1, tm, SWA_WIDTH), lambda s: (*tile_in(s), 0)),
            pl.BlockSpec((1, tm, D_MODEL), lambda s: (*tile_in(s), 0)),
            whole(GLA_WIDTH + SWA_WIDTH, D_MODEL),
            whole(1, GLA_WIDTH), whole(1, D_MODEL), whole(1, D_MODEL),
            whole(D_MODEL, 4 * CONV_WIDTH), whole(3, CONV_WIDTH), whole(CONV_WIDTH, D_MODEL),
            whole(1, D_MODEL), whole(1, D_MODEL),
        ],
        out_specs=pl.BlockSpec((1, tm, D_MODEL), lambda s: (*tile_out(s), 0)),
        out_shape=jax.ShapeDtypeStruct((bn, ln, D_MODEL), F32),
        scratch_shapes=[
            pltpu.VMEM((2, tm, D_MODEL), F32),
            pltpu.VMEM((HALO, D_MODEL), F32),
            pltpu.VMEM((rows, D_MODEL), BF16),
            pltpu.VMEM((2, rows, 4 * ODD_CHUNK), F32),
            pltpu.VMEM((rows, ODD_CHUNK), F32),
            pltpu.VMEM((tm, CONV_WIDTH), BF16),
        ],
        compiler_params=_params("arbitrary"),
        name="even_tail_odd_layer",
    )(o, o, za, yb, x, wmix, ng, lg0, lb0, win, cw, wout, lg1, lb1)


PREP_ROWS = 256


def _prep_even_kernel(w_ref, o_ref):
    g0 = GLA_COLS
    qb0 = g0 + GD_COLS
    kb0 = qb0 + SWA_WIDTH
    vb0 = kb0 + SWA_KVW
    zb0 = vb0 + SWA_KVW
    o_ref[:, :GLA_WIDTH] = (w_ref[:, :GLA_WIDTH] * (GLA_DK ** -0.5)).astype(BF16)
    o_ref[:, GLA_WIDTH:GLA_COLS] = w_ref[:, GLA_WIDTH:GLA_COLS].astype(BF16)
    c = GLA_COLS
    o_ref[:, c:c + SWA_WIDTH] = (w_ref[:, qb0:qb0 + SWA_WIDTH] * (SWA_HD ** -0.5 * LOG2E)).astype(BF16)
    c += SWA_WIDTH
    o_ref[:, c:c + SWA_WIDTH] = w_ref[:, zb0:zb0 + SWA_WIDTH].astype(BF16)
    c += SWA_WIDTH
    o_ref[:, c:c + SWA_KVW] = w_ref[:, kb0:kb0 + SWA_KVW].astype(BF16)
    c += SWA_KVW
    o_ref[:, c:c + SWA_KVW] = w_ref[:, vb0:vb0 + SWA_KVW].astype(BF16)
    c += SWA_KVW
    lane = lax.broadcasted_iota(jnp.int32, (PREP_ROWS, GD_PAD), 1)
    o_ref[:, c:] = jnp.where(lane < GD_COLS, w_ref[:, g0:g0 + GD_PAD], 0.0).astype(BF16)


def _prep_even(w_in, w_up_f, b_f, w_up_b, b_b, norm_g, w_out):
    even_in = w_in.shape[1]
    w = pl.pallas_call(
        _prep_even_kernel,
        grid=(D_MODEL // PREP_ROWS,),
        in_specs=[pl.BlockSpec((PREP_ROWS, even_in), lambda r: (r, 0))],
        out_specs=pl.BlockSpec((PREP_ROWS, EVEN_COLS), lambda r: (r, 0)),
        out_shape=jax.ShapeDtypeStruct((D_MODEL, EVEN_COLS), BF16),
        compiler_params=_params("parallel"),
        name="prep_even_weights",
    )(w_in)
    zr = jnp.zeros_like(w_up_f)
    zpad = jnp.zeros((GD_PAD - GD_COLS - 1, GLA_WIDTH), w_up_f.dtype)
    wup = (jnp.stack([jnp.concatenate([w_up_f, zr, b_f[None], zpad], axis=0),
                      jnp.concatenate([zr, w_up_b, b_b[None], zpad], axis=0)]) * LOG2E).astype(BF16)
    ng = jnp.tile(norm_g.astype(F32), GLA_HEADS).reshape(1, GLA_WIDTH)
    return w, wup, ng, w_out.astype(BF16)


def _prep_odd(w_in, conv_w, w_out):
    win = w_in.astype(BF16)
    cw = conv_w.astype(F32)
    wout = w_out.astype(BF16)
    return win, cw, wout


def _trunk(x, even, odd, bias_tab, sink, ln_g, ln_b, consts, tm=512, tm_in=1024, gla_tile=2048, swa_nq=16):
    w, wup, ng, wmix = even
    win, cw, wout = odd
    tri, mask = consts
    qa, ka, va, za, qb, zb, kb, vb, gd = _inproj_even(x, w, tm_in)
    o = _gla(qa, ka, va, gd, wup, tri, mask, gla_tile)
    yb = _swa(qb, zb, kb, vb, bias_tab, sink, swa_nq)
    lg = ln_g.astype(F32).reshape(DEPTH, 1, D_MODEL)
    lb = ln_b.astype(F32).reshape(DEPTH, 1, D_MODEL)
    return _tail(o, za, yb, x, wmix, ng, lg[0], lb[0], win, cw, wout, lg[1], lb[1], tm)


def kernel(x_prompt, x_sample, w_in_even, gla_w_up_fwd, gla_b_fwd, gla_w_up_bwd, gla_b_bwd, gla_norm_g, swa_sink,
           rel_bias, w_out_even, w_in_odd, conv_w, w_out_odd, ln_g, ln_b):
    even = _prep_even(w_in_even[0], gla_w_up_fwd[0], gla_b_fwd[0], gla_w_up_bwd[0], gla_b_bwd[0], gla_norm_g[0],
                      w_out_even[0])
    odd = _prep_odd(w_in_odd[0], conv_w[0], w_out_odd[0])
    bias_tab = _bias_table(rel_bias)
    consts = _gla_constants()
    run = lambda x: _trunk(x, even, odd, bias_tab, swa_sink[0], ln_g, ln_b, consts)
    return (run(x_prompt), run(x_sample))
```

```python
import functools
import math

import numpy as np
import jax
import jax.numpy as jnp
from jax import lax
from jax.experimental import pallas as pl
from jax.experimental.pallas import tpu as pltpu

F32 = jnp.float32
BF16 = jnp.bfloat16

D_MODEL = 1024
DEPTH = 2
GLA_HEADS = 4
GLA_DK = 128
GLA_WIDTH = 512
GLA_RANK = 16
GLA_TAU = 16.0
SWA_HEADS = 8
SWA_KV_HEADS = 2
SWA_HD = 64
SWA_WIDTH = 512
SWA_KVW = 128
WINDOW = 128
BLOCK = 128
REL_BUCKETS = 32
REL_MAX_DIST = 128
CONV_WIDTH = 1024
DN_ALPHA = (2 * DEPTH) ** 0.25
LN_EPS = 1e-5
NORM_EPS = 1e-6
NEG_BIG = -1e30
LOG2E = math.log2(math.e)
LN2 = math.log(2.0)
SWA_ONES_ROWS = 16

GLA_COLS = 4 * GLA_WIDTH
SWA_COLS = 2 * SWA_WIDTH + 2 * SWA_KVW
GD_COLS = 2 * GLA_RANK
GD_PAD = 128
EVEN_COLS = GLA_COLS + SWA_COLS + GD_PAD

GLA_CHUNK = 128
TOT_ROWS = 16
GLA_SAFE_LOGIT = -8.0
VMEM_LIMIT = 56 * 1024 * 1024


def _dot(a, b):
    return jnp.dot(a, b, preferred_element_type=F32)


def _dot_nt(a, b):
    return lax.dot_general(a, b, (((1,), (1,)), ((), ())), preferred_element_type=F32)


def _dot_tn(a, b):
    return lax.dot_general(a, b, (((0,), (0,)), ((), ())), preferred_element_type=F32)


def _silu(z):
    return z / (1.0 + jnp.exp(-z))


def _layer_norm(y, g, b):
    mu = jnp.mean(y, axis=-1, keepdims=True)
    yc = y - mu
    var = jnp.mean(yc * yc, axis=-1, keepdims=True)
    return yc * lax.rsqrt(var + LN_EPS) * g + b


def _params(*sem):
    return pltpu.CompilerParams(dimension_semantics=sem, vmem_limit_bytes=VMEM_LIMIT)


EVEN_OUT_WIDTHS = (GLA_WIDTH,) * 4 + (SWA_WIDTH, SWA_WIDTH, SWA_KVW, SWA_KVW, GD_PAD)
EVEN_GATE_OUTPUTS = (3, 5)


def _inproj_even_kernel(x_ref, w_ref, *out_refs):
    xb = x_ref[0].astype(BF16)
    c0 = 0
    for i, (ref, width) in enumerate(zip(out_refs[:-3], EVEN_OUT_WIDTHS[:-3])):
        u = _dot(xb, w_ref[:, c0:c0 + width])
        ref[0] = (_silu(u) if i in EVEN_GATE_OUTPUTS else u).astype(BF16)
        c0 += width
    kvg = _dot(xb, w_ref[:, c0:])
    out_refs[-3][0] = kvg[:, :SWA_KVW].astype(BF16)
    out_refs[-2][0] = kvg[:, SWA_KVW:2 * SWA_KVW].astype(BF16)
    one_lane = (lax.broadcasted_iota(jnp.int32, (1, GD_PAD), 1) == GD_COLS).astype(F32)
    out_refs[-1][0] = (kvg[:, 2 * SWA_KVW:] + one_lane).astype(BF16)


def _inproj_even(x, w, tm):
    bn, ln, _ = x.shape
    return pl.pallas_call(
        _inproj_even_kernel,
        grid=(bn, ln // tm),
        in_specs=[
            pl.BlockSpec((1, tm, D_MODEL), lambda b, t: (b, t, 0)),
            pl.BlockSpec((D_MODEL, EVEN_COLS), lambda b, t: (0, 0), pipeline_mode=pl.Buffered(1)),
        ],
        out_specs=[pl.BlockSpec((1, tm, width), lambda b, t: (b, t, 0)) for width in EVEN_OUT_WIDTHS],
        out_shape=[jax.ShapeDtypeStruct((bn, ln, width), BF16) for width in EVEN_OUT_WIDTHS],
        compiler_params=_params("parallel", "parallel"),
        name="inproj_even",
    )(x, w)


def _gla_kernel(q_ref, k_ref, v_ref, gd_ref, gdn_ref, wup_ref, tri_ref, mask_ref, o_ref,
                st_ref, flag_ref, logd_ref, sq_ref, kt_ref, u_ref, et_ref, cum_ref, qf_ref, kf_ref,
                *, tile):
    d = pl.program_id(0)
    t = pl.program_id(2)
    slot = lax.rem(t, 2)

    def gate(gd_blk, sl):
        a2 = _dot(gd_blk, wup_ref[0])
        logd_ref[sl] = ((jnp.minimum(a2, 0.0) - jnp.log2(1.0 + jnp.exp2(-jnp.abs(a2)))) * (LN2 / GLA_TAU)).astype(BF16)
        flag_ref[sl] = (jnp.min(a2) < GLA_SAFE_LOGIT * LOG2E).astype(jnp.int32)

    @pl.when(t == 0)
    def _():
        st_ref[...] = jnp.zeros_like(st_ref)
        gate(gd_ref[0], slot)

    unsafe = flag_ref[slot] != 0

    @pl.when(jnp.logical_not(unsafe))
    def _():
        _gla_fast_tile(d, q_ref, k_ref, v_ref, tri_ref, mask_ref, o_ref, st_ref, logd_ref.at[slot], sq_ref,
                       kt_ref, u_ref, et_ref, cum_ref, tile)
        gate(gdn_ref[0], 1 - slot)

    @pl.when(unsafe)
    def _():
        _gla_pairwise_tile(d, q_ref, k_ref, v_ref, tri_ref, o_ref, st_ref, logd_ref.at[slot], cum_ref,
                           qf_ref, kf_ref, tile)
        gate(gdn_ref[0], 1 - slot)


def _gla_pairwise_tile(d, q_ref, k_ref, v_ref, tri_ref, o_ref, st_ref, logd_ref, cum_ref, qf_ref, kf_ref, tile):
    nchunk = tile // GLA_CHUNK
    tri = tri_ref[0]
    jrow = lax.broadcasted_iota(jnp.int32, (GLA_CHUNK, GLA_CHUNK), 0)
    icol = lax.broadcasted_iota(jnp.int32, (GLA_CHUNK, GLA_CHUNK), 1)

    def chunk(c, carry):
        cc = c + d * (nchunk - 1 - 2 * c)
        rows = pl.ds(pl.multiple_of(cc * GLA_CHUNK, GLA_CHUNK), GLA_CHUNK)
        cum_ref[0] = _dot(tri, logd_ref[rows, :])
        for h in range(GLA_HEADS):
            hs = slice(h * GLA_DK, (h + 1) * GLA_DK)
            b = cum_ref[0, :GLA_CHUNK, hs]
            tot = cum_ref[0, GLA_CHUNK:GLA_CHUNK + 1, hs]
            qf_ref[...] = q_ref[0, rows, hs].astype(F32)
            kf_ref[...] = k_ref[0, rows, hs].astype(F32)
            v = v_ref[0, rows, hs]

            def pair_rows(g, st_t):
                base = pl.multiple_of(g * 8, 8)
                b8 = cum_ref[0, pl.ds(base, 8), hs]
                q8 = qf_ref[pl.ds(base, 8), :]
                for r in range(8):
                    i = base + r
                    w = jnp.exp(jnp.minimum(b8[r:r + 1] - cum_ref[0, :GLA_CHUNK, hs], 0.0))
                    col = jnp.sum(q8[r:r + 1] * kf_ref[...] * w, axis=1, keepdims=True)
                    valid = (1 - 2 * d) * (jrow - i) <= -d
                    st_t = jnp.where((icol == i) & valid, col, st_t)
                return st_t

            s = lax.fori_loop(0, GLA_CHUNK // 8, pair_rows, jnp.zeros((GLA_CHUNK, GLA_CHUNK), F32)).T.astype(BF16)
            qt = (qf_ref[...] * jnp.exp(b)).astype(BF16)
            kd = (kf_ref[...] * jnp.exp(tot - b)).astype(BF16)
            st = st_ref[h]
            vs = jnp.concatenate([v, st.T.astype(BF16)], axis=0)
            o_ref[0, 0, rows, hs] = _dot(jnp.concatenate([s, qt], axis=1), vs).astype(BF16)
            st_ref[h] = st * jnp.exp(tot) + _dot_tn(v, kd)
        return carry

    lax.fori_loop(0, nchunk, chunk, 0)


def _gla_fast_tile(d, q_ref, k_ref, v_ref, tri_ref, mask_ref, o_ref, st_ref, logd_ref, sq_ref,
                   kt_ref, u_ref, et_ref, cum_ref, tile):
    nchunk = tile // GLA_CHUNK
    tri = tri_ref[0]
    keep = mask_ref[0] > 0.0
    heads = [slice(h * GLA_DK, (h + 1) * GLA_DK) for h in range(GLA_HEADS)]
    s_cols = [slice(2 * h * GLA_DK, (2 * h + 1) * GLA_DK) for h in range(GLA_HEADS)]
    q_cols = [slice((2 * h + 1) * GLA_DK, (2 * h + 2) * GLA_DK) for h in range(GLA_HEADS)]
    sq_cols = [slice(2 * h * GLA_DK, (2 * h + 2) * GLA_DK) for h in range(GLA_HEADS)]

    for c in range(nchunk):
        rows = slice(c * GLA_CHUNK, (c + 1) * GLA_CHUNK)
        cum_ref[c % 2] = _dot(tri, logd_ref[rows, :])
        for h, hs in enumerate(heads):
            b = cum_ref[c % 2, :GLA_CHUNK, hs]
            etot = jnp.exp(cum_ref[c % 2, GLA_CHUNK:GLA_CHUNK + 1, hs])
            sq_ref[rows, q_cols[h]] = (q_ref[0, rows, hs].astype(F32) * jnp.exp(b)).astype(BF16)
            kt_ref[rows, hs] = (k_ref[0, rows, hs].astype(F32) * jnp.exp(-b)).astype(BF16)
            et_ref[c, :, hs] = etot

    for c in range(nchunk):
        rows = slice(c * GLA_CHUNK, (c + 1) * GLA_CHUNK)
        for h, hs in enumerate(heads):
            sq_ref[rows, s_cols[h]] = jnp.where(
                keep, _dot_nt(sq_ref[rows, q_cols[h]], kt_ref[rows, hs]), 0.0).astype(BF16)
            u_ref[c, h] = _dot_tn(v_ref[0, rows, hs], kt_ref[rows, hs])

    for c in range(nchunk):
        cc = c + d * (nchunk - 1 - 2 * c)
        rows = pl.ds(pl.multiple_of(cc * GLA_CHUNK, GLA_CHUNK), GLA_CHUNK)
        for h, hs in enumerate(heads):
            st = st_ref[h]
            vs = jnp.concatenate([v_ref[0, rows, hs], st.T.astype(BF16)], axis=0)
            o_ref[0, 0, rows, hs] = _dot(sq_ref[rows, sq_cols[h]], vs).astype(BF16)
            st_ref[h] = (st + u_ref[cc, h]) * et_ref[cc, :, hs]


def _gla(q, k, v, gd, wup, tri, mask, tile):
    bn, ln, _ = q.shape
    nt = ln // tile

    def tok(d, b, t):
        return t + d * (nt - 1 - 2 * t)

    return pl.pallas_call(
        functools.partial(_gla_kernel, tile=tile),
        grid=(2, bn, nt),
        in_specs=[
            pl.BlockSpec((1, tile, GLA_WIDTH), lambda d, b, t: (b, tok(d, b, t), 0)),
            pl.BlockSpec((1, tile, GLA_WIDTH), lambda d, b, t: (b, tok(d, b, t), 0)),
            pl.BlockSpec((1, tile, GLA_WIDTH), lambda d, b, t: (b, tok(d, b, t), 0)),
            pl.BlockSpec((1, tile, GD_PAD), lambda d, b, t: (b, tok(d, b, t), 0)),
            pl.BlockSpec((1, tile, GD_PAD), lambda d, b, t: (b, tok(d, b, jnp.minimum(t + 1, nt - 1)), 0)),
            pl.BlockSpec((1, GD_PAD, GLA_WIDTH), lambda d, b, t: (d, 0, 0)),
            pl.BlockSpec((1, GLA_CHUNK + TOT_ROWS, GLA_CHUNK), lambda d, b, t: (d, 0, 0)),
            pl.BlockSpec((1, GLA_CHUNK, GLA_CHUNK), lambda d, b, t: (d, 0, 0)),
        ],
        out_specs=pl.BlockSpec((1, 1, tile, GLA_WIDTH), lambda d, b, t: (d, b, tok(d, b, t), 0)),
        out_shape=jax.ShapeDtypeStruct((2, bn, ln, GLA_WIDTH), BF16),
        scratch_shapes=[
            pltpu.VMEM((GLA_HEADS, GLA_DK, GLA_DK), F32),
            pltpu.SMEM((2,), jnp.int32),
            pltpu.VMEM((2, tile, GLA_WIDTH), BF16),
            pltpu.VMEM((tile, 2 * GLA_WIDTH), BF16),
            pltpu.VMEM((tile, GLA_WIDTH), BF16),
            pltpu.VMEM((tile // GLA_CHUNK, GLA_HEADS, GLA_DK, GLA_DK), F32),
            pltpu.VMEM((tile // GLA_CHUNK, 1, GLA_WIDTH), F32),
            pltpu.VMEM((2, GLA_CHUNK + TOT_ROWS, GLA_WIDTH), F32),
            pltpu.VMEM((GLA_CHUNK, GLA_DK), F32),
            pltpu.VMEM((GLA_CHUNK, GLA_DK), F32),
        ],
        compiler_params=_params("arbitrary", "arbitrary", "arbitrary"),
        name="gla_scan",
    )(q, k, v, gd, gd, wup, tri, mask)


def _gla_constants():
    i = np.arange(GLA_CHUNK)[:, None]
    j = np.arange(GLA_CHUNK)[None, :]
    lower = (j <= i).astype(np.float32)
    upper = (j >= i).astype(np.float32)
    tri = np.zeros((2, GLA_CHUNK + TOT_ROWS, GLA_CHUNK), np.float32)
    tri[0, :GLA_CHUNK] = lower
    tri[1, :GLA_CHUNK] = upper
    tri[:, GLA_CHUNK:] = 1.0
    mask = np.stack([(j <= i), (j > i)]).astype(np.float32)
    return jnp.asarray(tri, BF16), jnp.asarray(mask, F32)


def _rel_tables():
    i = np.arange(BLOCK)[:, None]
    j = np.arange(3 * BLOCK)[None, :]
    rel = j - BLOCK - i
    half = REL_BUCKETS // 2
    max_exact = half // 2
    n = np.abs(rel)
    large = max_exact + (np.log(np.maximum(n, 1) / max_exact) / np.log(REL_MAX_DIST / max_exact)
                         * (half - max_exact)).astype(np.int32)
    large = np.minimum(large, half - 1)
    bucket = (rel > 0).astype(np.int32) * half + np.where(n < max_exact, n, large)
    band = np.abs(rel) <= WINDOW
    col = np.broadcast_to(j, rel.shape)
    valid = np.stack([band & (col >= BLOCK), band, band & (col < 2 * BLOCK)])
    return np.ascontiguousarray(bucket.T).astype(np.int32), np.ascontiguousarray(valid.transpose(0, 2, 1)).astype(np.int32)


def _bias_kernel(rb_ref, bucket_ref, valid_ref, out_ref):
    h = pl.program_id(0)
    bucket = bucket_ref[...]
    acc = jnp.zeros(bucket.shape, F32)
    for kk in range(REL_BUCKETS):
        acc = jnp.where(bucket == kk, rb_ref[kk, h] * LOG2E, acc)
    for kind in range(3):
        out_ref[kind, 0] = jnp.where(valid_ref[kind] > 0, acc, NEG_BIG)


def _bias_table(rel_bias):
    bucket, valid = _rel_tables()
    return pl.pallas_call(
        _bias_kernel,
        grid=(SWA_HEADS,),
        in_specs=[
            pl.BlockSpec(memory_space=pltpu.SMEM),
            pl.BlockSpec((3 * BLOCK, BLOCK), lambda h: (0, 0)),
            pl.BlockSpec((3, 3 * BLOCK, BLOCK), lambda h: (0, 0, 0)),
        ],
        out_specs=pl.BlockSpec((3, 1, 3 * BLOCK, BLOCK), lambda h: (0, h, 0, 0)),
        out_shape=jax.ShapeDtypeStruct((3, SWA_HEADS, 3 * BLOCK, BLOCK), F32),
        compiler_params=_params("arbitrary"),
        name="swa_bias_table",
    )(rel_bias.astype(F32), jnp.asarray(bucket), jnp.asarray(valid))


def _swa_kernel(sink_ref, q_ref, z_ref, kp_ref, kc_ref, kn_ref, vp_ref, vc_ref, vn_ref, bias_ref, o_ref,
                st_ref, pt_ref, *, nq):
    n = pl.program_id(1)
    nsteps = pl.num_programs(1)
    kcat = jnp.concatenate([kp_ref[0], kc_ref[0], kn_ref[0]], axis=0)
    vcat = jnp.concatenate([vp_ref[0], vc_ref[0], vn_ref[0]], axis=0)
    kswap = jnp.concatenate([kcat[:, SWA_HD:], kcat[:, :SWA_HD]], axis=1)
    lane = lax.broadcasted_iota(jnp.int32, kcat.shape, 1)
    low = lane < SWA_HD
    zero = jnp.zeros_like(kcat)
    kmat = {(0, 0): jnp.where(low, kcat, zero), (0, 1): jnp.where(low, zero, kswap),
            (1, 0): jnp.where(low, kswap, zero), (1, 1): jnp.where(low, zero, kcat)}
    vt = vcat.astype(F32).T.astype(BF16)
    ones = jnp.ones((SWA_ONES_ROWS, 3 * BLOCK), BF16)
    half = lax.broadcasted_iota(jnp.int32, (1, 2 * BLOCK), 1) < BLOCK
    pairs_per_kv = SWA_HEADS // SWA_KV_HEADS // 2
    pairs = [[slice((pairs_per_kv * g + i) * 128, (pairs_per_kv * g + i + 1) * 128) for i in range(pairs_per_kv)]
             for g in range(SWA_KV_HEADS)]
    combos = [(g, e) for g in range(SWA_KV_HEADS) for e in range(2)]
    ncomb = len(combos)
    for qb in range(nq):
        qrows = slice(qb * BLOCK, (qb + 1) * BLOCK)
        keys = slice(qb * BLOCK, (qb + 3) * BLOCK)
        kind = 1
        if qb == 0:
            kind = jnp.where(n == 0, 0, kind)
        if qb == nq - 1:
            kind = jnp.where(n == nsteps - 1, 2, kind)
        slot = (qb % 2) * ncomb
        for c, (g, e) in enumerate(combos):
            h0 = 2 * pairs_per_kv * g + e
            qg = jnp.concatenate([q_ref[0, qrows, ps] for ps in pairs[g]], axis=0)
            st_ref[slot + c] = (_dot_nt(kmat[(g, e)][keys], qg)
                                + jnp.concatenate([bias_ref[kind, h0], bias_ref[kind, h0 + 2]], axis=1))
        stats = []
        for c, (g, e) in enumerate(combos):
            h0 = 2 * pairs_per_kv * g + e
            sink = jnp.where(half, sink_ref[0, h0], sink_ref[0, h0 + 2]) * LOG2E
            m = jnp.maximum(jnp.max(st_ref[slot + c], axis=0, keepdims=True), sink)
            pt_ref[slot + c] = jnp.exp2(st_ref[slot + c] - m).astype(BF16)
            stats.append(jnp.exp2(sink - m))
        outs = {}
        for c, (g, e) in enumerate(combos):
            vaug = jnp.concatenate([vt[g * SWA_HD:(g + 1) * SWA_HD, keys], ones], axis=0)
            ot = _dot(vaug, pt_ref[slot + c])
            outs[(g, e)] = ot[:SWA_HD] * (1.0 / (ot[SWA_HD:SWA_HD + 1] + stats[c]))
        for g in range(SWA_KV_HEADS):
            for i, ps in enumerate(pairs[g]):
                cs = slice(i * BLOCK, (i + 1) * BLOCK)
                o = jnp.concatenate([outs[(g, 0)][:, cs], outs[(g, 1)][:, cs]], axis=0).T
                o_ref[0, qrows, ps] = (o * z_ref[0, qrows, ps].astype(F32)).astype(BF16)


def _swa(q, z, k, v, bias, sink, nq):
    bn, ln, _ = q.shape
    nb = ln // BLOCK
    assert nb % nq == 0 and nb >= 2
    prev = pl.BlockSpec((1, BLOCK, SWA_KVW), lambda b, n: (b, jnp.maximum(n * nq - 1, 0), 0))
    own = pl.BlockSpec((1, nq * BLOCK, SWA_KVW), lambda b, n: (b, n, 0))
    nxt = pl.BlockSpec((1, BLOCK, SWA_KVW), lambda b, n: (b, jnp.minimum((n + 1) * nq, nb - 1), 0))
    return pl.pallas_call(
        functools.partial(_swa_kernel, nq=nq),
        grid=(bn, nb // nq),
        in_specs=[
            pl.BlockSpec(memory_space=pltpu.SMEM),
            pl.BlockSpec((1, nq * BLOCK, SWA_WIDTH), lambda b, n: (b, n, 0)),
            pl.BlockSpec((1, nq * BLOCK, SWA_WIDTH), lambda b, n: (b, n, 0)),
            prev, own, nxt,
            prev, own, nxt,
            pl.BlockSpec((3, SWA_HEADS, 3 * BLOCK, BLOCK), lambda b, n: (0, 0, 0, 0)),
        ],
        out_specs=pl.BlockSpec((1, nq * BLOCK, SWA_WIDTH), lambda b, n: (b, n, 0)),
        out_shape=jax.ShapeDtypeStruct((bn, ln, SWA_WIDTH), BF16),
        scratch_shapes=[
            pltpu.VMEM((4 * SWA_KV_HEADS, 3 * BLOCK, 2 * BLOCK), F32),
            pltpu.VMEM((4 * SWA_KV_HEADS, 3 * BLOCK, 2 * BLOCK), BF16),
        ],
        compiler_params=_params("parallel", "arbitrary"),
        name="swa_attention",
    )(sink.reshape(1, SWA_HEADS).astype(F32), q, z, k, k, k, v, v, v, bias)


ODD_CHUNK = 256
HALO = 8
LN_ROWS = 256


def _tail_kernel(of_ref, ob_ref, z_ref, yb_ref, x_ref, wmix_ref, ng_ref, lg0_ref, lb0_ref,
                 win_ref, cw_ref, wout_ref, lg1_ref, lb1_ref, out_ref,
                 x1_ref, halo_ref, xcat_ref, u_ref, th_ref, mixed_ref, *, tm, nt):
    s = pl.program_id(0)
    last_step = pl.num_programs(0) - 1

    @pl.when(s == 0)
    def _():
        halo_ref[...] = jnp.zeros_like(halo_ref)

    @pl.when(s < last_step)
    def _even_tail():
        slot = lax.rem(s, 2)
        for r0 in range(0, tm, LN_ROWS):
            rs = slice(r0, r0 + LN_ROWS)
            o = of_ref[0, 0, rs, :].astype(F32) + ob_ref[0, 0, rs, :].astype(F32)
            parts = []
            for h in range(GLA_HEADS):
                oh = o[:, h * GLA_DK:(h + 1) * GLA_DK]
                parts.append(oh * lax.rsqrt(jnp.mean(oh * oh, axis=-1, keepdims=True) + NORM_EPS))
            on = jnp.concatenate(parts, axis=1) * ng_ref[...]
            ya = (on * z_ref[0, rs, :].astype(F32)).astype(BF16)
            sub = _dot(ya, wmix_ref[:GLA_WIDTH, :]) + _dot(yb_ref[0, rs, :], wmix_ref[GLA_WIDTH:, :])
            x1_ref[slot, rs, :] = _layer_norm(DN_ALPHA * x_ref[0, rs, :] + sub, lg0_ref[...], lb0_ref[...])

    @pl.when(s > 0)
    def _odd_layer():
        t = lax.rem(s - 1, nt)
        cur = x1_ref.at[lax.rem(s - 1, 2)]
        nxt = x1_ref.at[lax.rem(s, 2)]
        _odd_body(t, nt, halo_ref, cur, nxt, win_ref, cw_ref, wout_ref, lg1_ref, lb1_ref, out_ref,
                  xcat_ref, u_ref, th_ref, mixed_ref, tm)
        halo_ref[...] = cur[tm - HALO:tm, :]


def _odd_body(t, nt, prev_ref, x_ref, next_ref, win_ref, cw_ref, wout_ref, lg_ref, lb_ref, out_ref,
              xcat_ref, u_ref, th_ref, mixed_ref, tm):
    main = slice(HALO, HALO + tm)
    xcat_ref[...] = jnp.concatenate([prev_ref[...], x_ref[...], next_ref[0:HALO, :]], axis=0).astype(BF16)
    nj = CONV_WIDTH // ODD_CHUNK

    def in_proj(j):
        for i in range(4):
            u_ref[j % 2, :, i * ODD_CHUNK:(i + 1) * ODD_CHUNK] = _dot(
                xcat_ref[...], win_ref[:, i * CONV_WIDTH + j * ODD_CHUNK:i * CONV_WIDTH + (j + 1) * ODD_CHUNK])

    in_proj(0)
    for j in range(nj):
        cols = slice(j * ODD_CHUNK, (j + 1) * ODD_CHUNK)
        ub = u_ref.at[j % 2]
        if j + 1 < nj:
            in_proj(j + 1)
        th_ref[...] = ub[:, ODD_CHUNK:2 * ODD_CHUNK] * ub[:, 2 * ODD_CHUNK:3 * ODD_CHUNK]
        first = pl.ds(HALO - 1, 1)
        last = pl.ds(HALO + tm, 1)
        th_ref[first, :] = jnp.where(t == 0, 0.0, th_ref[first, :])
        th_ref[last, :] = jnp.where(t == nt - 1, 0.0, th_ref[last, :])
        conv = (cw_ref[0:1, cols] * th_ref[HALO - 1:HALO - 1 + tm, :]
                + cw_ref[1:2, cols] * th_ref[main, :]
                + cw_ref[2:3, cols] * th_ref[HALO + 1:HALO + 1 + tm, :])
        mixed_ref[:, cols] = (_silu(ub[main, 3 * ODD_CHUNK:]) * ub[main, :ODD_CHUNK] * conv).astype(BF16)
    for r0 in range(0, tm, LN_ROWS):
        rs = slice(r0, r0 + LN_ROWS)
        acc = _dot(mixed_ref[rs, :], wout_ref[...])
        out_ref[0, rs, :] = _layer_norm(DN_ALPHA * x_ref[rs, :] + acc, lg_ref[...], lb_ref[...])


def _tail(o, za, yb, x, wmix, ng, lg0, lb0, win, cw, wout, lg1, lb1, tm):
    bn, ln, _ = x.shape
    nt = ln // tm
    ntiles = bn * nt
    rows = tm + 2 * HALO

    def tile_in(s):
        g = jnp.minimum(s, ntiles - 1)
        return g // nt, g % nt

    def tile_out(s):
        g = jnp.maximum(s - 1, 0)
        return g // nt, g % nt

    whole = lambda *shape: pl.BlockSpec(shape, lambda s: (0,) * len(shape), pipeline_mode=pl.Buffered(1))
    return pl.pallas_call(
        functools.partial(_tail_kernel, tm=tm, nt=nt),
        grid=(ntiles + 1,),
        in_specs=[
            pl.BlockSpec((1, 1, tm, GLA_WIDTH), lambda s: (0, *tile_in(s), 0)),
            pl.BlockSpec((1, 1, tm, GLA_WIDTH), lambda s: (1, *tile_in(s), 0)),
            pl.BlockSpec((1, tm, GLA_WIDTH), lambda s: (*tile_in(s), 0)),
            pl.BlockSpec((1, tm, SWA_WIDTH), lambda s: (*tile_in(s), 0)),
            pl.BlockSpec((1, tm, D_MODEL), lambda s: (*tile_in(s), 0)),
            whole(GLA_WIDTH + SWA_WIDTH, D_MODEL),
            whole(1, GLA_WIDTH), whole(1, D_MODEL), whole(1, D_MODEL),
            whole(D_MODEL, 4 * CONV_WIDTH), whole(3, CONV_WIDTH), whole(CONV_WIDTH, D_MODEL),
            whole(1, D_MODEL), whole(1, D_MODEL),
        ],
        out_specs=pl.BlockSpec((1, tm, D_MODEL), lambda s: (*tile_out(s), 0)),
        out_shape=jax.ShapeDtypeStruct((bn, ln, D_MODEL), F32),
        scratch_shapes=[
            pltpu.VMEM((2, tm, D_MODEL), F32),
            pltpu.VMEM((HALO, D_MODEL), F32),
            pltpu.VMEM((rows, D_MODEL), BF16),
            pltpu.VMEM((2, rows, 4 * ODD_CHUNK), F32),
            pltpu.VMEM((rows, ODD_CHUNK), F32),
            pltpu.VMEM((tm, CONV_WIDTH), BF16),
        ],
        compiler_params=_params("arbitrary"),
        name="even_tail_odd_layer",
    )(o, o, za, yb, x, wmix, ng, lg0, lb0, win, cw, wout, lg1, lb1)


PREP_ROWS = 256


def _prep_even_kernel(w_ref, o_ref):
    g0 = GLA_COLS
    qb0 = g0 + GD_COLS
    kb0 = qb0 + SWA_WIDTH
    vb0 = kb0 + SWA_KVW
    zb0 = vb0 + SWA_KVW
    o_ref[:, :GLA_WIDTH] = (w_ref[:, :GLA_WIDTH] * (GLA_DK ** -0.5)).astype(BF16)
    o_ref[:, GLA_WIDTH:GLA_COLS] = w_ref[:, GLA_WIDTH:GLA_COLS].astype(BF16)
    c = GLA_COLS
    o_ref[:, c:c + SWA_WIDTH] = (w_ref[:, qb0:qb0 + SWA_WIDTH] * (SWA_HD ** -0.5 * LOG2E)).astype(BF16)
    c += SWA_WIDTH
    o_ref[:, c:c + SWA_WIDTH] = w_ref[:, zb0:zb0 + SWA_WIDTH].astype(BF16)
    c += SWA_WIDTH
    o_ref[:, c:c + SWA_KVW] = w_ref[:, kb0:kb0 + SWA_KVW].astype(BF16)
    c += SWA_KVW
    o_ref[:, c:c + SWA_KVW] = w_ref[:, vb0:vb0 + SWA_KVW].astype(BF16)
    c += SWA_KVW
    lane = lax.broadcasted_iota(jnp.int32, (PREP_ROWS, GD_PAD), 1)
    o_ref[:, c:] = jnp.where(lane < GD_COLS, w_ref[:, g0:g0 + GD_PAD], 0.0).astype(BF16)


def _prep_even(w_in_layers, w_up_f, b_f, w_up_b, b_b, norm_g, w_out):
    even_in = w_in_layers.shape[2]
    w = pl.pallas_call(
        _prep_even_kernel,
        grid=(D_MODEL // PREP_ROWS,),
        in_specs=[pl.BlockSpec((None, PREP_ROWS, even_in), lambda r: (0, r, 0))],
        out_specs=pl.BlockSpec((PREP_ROWS, EVEN_COLS), lambda r: (r, 0)),
        out_shape=jax.ShapeDtypeStruct((D_MODEL, EVEN_COLS), BF16),
        compiler_params=_params("parallel"),
        name="prep_even_weights",
    )(w_in_layers)
    zr = jnp.zeros_like(w_up_f)
    zpad = jnp.zeros((GD_PAD - GD_COLS - 1, GLA_WIDTH), w_up_f.dtype)
    wup = (jnp.stack([jnp.concatenate([w_up_f, zr, b_f[None], zpad], axis=0),
                      jnp.concatenate([zr, w_up_b, b_b[None], zpad], axis=0)]) * LOG2E).astype(BF16)
    ng = jnp.tile(norm_g.astype(F32), GLA_HEADS).reshape(1, GLA_WIDTH)
    return w, wup, ng, w_out.astype(BF16)


def _prep_odd(w_in, conv_w, w_out):
    win = w_in.astype(BF16)
    cw = conv_w.astype(F32)
    wout = w_out.astype(BF16)
    return win, cw, wout


def _trunk(x, even, odd, bias_tab, sink, ln_g, ln_b, consts, tm=512, tm_in=1024, gla_tile=2048, swa_nq=16):
    w, wup, ng, wmix = even
    win, cw, wout = odd
    tri, mask = consts
    qa, ka, va, za, qb, zb, kb, vb, gd = _inproj_even(x, w, tm_in)
    o = _gla(qa, ka, va, gd, wup, tri, mask, gla_tile)
    yb = _swa(qb, zb, kb, vb, bias_tab, sink, swa_nq)
    lg = ln_g.astype(F32).reshape(DEPTH, 1, D_MODEL)
    lb = ln_b.astype(F32).reshape(DEPTH, 1, D_MODEL)
    return _tail(o, za, yb, x, wmix, ng, lg[0], lb[0], win, cw, wout, lg[1], lb[1], tm)


def kernel(x_prompt, x_sample, w_in_even, gla_w_up_fwd, gla_b_fwd, gla_w_up_bwd, gla_b_bwd, gla_norm_g, swa_sink,
           rel_bias, w_out_even, w_in_odd, conv_w, w_out_odd, ln_g, ln_b):
    even = _prep_even(w_in_even, gla_w_up_fwd[0], gla_b_fwd[0], gla_w_up_bwd[0], gla_b_bwd[0], gla_norm_g[0],
                      w_out_even[0])
    odd = _prep_odd(w_in_odd[0], conv_w[0], w_out_odd[0])
    bias_tab = _bias_table(rel_bias)
    consts = _gla_constants()
    run = lambda x: _trunk(x, even, odd, bias_tab, swa_sink[0], ln_g, ln_b, consts)
    return (run(x_prompt), run(x_sample))
```

```python
import functools
import math

import numpy as np
import jax
import jax.numpy as jnp
from jax import lax
from jax.experimental import pallas as pl
from jax.experimental.pallas import tpu as pltpu

F32 = jnp.float32
BF16 = jnp.bfloat16

D_MODEL = 1024
DEPTH = 2
GLA_HEADS = 4
GLA_DK = 128
GLA_WIDTH = 512
GLA_RANK = 16
GLA_TAU = 16.0
SWA_HEADS = 8
SWA_KV_HEADS = 2
SWA_HD = 64
SWA_WIDTH = 512
SWA_KVW = 128
WINDOW = 128
BLOCK = 128
REL_BUCKETS = 32
REL_MAX_DIST = 128
CONV_WIDTH = 1024
DN_ALPHA = (2 * DEPTH) ** 0.25
LN_EPS = 1e-5
NORM_EPS = 1e-6
NEG_BIG = -1e30
LOG2E = math.log2(math.e)
LN2 = math.log(2.0)
SWA_ONES_ROWS = 16

GLA_COLS = 4 * GLA_WIDTH
SWA_COLS = 2 * SWA_WIDTH + 2 * SWA_KVW
GD_COLS = 2 * GLA_RANK
GD_PAD = 128
EVEN_COLS = GLA_COLS + SWA_COLS + GD_PAD

GLA_CHUNK = 128
TOT_ROWS = 16
GLA_SAFE_LOGIT = -8.0
VMEM_LIMIT = 56 * 1024 * 1024


def _dot(a, b):
    return jnp.dot(a, b, preferred_element_type=F32)


def _dot_nt(a, b):
    return lax.dot_general(a, b, (((1,), (1,)), ((), ())), preferred_element_type=F32)


def _dot_tn(a, b):
    return lax.dot_general(a, b, (((0,), (0,)), ((), ())), preferred_element_type=F32)


def _silu(z):
    return z / (1.0 + jnp.exp(-z))


def _layer_norm(y, g, b):
    mu = jnp.mean(y, axis=-1, keepdims=True)
    yc = y - mu
    var = jnp.mean(yc * yc, axis=-1, keepdims=True)
    return yc * lax.rsqrt(var + LN_EPS) * g + b


def _params(*sem):
    return pltpu.CompilerParams(dimension_semantics=sem, vmem_limit_bytes=VMEM_LIMIT)


EVEN_OUT_WIDTHS = (GLA_WIDTH,) * 4 + (SWA_WIDTH, SWA_WIDTH, SWA_KVW, SWA_KVW, GD_PAD)
EVEN_GATE_OUTPUTS = (3,)


def _inproj_even_kernel(x_ref, w_ref, *out_refs):
    xb = x_ref[0].astype(BF16)
    c0 = 0
    for i, (ref, width) in enumerate(zip(out_refs[:-3], EVEN_OUT_WIDTHS[:-3])):
        u = _dot(xb, w_ref[:, c0:c0 + width])
        ref[0] = (_silu(u) if i in EVEN_GATE_OUTPUTS else u).astype(BF16)
        c0 += width
    kvg = _dot(xb, w_ref[:, c0:])
    out_refs[-3][0] = kvg[:, :SWA_KVW].astype(BF16)
    out_refs[-2][0] = kvg[:, SWA_KVW:2 * SWA_KVW].astype(BF16)
    one_lane = (lax.broadcasted_iota(jnp.int32, (1, GD_PAD), 1) == GD_COLS).astype(F32)
    out_refs[-1][0] = (kvg[:, 2 * SWA_KVW:] + one_lane).astype(BF16)


def _inproj_even(x, w, tm):
    bn, ln, _ = x.shape
    return pl.pallas_call(
        _inproj_even_kernel,
        grid=(bn, ln // tm),
        in_specs=[
            pl.BlockSpec((1, tm, D_MODEL), lambda b, t: (b, t, 0)),
            pl.BlockSpec((D_MODEL, EVEN_COLS), lambda b, t: (0, 0), pipeline_mode=pl.Buffered(1)),
        ],
        out_specs=[pl.BlockSpec((1, tm, width), lambda b, t: (b, t, 0)) for width in EVEN_OUT_WIDTHS],
        out_shape=[jax.ShapeDtypeStruct((bn, ln, width), BF16) for width in EVEN_OUT_WIDTHS],
        compiler_params=_params("parallel", "parallel"),
        name="inproj_even",
    )(x, w)


def _gla_kernel(q_ref, k_ref, v_ref, gd_ref, gdn_ref, wup_ref, wupn_ref, tri_ref, mask_ref, o_ref,
                st_ref, flag_ref, logd_ref, sq_ref, kt_ref, u_ref, et_ref, cum_ref, qf_ref, kf_ref,
                *, tile):
    d = pl.program_id(0)
    b = pl.program_id(1)
    t = pl.program_id(2)
    step = (d * pl.num_programs(1) + b) * pl.num_programs(2) + t
    slot = lax.rem(step, 2)

    def gate(gd_blk, w_blk, sl):
        a2 = _dot(gd_blk, w_blk)
        logd_ref[sl] = ((jnp.minimum(a2, 0.0) - jnp.log2(1.0 + jnp.exp2(-jnp.abs(a2)))) * (LN2 / GLA_TAU)).astype(BF16)
        flag_ref[sl] = (jnp.min(a2) < GLA_SAFE_LOGIT * LOG2E).astype(jnp.int32)

    @pl.when(step == 0)
    def _():
        gate(gd_ref[0], wup_ref[0], slot)

    @pl.when(t == 0)
    def _():
        st_ref[...] = jnp.zeros_like(st_ref)

    unsafe = flag_ref[slot] != 0

    @pl.when(jnp.logical_not(unsafe))
    def _():
        _gla_fast_tile(d, q_ref, k_ref, v_ref, tri_ref, mask_ref, o_ref, st_ref, logd_ref.at[slot], sq_ref,
                       kt_ref, u_ref, et_ref, cum_ref, tile)
        gate(gdn_ref[0], wupn_ref[0], 1 - slot)

    @pl.when(unsafe)
    def _():
        _gla_pairwise_tile(d, q_ref, k_ref, v_ref, tri_ref, o_ref, st_ref, logd_ref.at[slot], cum_ref,
                           qf_ref, kf_ref, tile)
        gate(gdn_ref[0], wupn_ref[0], 1 - slot)


def _gla_pairwise_tile(d, q_ref, k_ref, v_ref, tri_ref, o_ref, st_ref, logd_ref, cum_ref, qf_ref, kf_ref, tile):
    nchunk = tile // GLA_CHUNK
    tri = tri_ref[0]
    jrow = lax.broadcasted_iota(jnp.int32, (GLA_CHUNK, GLA_CHUNK), 0)
    icol = lax.broadcasted_iota(jnp.int32, (GLA_CHUNK, GLA_CHUNK), 1)

    def chunk(c, carry):
        cc = c + d * (nchunk - 1 - 2 * c)
        rows = pl.ds(pl.multiple_of(cc * GLA_CHUNK, GLA_CHUNK), GLA_CHUNK)
        cum_ref[0] = _dot(tri, logd_ref[rows, :])
        for h in range(GLA_HEADS):
            hs = slice(h * GLA_DK, (h + 1) * GLA_DK)
            b = cum_ref[0, :GLA_CHUNK, hs]
            tot = cum_ref[0, GLA_CHUNK:GLA_CHUNK + 1, hs]
            qf_ref[...] = q_ref[0, rows, hs].astype(F32)
            kf_ref[...] = k_ref[0, rows, hs].astype(F32)
            v = v_ref[0, rows, hs]

            def pair_rows(g, st_t):
                base = pl.multiple_of(g * 8, 8)
                b8 = cum_ref[0, pl.ds(base, 8), hs]
                q8 = qf_ref[pl.ds(base, 8), :]
                for r in range(8):
                    i = base + r
                    w = jnp.exp(jnp.minimum(b8[r:r + 1] - cum_ref[0, :GLA_CHUNK, hs], 0.0))
                    col = jnp.sum(q8[r:r + 1] * kf_ref[...] * w, axis=1, keepdims=True)
                    valid = (1 - 2 * d) * (jrow - i) <= -d
                    st_t = jnp.where((icol == i) & valid, col, st_t)
                return st_t

            s = lax.fori_loop(0, GLA_CHUNK // 8, pair_rows, jnp.zeros((GLA_CHUNK, GLA_CHUNK), F32)).T.astype(BF16)
            qt = (qf_ref[...] * jnp.exp(b)).astype(BF16)
            kd = (kf_ref[...] * jnp.exp(tot - b)).astype(BF16)
            st = st_ref[h]
            vs = jnp.concatenate([v, st.T.astype(BF16)], axis=0)
            o_ref[0, 0, rows, hs] = _dot(jnp.concatenate([s, qt], axis=1), vs).astype(BF16)
            st_ref[h] = st * jnp.exp(tot) + _dot_tn(v, kd)
        return carry

    lax.fori_loop(0, nchunk, chunk, 0)


def _gla_fast_tile(d, q_ref, k_ref, v_ref, tri_ref, mask_ref, o_ref, st_ref, logd_ref, sq_ref,
                   kt_ref, u_ref, et_ref, cum_ref, tile):
    nchunk = tile // GLA_CHUNK
    tri = tri_ref[0]
    keep = mask_ref[0] > 0.0
    heads = [slice(h * GLA_DK, (h + 1) * GLA_DK) for h in range(GLA_HEADS)]
    s_cols = [slice(2 * h * GLA_DK, (2 * h + 1) * GLA_DK) for h in range(GLA_HEADS)]
    q_cols = [slice((2 * h + 1) * GLA_DK, (2 * h + 2) * GLA_DK) for h in range(GLA_HEADS)]
    sq_cols = [slice(2 * h * GLA_DK, (2 * h + 2) * GLA_DK) for h in range(GLA_HEADS)]

    for c in range(nchunk):
        rows = slice(c * GLA_CHUNK, (c + 1) * GLA_CHUNK)
        cum_ref[c % 2] = _dot(tri, logd_ref[rows, :])
        for h, hs in enumerate(heads):
            b = cum_ref[c % 2, :GLA_CHUNK, hs]
            etot = jnp.exp(cum_ref[c % 2, GLA_CHUNK:GLA_CHUNK + 1, hs])
            sq_ref[rows, q_cols[h]] = (q_ref[0, rows, hs].astype(F32) * jnp.exp(b)).astype(BF16)
            kt_ref[rows, hs] = (k_ref[0, rows, hs].astype(F32) * jnp.exp(-b)).astype(BF16)
            et_ref[c, :, hs] = etot

    for c in range(nchunk):
        rows = slice(c * GLA_CHUNK, (c + 1) * GLA_CHUNK)
        for h, hs in enumerate(heads):
            sq_ref[rows, s_cols[h]] = jnp.where(
                keep, _dot_nt(sq_ref[rows, q_cols[h]], kt_ref[rows, hs]), 0.0).astype(BF16)
            u_ref[c, h] = _dot_tn(v_ref[0, rows, hs], kt_ref[rows, hs])

    for c in range(nchunk):
        cc = c + d * (nchunk - 1 - 2 * c)
        rows = pl.ds(pl.multiple_of(cc * GLA_CHUNK, GLA_CHUNK), GLA_CHUNK)
        for h, hs in enumerate(heads):
            st = st_ref[h]
            vs = jnp.concatenate([v_ref[0, rows, hs], st.T.astype(BF16)], axis=0)
            o_ref[0, 0, rows, hs] = _dot(sq_ref[rows, sq_cols[h]], vs).astype(BF16)
            st_ref[h] = (st + u_ref[cc, h]) * et_ref[cc, :, hs]


def _gla(q, k, v, gd, wup, tri, mask, tile):
    bn, ln, _ = q.shape
    nt = ln // tile

    def tok(d, b, t):
        return t + d * (nt - 1 - 2 * t)

    def nxt(d, b, t):
        roll_t = t == nt - 1
        roll_b = roll_t & (b == bn - 1)
        d2 = jnp.minimum(jnp.where(roll_b, d + 1, d), 1)
        b2 = jnp.where(roll_b, 0, jnp.where(roll_t, b + 1, b))
        return d2, b2, jnp.where(roll_t, 0, t + 1)

    def gd_next(d, b, t):
        d2, b2, t2 = nxt(d, b, t)
        return b2, tok(d2, b2, t2), 0

    return pl.pallas_call(
        functools.partial(_gla_kernel, tile=tile),
        grid=(2, bn, nt),
        in_specs=[
            pl.BlockSpec((1, tile, GLA_WIDTH), lambda d, b, t: (b, tok(d, b, t), 0)),
            pl.BlockSpec((1, tile, GLA_WIDTH), lambda d, b, t: (b, tok(d, b, t), 0)),
            pl.BlockSpec((1, tile, GLA_WIDTH), lambda d, b, t: (b, tok(d, b, t), 0)),
            pl.BlockSpec((1, tile, GD_PAD), lambda d, b, t: (b, tok(d, b, t), 0)),
            pl.BlockSpec((1, tile, GD_PAD), gd_next),
            pl.BlockSpec((1, GD_PAD, GLA_WIDTH), lambda d, b, t: (d, 0, 0)),
            pl.BlockSpec((1, GD_PAD, GLA_WIDTH), lambda d, b, t: (nxt(d, b, t)[0], 0, 0)),
            pl.BlockSpec((1, GLA_CHUNK + TOT_ROWS, GLA_CHUNK), lambda d, b, t: (d, 0, 0)),
            pl.BlockSpec((1, GLA_CHUNK, GLA_CHUNK), lambda d, b, t: (d, 0, 0)),
        ],
        out_specs=pl.BlockSpec((1, 1, tile, GLA_WIDTH), lambda d, b, t: (d, b, tok(d, b, t), 0)),
        out_shape=jax.ShapeDtypeStruct((2, bn, ln, GLA_WIDTH), BF16),
        scratch_shapes=[
            pltpu.VMEM((GLA_HEADS, GLA_DK, GLA_DK), F32),
            pltpu.SMEM((2,), jnp.int32),
            pltpu.VMEM((2, tile, GLA_WIDTH), BF16),
            pltpu.VMEM((tile, 2 * GLA_WIDTH), BF16),
            pltpu.VMEM((tile, GLA_WIDTH), BF16),
            pltpu.VMEM((tile // GLA_CHUNK, GLA_HEADS, GLA_DK, GLA_DK), F32),
            pltpu.VMEM((tile // GLA_CHUNK, 1, GLA_WIDTH), F32),
            pltpu.VMEM((2, GLA_CHUNK + TOT_ROWS, GLA_WIDTH), F32),
            pltpu.VMEM((GLA_CHUNK, GLA_DK), F32),
            pltpu.VMEM((GLA_CHUNK, GLA_DK), F32),
        ],
        compiler_params=_params("arbitrary", "arbitrary", "arbitrary"),
        name="gla_scan",
    )(q, k, v, gd, gd, wup, wup, tri, mask)


def _gla_constants():
    i = np.arange(GLA_CHUNK)[:, None]
    j = np.arange(GLA_CHUNK)[None, :]
    lower = (j <= i).astype(np.float32)
    upper = (j >= i).astype(np.float32)
    tri = np.zeros((2, GLA_CHUNK + TOT_ROWS, GLA_CHUNK), np.float32)
    tri[0, :GLA_CHUNK] = lower
    tri[1, :GLA_CHUNK] = upper
    tri[:, GLA_CHUNK:] = 1.0
    mask = np.stack([(j <= i), (j > i)]).astype(np.float32)
    return jnp.asarray(tri, BF16), jnp.asarray(mask, F32)


def _rel_tables():
    i = np.arange(BLOCK)[:, None]
    j = np.arange(3 * BLOCK)[None, :]
    rel = j - BLOCK - i
    half = REL_BUCKETS // 2
    max_exact = half // 2
    n = np.abs(rel)
    large = max_exact + (np.log(np.maximum(n, 1) / max_exact) / np.log(REL_MAX_DIST / max_exact)
                         * (half - max_exact)).astype(np.int32)
    large = np.minimum(large, half - 1)
    bucket = (rel > 0).astype(np.int32) * half + np.where(n < max_exact, n, large)
    band = np.abs(rel) <= WINDOW
    col = np.broadcast_to(j, rel.shape)
    valid = np.stack([band & (col >= BLOCK), band, band & (col < 2 * BLOCK)])
    return np.ascontiguousarray(bucket.T).astype(np.int32), np.ascontiguousarray(valid.transpose(0, 2, 1)).astype(np.int32)


def _bias_kernel(rb_ref, bucket_ref, valid_ref, out_ref):
    h = pl.program_id(0)
    bucket = bucket_ref[...]
    acc = jnp.zeros(bucket.shape, F32)
    for kk in range(REL_BUCKETS):
        acc = jnp.where(bucket == kk, rb_ref[kk, h] * LOG2E, acc)
    for kind in range(3):
        out_ref[kind, 0] = jnp.where(valid_ref[kind] > 0, acc, NEG_BIG)


def _bias_table(rel_bias):
    bucket, valid = _rel_tables()
    return pl.pallas_call(
        _bias_kernel,
        grid=(SWA_HEADS,),
        in_specs=[
            pl.BlockSpec(memory_space=pltpu.SMEM),
            pl.BlockSpec((3 * BLOCK, BLOCK), lambda h: (0, 0)),
            pl.BlockSpec((3, 3 * BLOCK, BLOCK), lambda h: (0, 0, 0)),
        ],
        out_specs=pl.BlockSpec((3, 1, 3 * BLOCK, BLOCK), lambda h: (0, h, 0, 0)),
        out_shape=jax.ShapeDtypeStruct((3, SWA_HEADS, 3 * BLOCK, BLOCK), F32),
        compiler_params=_params("arbitrary"),
        name="swa_bias_table",
    )(rel_bias.astype(F32), jnp.asarray(bucket), jnp.asarray(valid))


def _swa_kernel(sink_ref, q_ref, z_ref, kp_ref, kc_ref, kn_ref, vp_ref, vc_ref, vn_ref, bias_ref, o_ref,
                km_ref, va_ref, st_ref, pt_ref, ot_ref, *, nq):
    n = pl.program_id(1)
    nsteps = pl.num_programs(1)
    combos = [(g, e) for g in range(SWA_KV_HEADS) for e in range(2)]
    ncomb = len(combos)
    kcat = jnp.concatenate([kp_ref[0], kc_ref[0], kn_ref[0]], axis=0)
    kswap = jnp.concatenate([kcat[:, SWA_HD:], kcat[:, :SWA_HD]], axis=1)
    low = lax.broadcasted_iota(jnp.int32, kcat.shape, 1) < SWA_HD
    zero = jnp.zeros_like(kcat)
    km_ref[0] = jnp.where(low, kcat, zero)
    km_ref[1] = jnp.where(low, zero, kswap)
    km_ref[2] = jnp.where(low, kswap, zero)
    km_ref[3] = jnp.where(low, zero, kcat)
    vt = jnp.concatenate([vp_ref[0], vc_ref[0], vn_ref[0]], axis=0).astype(F32).T.astype(BF16)
    for g in range(SWA_KV_HEADS):
        va_ref[g, :SWA_HD, :] = vt[g * SWA_HD:(g + 1) * SWA_HD]
        va_ref[g, SWA_HD:, :] = jnp.ones((SWA_ONES_ROWS, vt.shape[1]), BF16)
    half = lax.broadcasted_iota(jnp.int32, (1, 2 * BLOCK), 1) < BLOCK
    pairs_per_kv = SWA_HEADS // SWA_KV_HEADS // 2
    pairs = [[slice((pairs_per_kv * g + i) * 128, (pairs_per_kv * g + i + 1) * 128) for i in range(pairs_per_kv)]
             for g in range(SWA_KV_HEADS)]
    for qb in range(nq):
        qrows = slice(qb * BLOCK, (qb + 1) * BLOCK)
        keys = slice(qb * BLOCK, (qb + 3) * BLOCK)
        kind = 1
        if qb == 0:
            kind = jnp.where(n == 0, 0, kind)
        if qb == nq - 1:
            kind = jnp.where(n == nsteps - 1, 2, kind)
        slot = (qb % 2) * ncomb
        for c, (g, e) in enumerate(combos):
            h0 = 2 * pairs_per_kv * g + e
            qg = jnp.concatenate([q_ref[0, qrows, ps] for ps in pairs[g]], axis=0)
            st_ref[slot + c] = (_dot_nt(km_ref[c, keys, :], qg)
                                + jnp.concatenate([bias_ref[kind, h0], bias_ref[kind, h0 + 2]], axis=1))
        stats = []
        for c, (g, e) in enumerate(combos):
            h0 = 2 * pairs_per_kv * g + e
            sink = jnp.where(half, sink_ref[0, h0], sink_ref[0, h0 + 2]) * LOG2E
            m = jnp.maximum(jnp.max(st_ref[slot + c], axis=0, keepdims=True), sink)
            pt_ref[slot + c] = jnp.exp2(st_ref[slot + c] - m).astype(BF16)
            stats.append(jnp.exp2(sink - m))
        for c, (g, e) in enumerate(combos):
            ot = _dot(va_ref[g, :, keys], pt_ref[slot + c])
            ot_ref[slot + c] = ot[:SWA_HD] * (1.0 / (ot[SWA_HD:SWA_HD + 1] + stats[c]))
        for g in range(SWA_KV_HEADS):
            for i, ps in enumerate(pairs[g]):
                cs = slice(i * BLOCK, (i + 1) * BLOCK)
                o = jnp.concatenate([ot_ref[slot + 2 * g, :, cs], ot_ref[slot + 2 * g + 1, :, cs]],
                                    axis=0).T
                o_ref[0, qrows, ps] = (o * _silu(z_ref[0, qrows, ps].astype(F32))).astype(BF16)


def _swa(q, z, k, v, bias, sink, nq):
    bn, ln, _ = q.shape
    nb = ln // BLOCK
    assert nb % nq == 0 and nb >= 2
    prev = pl.BlockSpec((1, BLOCK, SWA_KVW), lambda b, n: (b, jnp.maximum(n * nq - 1, 0), 0))
    own = pl.BlockSpec((1, nq * BLOCK, SWA_KVW), lambda b, n: (b, n, 0))
    nxt = pl.BlockSpec((1, BLOCK, SWA_KVW), lambda b, n: (b, jnp.minimum((n + 1) * nq, nb - 1), 0))
    return pl.pallas_call(
        functools.partial(_swa_kernel, nq=nq),
        grid=(bn, nb // nq),
        in_specs=[
            pl.BlockSpec(memory_space=pltpu.SMEM),
            pl.BlockSpec((1, nq * BLOCK, SWA_WIDTH), lambda b, n: (b, n, 0)),
            pl.BlockSpec((1, nq * BLOCK, SWA_WIDTH), lambda b, n: (b, n, 0)),
            prev, own, nxt,
            prev, own, nxt,
            pl.BlockSpec((3, SWA_HEADS, 3 * BLOCK, BLOCK), lambda b, n: (0, 0, 0, 0)),
        ],
        out_specs=pl.BlockSpec((1, nq * BLOCK, SWA_WIDTH), lambda b, n: (b, n, 0)),
        out_shape=jax.ShapeDtypeStruct((bn, ln, SWA_WIDTH), BF16),
        scratch_shapes=[
            pltpu.VMEM((2 * SWA_KV_HEADS, (nq + 2) * BLOCK, SWA_KVW), BF16),
            pltpu.VMEM((SWA_KV_HEADS, SWA_HD + SWA_ONES_ROWS, (nq + 2) * BLOCK), BF16),
            pltpu.VMEM((4 * SWA_KV_HEADS, 3 * BLOCK, 2 * BLOCK), F32),
            pltpu.VMEM((4 * SWA_KV_HEADS, 3 * BLOCK, 2 * BLOCK), BF16),
            pltpu.VMEM((4 * SWA_KV_HEADS, SWA_HD, 2 * BLOCK), F32),
        ],
        compiler_params=_params("parallel", "arbitrary"),
        name="swa_attention",
    )(sink.reshape(1, SWA_HEADS).astype(F32), q, z, k, k, k, v, v, v, bias)


ODD_CHUNK = 256
HALO = 8
LN_ROWS = 256


def _tail_kernel(of_ref, ob_ref, z_ref, yb_ref, x_ref, wmix_ref, ng_ref, lg0_ref, lb0_ref,
                 win_ref, cw_ref, wout_ref, lg1_ref, lb1_ref, out_ref,
                 x1_ref, halo_ref, xcat_ref, u_ref, th_ref, mixed_ref, *, tm, nt):
    s = pl.program_id(0)
    last_step = pl.num_programs(0) - 1

    @pl.when(s == 0)
    def _():
        halo_ref[...] = jnp.zeros_like(halo_ref)

    @pl.when(s < last_step)
    def _even_tail():
        slot = lax.rem(s, 2)
        for r0 in range(0, tm, LN_ROWS):
            rs = slice(r0, r0 + LN_ROWS)
            o = of_ref[0, 0, rs, :].astype(F32) + ob_ref[0, 0, rs, :].astype(F32)
            parts = []
            for h in range(GLA_HEADS):
                oh = o[:, h * GLA_DK:(h + 1) * GLA_DK]
                parts.append(oh * lax.rsqrt(jnp.mean(oh * oh, axis=-1, keepdims=True) + NORM_EPS))
            on = jnp.concatenate(parts, axis=1) * ng_ref[...]
            ya = (on * z_ref[0, rs, :].astype(F32)).astype(BF16)
            sub = _dot(ya, wmix_ref[:GLA_WIDTH, :]) + _dot(yb_ref[0, rs, :], wmix_ref[GLA_WIDTH:, :])
            x1_ref[slot, rs, :] = _layer_norm(DN_ALPHA * x_ref[0, rs, :] + sub, lg0_ref[...], lb0_ref[...])

    @pl.when(s > 0)
    def _odd_layer():
        t = lax.rem(s - 1, nt)
        cur = x1_ref.at[lax.rem(s - 1, 2)]
        nxt = x1_ref.at[lax.rem(s, 2)]
        _odd_body(t, nt, halo_ref, cur, nxt, win_ref, cw_ref, wout_ref, lg1_ref, lb1_ref, out_ref,
                  xcat_ref, u_ref, th_ref, mixed_ref, tm)
        halo_ref[...] = cur[tm - HALO:tm, :]


def _odd_body(t, nt, prev_ref, x_ref, next_ref, win_ref, cw_ref, wout_ref, lg_ref, lb_ref, out_ref,
              xcat_ref, u_ref, th_ref, mixed_ref, tm):
    main = slice(HALO, HALO + tm)
    xcat_ref[...] = jnp.concatenate([prev_ref[...], x_ref[...], next_ref[0:HALO, :]], axis=0).astype(BF16)
    nj = CONV_WIDTH // ODD_CHUNK

    def in_proj(j):
        for i in range(4):
            u_ref[j % 2, :, i * ODD_CHUNK:(i + 1) * ODD_CHUNK] = _dot(
                xcat_ref[...], win_ref[:, i * CONV_WIDTH + j * ODD_CHUNK:i * CONV_WIDTH + (j + 1) * ODD_CHUNK])

    in_proj(0)
    for j in range(nj):
        cols = slice(j * ODD_CHUNK, (j + 1) * ODD_CHUNK)
        ub = u_ref.at[j % 2]
        if j + 1 < nj:
            in_proj(j + 1)
        th_ref[...] = ub[:, ODD_CHUNK:2 * ODD_CHUNK] * ub[:, 2 * ODD_CHUNK:3 * ODD_CHUNK]
        first = pl.ds(HALO - 1, 1)
        last = pl.ds(HALO + tm, 1)
        th_ref[first, :] = jnp.where(t == 0, 0.0, th_ref[first, :])
        th_ref[last, :] = jnp.where(t == nt - 1, 0.0, th_ref[last, :])
        conv = (cw_ref[0:1, cols] * th_ref[HALO - 1:HALO - 1 + tm, :]
                + cw_ref[1:2, cols] * th_ref[main, :]
                + cw_ref[2:3, cols] * th_ref[HALO + 1:HALO + 1 + tm, :])
        mixed_ref[:, cols] = (_silu(ub[main, 3 * ODD_CHUNK:]) * ub[main, :ODD_CHUNK] * conv).astype(BF16)
    for r0 in range(0, tm, LN_ROWS):
        rs = slice(r0, r0 + LN_ROWS)
        acc = _dot(mixed_ref[rs, :], wout_ref[...])
        out_ref[0, rs, :] = _layer_norm(DN_ALPHA * x_ref[rs, :] + acc, lg_ref[...], lb_ref[...])


def _tail(o, za, yb, x, wmix, ng, lg0, lb0, win, cw, wout, lg1, lb1, tm):
    bn, ln, _ = x.shape
    nt = ln // tm
    ntiles = bn * nt
    rows = tm + 2 * HALO

    def tile_in(s):
        g = jnp.minimum(s, ntiles - 1)
        return g // nt, g % nt

    def tile_out(s):
        g = jnp.maximum(s - 1, 0)
        return g // nt, g % nt

    whole = lambda *shape: pl.BlockSpec(shape, lambda s: (0,) * len(shape), pipeline_mode=pl.Buffered(1))
    return pl.pallas_call(
        functools.partial(_tail_kernel, tm=tm, nt=nt),
        grid=(ntiles + 1,),
        in_specs=[
            pl.BlockSpec((1, 1, tm, GLA_WIDTH), lambda s: (0, *tile_in(s), 0)),
            pl.BlockSpec((1, 1, tm, GLA_WIDTH), lambda s: (1, *tile_in(s), 0)),
            pl.BlockSpec((1, tm, GLA_WIDTH), lambda s: (*tile_in(s), 0)),
            pl.BlockSpec((1, tm, SWA_WIDTH), lambda s: (*tile_in(s), 0)),
            pl.BlockSpec((1, tm, D_MODEL), lambda s: (*tile_in(s), 0)),
            whole(GLA_WIDTH + SWA_WIDTH, D_MODEL),
            whole(1, GLA_WIDTH), whole(1, D_MODEL), whole(1, D_MODEL),
            whole(D_MODEL, 4 * CONV_WIDTH), whole(3, CONV_WIDTH), whole(CONV_WIDTH, D_MODEL),
            whole(1, D_MODEL), whole(1, D_MODEL),
        ],
        out_specs=pl.BlockSpec((1, tm, D_MODEL), lambda s: (*tile_out(s), 0)),
        out_shape=jax.ShapeDtypeStruct((bn, ln, D_MODEL), F32),
        scratch_shapes=[
            pltpu.VMEM((2, tm, D_MODEL), F32),
            pltpu.VMEM((HALO, D_MODEL), F32),
            pltpu.VMEM((rows, D_MODEL), BF16),
            pltpu.VMEM((2, rows, 4 * ODD_CHUNK), F32),
            pltpu.VMEM((rows, ODD_CHUNK), F32),
            pltpu.VMEM((tm, CONV_WIDTH), BF16),
        ],
        compiler_params=_params("arbitrary"),
        name="even_tail_odd_layer",
    )(o, o, za, yb, x, wmix, ng, lg0, lb0, win, cw, wout, lg1, lb1)


PREP_ROWS = 256


def _prep_even_kernel(w_ref, o_ref):
    g0 = GLA_COLS
    qb0 = g0 + GD_COLS
    kb0 = qb0 + SWA_WIDTH
    vb0 = kb0 + SWA_KVW
    zb0 = vb0 + SWA_KVW
    o_ref[:, :GLA_WIDTH] = (w_ref[:, :GLA_WIDTH] * (GLA_DK ** -0.5)).astype(BF16)
    o_ref[:, GLA_WIDTH:GLA_COLS] = w_ref[:, GLA_WIDTH:GLA_COLS].astype(BF16)
    c = GLA_COLS
    o_ref[:, c:c + SWA_WIDTH] = (w_ref[:, qb0:qb0 + SWA_WIDTH] * (SWA_HD ** -0.5 * LOG2E)).astype(BF16)
    c += SWA_WIDTH
    o_ref[:, c:c + SWA_WIDTH] = w_ref[:, zb0:zb0 + SWA_WIDTH].astype(BF16)
    c += SWA_WIDTH
    o_ref[:, c:c + SWA_KVW] = w_ref[:, kb0:kb0 + SWA_KVW].astype(BF16)
    c += SWA_KVW
    o_ref[:, c:c + SWA_KVW] = w_ref[:, vb0:vb0 + SWA_KVW].astype(BF16)
    c += SWA_KVW
    lane = lax.broadcasted_iota(jnp.int32, (PREP_ROWS, GD_PAD), 1)
    o_ref[:, c:] = jnp.where(lane < GD_COLS, w_ref[:, g0:g0 + GD_PAD], 0.0).astype(BF16)


def _prep_even(w_in_layers, w_up_f, b_f, w_up_b, b_b, norm_g, w_out_layers):
    even_in = w_in_layers.shape[2]
    w = pl.pallas_call(
        _prep_even_kernel,
        grid=(D_MODEL // PREP_ROWS,),
        in_specs=[pl.BlockSpec((None, PREP_ROWS, even_in), lambda r: (0, r, 0))],
        out_specs=pl.BlockSpec((PREP_ROWS, EVEN_COLS), lambda r: (r, 0)),
        out_shape=jax.ShapeDtypeStruct((D_MODEL, EVEN_COLS), BF16),
        compiler_params=_params("parallel"),
        name="prep_even_weights",
    )(w_in_layers)
    zr = jnp.zeros_like(w_up_f)
    zpad = jnp.zeros((GD_PAD - GD_COLS - 1, GLA_WIDTH), w_up_f.dtype)
    wup = (jnp.stack([jnp.concatenate([w_up_f, zr, b_f[None], zpad], axis=0),
                      jnp.concatenate([zr, w_up_b, b_b[None], zpad], axis=0)]) * LOG2E).astype(BF16)
    ng = jnp.tile(norm_g.astype(F32), GLA_HEADS).reshape(1, GLA_WIDTH)
    return w, wup, ng, _layer0_bf16(w_out_layers)


def _cast_kernel(w_ref, o_ref):
    o_ref[...] = w_ref[...].astype(BF16)


def _layer0_bf16(w_layers):
    _, rows, cols = w_layers.shape
    return pl.pallas_call(
        _cast_kernel,
        grid=(rows // PREP_ROWS,),
        in_specs=[pl.BlockSpec((None, PREP_ROWS, cols), lambda r: (0, r, 0))],
        out_specs=pl.BlockSpec((PREP_ROWS, cols), lambda r: (r, 0)),
        out_shape=jax.ShapeDtypeStruct((rows, cols), BF16),
        compiler_params=_params("parallel"),
        name="cast_weights",
    )(w_layers)


def _prep_odd(w_in_layers, conv_w, w_out_layers):
    return _layer0_bf16(w_in_layers), conv_w.astype(F32), _layer0_bf16(w_out_layers)


def _trunk(x, even, odd, bias_tab, sink, ln_g, ln_b, consts, tm=512, tm_in=1024, gla_tile=2048, swa_nq=16):
    w, wup, ng, wmix = even
    win, cw, wout = odd
    tri, mask = consts
    qa, ka, va, za, qb, zb, kb, vb, gd = _inproj_even(x, w, tm_in)
    o = _gla(qa, ka, va, gd, wup, tri, mask, gla_tile)
    yb = _swa(qb, zb, kb, vb, bias_tab, sink, swa_nq)
    lg = ln_g.astype(F32).reshape(DEPTH, 1, D_MODEL)
    lb = ln_b.astype(F32).reshape(DEPTH, 1, D_MODEL)
    return _tail(o, za, yb, x, wmix, ng, lg[0], lb[0], win, cw, wout, lg[1], lb[1], tm)


def kernel(x_prompt, x_sample, w_in_even, gla_w_up_fwd, gla_b_fwd, gla_w_up_bwd, gla_b_bwd, gla_norm_g, swa_sink,
           rel_bias, w_out_even, w_in_odd, conv_w, w_out_odd, ln_g, ln_b):
    even = _prep_even(w_in_even, gla_w_up_fwd[0], gla_b_fwd[0], gla_w_up_bwd[0], gla_b_bwd[0], gla_norm_g[0],
                      w_out_even)
    odd = _prep_odd(w_in_odd, conv_w[0], w_out_odd)
    bias_tab = _bias_table(rel_bias)
    consts = _gla_constants()
    run = lambda x: _trunk(x, even, odd, bias_tab, swa_sink[0], ln_g, ln_b, consts)
    return (run(x_prompt), run(x_sample))
```

```python
import functools
import math

import numpy as np
import jax
import jax.numpy as jnp
from jax import lax
from jax.experimental import pallas as pl
from jax.experimental.pallas import tpu as pltpu

F32 = jnp.float32
BF16 = jnp.bfloat16

D_MODEL = 1024
DEPTH = 2
GLA_HEADS = 4
GLA_DK = 128
GLA_WIDTH = 512
GLA_RANK = 16
GLA_TAU = 16.0
SWA_HEADS = 8
SWA_KV_HEADS = 2
SWA_HD = 64
SWA_WIDTH = 512
SWA_KVW = 128
WINDOW = 128
BLOCK = 128
REL_BUCKETS = 32
REL_MAX_DIST = 128
CONV_WIDTH = 1024
DN_ALPHA = (2 * DEPTH) ** 0.25
LN_EPS = 1e-5
NORM_EPS = 1e-6
NEG_BIG = -1e30
LOG2E = math.log2(math.e)
LN2 = math.log(2.0)
SWA_ONES_ROWS = 16

GLA_COLS = 4 * GLA_WIDTH
SWA_COLS = 2 * SWA_WIDTH + 2 * SWA_KVW
GD_COLS = 2 * GLA_RANK
GD_PAD = 128
EVEN_COLS = GLA_COLS + SWA_COLS + GD_PAD

GLA_CHUNK = 128
TOT_ROWS = 16
GLA_SAFE_LOGIT = -8.0
VMEM_LIMIT = 56 * 1024 * 1024


def _dot(a, b):
    return jnp.dot(a, b, preferred_element_type=F32)


def _dot_nt(a, b):
    return lax.dot_general(a, b, (((1,), (1,)), ((), ())), preferred_element_type=F32)


def _dot_tn(a, b):
    return lax.dot_general(a, b, (((0,), (0,)), ((), ())), preferred_element_type=F32)


def _silu(z):
    return z / (1.0 + jnp.exp(-z))


def _deepnorm(x, sub_over_alpha, g, b):
    y = x + sub_over_alpha
    mu = jnp.mean(y, axis=-1, keepdims=True)
    yc = y - mu
    var = jnp.mean(yc * yc, axis=-1, keepdims=True)
    return yc * lax.rsqrt(var + LN_EPS / (DN_ALPHA * DN_ALPHA)) * g + b


def _params(*sem):
    return pltpu.CompilerParams(dimension_semantics=sem, vmem_limit_bytes=VMEM_LIMIT)


EVEN_OUT_WIDTHS = (GLA_WIDTH,) * 4 + (SWA_WIDTH, SWA_WIDTH, SWA_KVW, SWA_KVW, GD_PAD)
EVEN_GATE_OUTPUTS = (3,)


def _inproj_even_kernel(x_ref, w_ref, *out_refs):
    xb = x_ref[0].astype(BF16)
    c0 = 0
    for i, (ref, width) in enumerate(zip(out_refs[:-3], EVEN_OUT_WIDTHS[:-3])):
        u = _dot(xb, w_ref[:, c0:c0 + width])
        ref[0] = (_silu(u) if i in EVEN_GATE_OUTPUTS else u).astype(BF16)
        c0 += width
    kvg = _dot(xb, w_ref[:, c0:])
    out_refs[-3][0] = kvg[:, :SWA_KVW].astype(BF16)
    out_refs[-2][0] = kvg[:, SWA_KVW:2 * SWA_KVW].astype(BF16)
    one_lane = (lax.broadcasted_iota(jnp.int32, (1, GD_PAD), 1) == GD_COLS).astype(F32)
    out_refs[-1][0] = (kvg[:, 2 * SWA_KVW:] + one_lane).astype(BF16)


def _inproj_even(x, w, tm):
    bn, ln, _ = x.shape
    return pl.pallas_call(
        _inproj_even_kernel,
        grid=(bn, ln // tm),
        in_specs=[
            pl.BlockSpec((1, tm, D_MODEL), lambda b, t: (b, t, 0)),
            pl.BlockSpec((D_MODEL, EVEN_COLS), lambda b, t: (0, 0), pipeline_mode=pl.Buffered(1)),
        ],
        out_specs=[pl.BlockSpec((1, tm, width), lambda b, t: (b, t, 0)) for width in EVEN_OUT_WIDTHS],
        out_shape=[jax.ShapeDtypeStruct((bn, ln, width), BF16) for width in EVEN_OUT_WIDTHS],
        compiler_params=_params("parallel", "parallel"),
        name="inproj_even",
    )(x, w)


def _gla_kernel(q_ref, k_ref, v_ref, gd_ref, gdn_ref, wup_ref, wupn_ref, tri_ref, mask_ref, o_ref,
                st_ref, flag_ref, logd_ref, sq_ref, kt_ref, u_ref, et_ref, cum_ref, qf_ref, kf_ref,
                *, tile):
    d = pl.program_id(0)
    b = pl.program_id(1)
    t = pl.program_id(2)
    step = (d * pl.num_programs(1) + b) * pl.num_programs(2) + t
    slot = lax.rem(step, 2)

    def gate(gd_blk, w_blk, sl):
        a2 = _dot(gd_blk, w_blk)
        logd_ref[sl] = ((jnp.minimum(a2, 0.0) - jnp.log2(1.0 + jnp.exp2(-jnp.abs(a2)))) * (LN2 / GLA_TAU)).astype(BF16)
        flag_ref[sl] = (jnp.min(a2) < GLA_SAFE_LOGIT * LOG2E).astype(jnp.int32)

    @pl.when(step == 0)
    def _():
        gate(gd_ref[0], wup_ref[0], slot)

    @pl.when(t == 0)
    def _():
        st_ref[...] = jnp.zeros_like(st_ref)

    unsafe = flag_ref[slot] != 0

    @pl.when(jnp.logical_not(unsafe))
    def _():
        _gla_fast_tile(d, q_ref, k_ref, v_ref, tri_ref, mask_ref, o_ref, st_ref, logd_ref.at[slot], sq_ref,
                       kt_ref, u_ref, et_ref, cum_ref, tile)
        gate(gdn_ref[0], wupn_ref[0], 1 - slot)

    @pl.when(unsafe)
    def _():
        _gla_pairwise_tile(d, q_ref, k_ref, v_ref, tri_ref, o_ref, st_ref, logd_ref.at[slot], cum_ref,
                           qf_ref, kf_ref, tile)
        gate(gdn_ref[0], wupn_ref[0], 1 - slot)


def _gla_pairwise_tile(d, q_ref, k_ref, v_ref, tri_ref, o_ref, st_ref, logd_ref, cum_ref, qf_ref, kf_ref, tile):
    nchunk = tile // GLA_CHUNK
    tri = tri_ref[0]
    jrow = lax.broadcasted_iota(jnp.int32, (GLA_CHUNK, GLA_CHUNK), 0)
    icol = lax.broadcasted_iota(jnp.int32, (GLA_CHUNK, GLA_CHUNK), 1)

    def chunk(c, carry):
        cc = c + d * (nchunk - 1 - 2 * c)
        rows = pl.ds(pl.multiple_of(cc * GLA_CHUNK, GLA_CHUNK), GLA_CHUNK)
        cum_ref[0] = _dot(tri, logd_ref[rows, :])
        for h in range(GLA_HEADS):
            hs = slice(h * GLA_DK, (h + 1) * GLA_DK)
            b = cum_ref[0, :GLA_CHUNK, hs]
            tot = cum_ref[0, GLA_CHUNK:GLA_CHUNK + 1, hs]
            qf_ref[...] = q_ref[0, rows, hs].astype(F32)
            kf_ref[...] = k_ref[0, rows, hs].astype(F32)
            v = v_ref[0, rows, hs]

            def pair_rows(g, st_t):
                base = pl.multiple_of(g * 8, 8)
                b8 = cum_ref[0, pl.ds(base, 8), hs]
                q8 = qf_ref[pl.ds(base, 8), :]
                for r in range(8):
                    i = base + r
                    w = jnp.exp(jnp.minimum(b8[r:r + 1] - cum_ref[0, :GLA_CHUNK, hs], 0.0))
                    col = jnp.sum(q8[r:r + 1] * kf_ref[...] * w, axis=1, keepdims=True)
                    valid = (1 - 2 * d) * (jrow - i) <= -d
                    st_t = jnp.where((icol == i) & valid, col, st_t)
                return st_t

            s = lax.fori_loop(0, GLA_CHUNK // 8, pair_rows, jnp.zeros((GLA_CHUNK, GLA_CHUNK), F32)).T.astype(BF16)
            qt = (qf_ref[...] * jnp.exp(b)).astype(BF16)
            kd = (kf_ref[...] * jnp.exp(tot - b)).astype(BF16)
            st = st_ref[h]
            vs = jnp.concatenate([v, st.T.astype(BF16)], axis=0)
            o_ref[0, 0, rows, hs] = _dot(jnp.concatenate([s, qt], axis=1), vs).astype(BF16)
            st_ref[h] = st * jnp.exp(tot) + _dot_tn(v, kd)
        return carry

    lax.fori_loop(0, nchunk, chunk, 0)


def _gla_fast_tile(d, q_ref, k_ref, v_ref, tri_ref, mask_ref, o_ref, st_ref, logd_ref, sq_ref,
                   kt_ref, u_ref, et_ref, cum_ref, tile):
    nchunk = tile // GLA_CHUNK
    tri = tri_ref[0]
    keep = mask_ref[0] > 0.0
    heads = [slice(h * GLA_DK, (h + 1) * GLA_DK) for h in range(GLA_HEADS)]
    s_cols = [slice(2 * h * GLA_DK, (2 * h + 1) * GLA_DK) for h in range(GLA_HEADS)]
    q_cols = [slice((2 * h + 1) * GLA_DK, (2 * h + 2) * GLA_DK) for h in range(GLA_HEADS)]
    sq_cols = [slice(2 * h * GLA_DK, (2 * h + 2) * GLA_DK) for h in range(GLA_HEADS)]

    for c in range(nchunk):
        rows = slice(c * GLA_CHUNK, (c + 1) * GLA_CHUNK)
        cum_ref[c % 2] = _dot(tri, logd_ref[rows, :])
        for h, hs in enumerate(heads):
            b = cum_ref[c % 2, :GLA_CHUNK, hs]
            etot = jnp.exp(cum_ref[c % 2, GLA_CHUNK:GLA_CHUNK + 1, hs])
            sq_ref[rows, q_cols[h]] = (q_ref[0, rows, hs].astype(F32) * jnp.exp(b)).astype(BF16)
            kt_ref[rows, hs] = (k_ref[0, rows, hs].astype(F32) * jnp.exp(-b)).astype(BF16)
            et_ref[c, :, hs] = etot

    for c in range(nchunk):
        rows = slice(c * GLA_CHUNK, (c + 1) * GLA_CHUNK)
        for h, hs in enumerate(heads):
            sq_ref[rows, s_cols[h]] = jnp.where(
                keep, _dot_nt(sq_ref[rows, q_cols[h]], kt_ref[rows, hs]), 0.0).astype(BF16)
            u_ref[c, h] = _dot_tn(v_ref[0, rows, hs], kt_ref[rows, hs])

    for c in range(nchunk):
        cc = c + d * (nchunk - 1 - 2 * c)
        rows = pl.ds(pl.multiple_of(cc * GLA_CHUNK, GLA_CHUNK), GLA_CHUNK)
        for h, hs in enumerate(heads):
            st = st_ref[h]
            vs = jnp.concatenate([v_ref[0, rows, hs], st.T.astype(BF16)], axis=0)
            o_ref[0, 0, rows, hs] = _dot(sq_ref[rows, sq_cols[h]], vs).astype(BF16)
            st_ref[h] = (st + u_ref[cc, h]) * et_ref[cc, :, hs]


def _gla(q, k, v, gd, wup, tri, mask, tile):
    bn, ln, _ = q.shape
    nt = ln // tile

    def tok(d, b, t):
        return t + d * (nt - 1 - 2 * t)

    def nxt(d, b, t):
        roll_t = t == nt - 1
        roll_b = roll_t & (b == bn - 1)
        d2 = jnp.minimum(jnp.where(roll_b, d + 1, d), 1)
        b2 = jnp.where(roll_b, 0, jnp.where(roll_t, b + 1, b))
        return d2, b2, jnp.where(roll_t, 0, t + 1)

    def gd_next(d, b, t):
        d2, b2, t2 = nxt(d, b, t)
        return b2, tok(d2, b2, t2), 0

    return pl.pallas_call(
        functools.partial(_gla_kernel, tile=tile),
        grid=(2, bn, nt),
        in_specs=[
            pl.BlockSpec((1, tile, GLA_WIDTH), lambda d, b, t: (b, tok(d, b, t), 0)),
            pl.BlockSpec((1, tile, GLA_WIDTH), lambda d, b, t: (b, tok(d, b, t), 0)),
            pl.BlockSpec((1, tile, GLA_WIDTH), lambda d, b, t: (b, tok(d, b, t), 0)),
            pl.BlockSpec((1, tile, GD_PAD), lambda d, b, t: (b, tok(d, b, t), 0)),
            pl.BlockSpec((1, tile, GD_PAD), gd_next),
            pl.BlockSpec((1, GD_PAD, GLA_WIDTH), lambda d, b, t: (d, 0, 0)),
            pl.BlockSpec((1, GD_PAD, GLA_WIDTH), lambda d, b, t: (nxt(d, b, t)[0], 0, 0)),
            pl.BlockSpec((1, GLA_CHUNK + TOT_ROWS, GLA_CHUNK), lambda d, b, t: (d, 0, 0)),
            pl.BlockSpec((1, GLA_CHUNK, GLA_CHUNK), lambda d, b, t: (d, 0, 0)),
        ],
        out_specs=pl.BlockSpec((1, 1, tile, GLA_WIDTH), lambda d, b, t: (d, b, tok(d, b, t), 0)),
        out_shape=jax.ShapeDtypeStruct((2, bn, ln, GLA_WIDTH), BF16),
        scratch_shapes=[
            pltpu.VMEM((GLA_HEADS, GLA_DK, GLA_DK), F32),
            pltpu.SMEM((2,), jnp.int32),
            pltpu.VMEM((2, tile, GLA_WIDTH), BF16),
            pltpu.VMEM((tile, 2 * GLA_WIDTH), BF16),
            pltpu.VMEM((tile, GLA_WIDTH), BF16),
            pltpu.VMEM((tile // GLA_CHUNK, GLA_HEADS, GLA_DK, GLA_DK), F32),
            pltpu.VMEM((tile // GLA_CHUNK, 1, GLA_WIDTH), F32),
            pltpu.VMEM((2, GLA_CHUNK + TOT_ROWS, GLA_WIDTH), F32),
            pltpu.VMEM((GLA_CHUNK, GLA_DK), F32),
            pltpu.VMEM((GLA_CHUNK, GLA_DK), F32),
        ],
        compiler_params=_params("arbitrary", "arbitrary", "arbitrary"),
        name="gla_scan",
    )(q, k, v, gd, gd, wup, wup, tri, mask)


def _gla_constants():
    i = np.arange(GLA_CHUNK)[:, None]
    j = np.arange(GLA_CHUNK)[None, :]
    lower = (j <= i).astype(np.float32)
    upper = (j >= i).astype(np.float32)
    tri = np.zeros((2, GLA_CHUNK + TOT_ROWS, GLA_CHUNK), np.float32)
    tri[0, :GLA_CHUNK] = lower
    tri[1, :GLA_CHUNK] = upper
    tri[:, GLA_CHUNK:] = 1.0
    mask = np.stack([(j <= i), (j > i)]).astype(np.float32)
    return jnp.asarray(tri, BF16), jnp.asarray(mask, F32)


def _rel_tables():
    i = np.arange(BLOCK)[:, None]
    j = np.arange(3 * BLOCK)[None, :]
    rel = j - BLOCK - i
    half = REL_BUCKETS // 2
    max_exact = half // 2
    n = np.abs(rel)
    large = max_exact + (np.log(np.maximum(n, 1) / max_exact) / np.log(REL_MAX_DIST / max_exact)
                         * (half - max_exact)).astype(np.int32)
    large = np.minimum(large, half - 1)
    bucket = (rel > 0).astype(np.int32) * half + np.where(n < max_exact, n, large)
    band = np.abs(rel) <= WINDOW
    col = np.broadcast_to(j, rel.shape)
    valid = np.stack([band & (col >= BLOCK), band, band & (col < 2 * BLOCK)])
    return np.ascontiguousarray(bucket.T).astype(np.int32), np.ascontiguousarray(valid.transpose(0, 2, 1)).astype(np.int32)


def _bias_kernel(rb_ref, bucket_ref, valid_ref, out_ref):
    h = pl.program_id(0)
    bucket = bucket_ref[...]
    acc = jnp.zeros(bucket.shape, F32)
    for kk in range(REL_BUCKETS):
        acc = jnp.where(bucket == kk, rb_ref[kk, h] * LOG2E, acc)
    for kind in range(3):
        out_ref[kind, 0] = jnp.where(valid_ref[kind] > 0, acc, NEG_BIG)


def _bias_table(rel_bias):
    bucket, valid = _rel_tables()
    return pl.pallas_call(
        _bias_kernel,
        grid=(SWA_HEADS,),
        in_specs=[
            pl.BlockSpec(memory_space=pltpu.SMEM),
            pl.BlockSpec((3 * BLOCK, BLOCK), lambda h: (0, 0)),
            pl.BlockSpec((3, 3 * BLOCK, BLOCK), lambda h: (0, 0, 0)),
        ],
        out_specs=pl.BlockSpec((3, 1, 3 * BLOCK, BLOCK), lambda h: (0, h, 0, 0)),
        out_shape=jax.ShapeDtypeStruct((3, SWA_HEADS, 3 * BLOCK, BLOCK), F32),
        compiler_params=_params("arbitrary"),
        name="swa_bias_table",
    )(rel_bias.astype(F32), jnp.asarray(bucket), jnp.asarray(valid))


def _swa_kernel(sink_ref, q_ref, z_ref, kp_ref, kc_ref, kn_ref, vp_ref, vc_ref, vn_ref, bias_ref, o_ref,
                km_ref, va_ref, st_ref, pt_ref, ot_ref, *, nq):
    n = pl.program_id(1)
    nsteps = pl.num_programs(1)
    combos = [(g, e) for g in range(SWA_KV_HEADS) for e in range(2)]
    ncomb = len(combos)
    kcat = jnp.concatenate([kp_ref[0], kc_ref[0], kn_ref[0]], axis=0)
    kswap = jnp.concatenate([kcat[:, SWA_HD:], kcat[:, :SWA_HD]], axis=1)
    low = lax.broadcasted_iota(jnp.int32, kcat.shape, 1) < SWA_HD
    zero = jnp.zeros_like(kcat)
    km_ref[0] = jnp.where(low, kcat, zero)
    km_ref[1] = jnp.where(low, zero, kswap)
    km_ref[2] = jnp.where(low, kswap, zero)
    km_ref[3] = jnp.where(low, zero, kcat)
    vt = jnp.concatenate([vp_ref[0], vc_ref[0], vn_ref[0]], axis=0).astype(F32).T.astype(BF16)
    for g in range(SWA_KV_HEADS):
        va_ref[g, :SWA_HD, :] = vt[g * SWA_HD:(g + 1) * SWA_HD]
        va_ref[g, SWA_HD:, :] = jnp.ones((SWA_ONES_ROWS, vt.shape[1]), BF16)
    half = lax.broadcasted_iota(jnp.int32, (1, 2 * BLOCK), 1) < BLOCK
    pairs_per_kv = SWA_HEADS // SWA_KV_HEADS // 2
    pairs = [[slice((pairs_per_kv * g + i) * 128, (pairs_per_kv * g + i + 1) * 128) for i in range(pairs_per_kv)]
             for g in range(SWA_KV_HEADS)]
    for qb in range(nq):
        qrows = slice(qb * BLOCK, (qb + 1) * BLOCK)
        keys = slice(qb * BLOCK, (qb + 3) * BLOCK)
        kind = 1
        if qb == 0:
            kind = jnp.where(n == 0, 0, kind)
        if qb == nq - 1:
            kind = jnp.where(n == nsteps - 1, 2, kind)
        slot = (qb % 2) * ncomb
        for c, (g, e) in enumerate(combos):
            h0 = 2 * pairs_per_kv * g + e
            qg = jnp.concatenate([q_ref[0, qrows, ps] for ps in pairs[g]], axis=0)
            st_ref[slot + c] = (_dot_nt(km_ref[c, keys, :], qg)
                                + jnp.concatenate([bias_ref[kind, h0], bias_ref[kind, h0 + 2]], axis=1))
        stats = []
        for c, (g, e) in enumerate(combos):
            h0 = 2 * pairs_per_kv * g + e
            sink = jnp.where(half, sink_ref[0, h0], sink_ref[0, h0 + 2]) * LOG2E
            m = jnp.maximum(jnp.max(st_ref[slot + c], axis=0, keepdims=True), sink)
            pt_ref[slot + c] = jnp.exp2(st_ref[slot + c] - m).astype(BF16)
            stats.append(jnp.exp2(sink - m))
        for c, (g, e) in enumerate(combos):
            ot = _dot(va_ref[g, :, keys], pt_ref[slot + c])
            ot_ref[slot + c] = ot[:SWA_HD] * (1.0 / (ot[SWA_HD:SWA_HD + 1] + stats[c]))
        for g in range(SWA_KV_HEADS):
            for i, ps in enumerate(pairs[g]):
                cs = slice(i * BLOCK, (i + 1) * BLOCK)
                o = jnp.concatenate([ot_ref[slot + 2 * g, :, cs], ot_ref[slot + 2 * g + 1, :, cs]],
                                    axis=0).T
                o_ref[0, qrows, ps] = (o * _silu(z_ref[0, qrows, ps].astype(F32))).astype(BF16)


def _swa(q, z, k, v, bias, sink, nq):
    bn, ln, _ = q.shape
    nb = ln // BLOCK
    assert nb % nq == 0 and nb >= 2
    prev = pl.BlockSpec((1, BLOCK, SWA_KVW), lambda b, n: (b, jnp.maximum(n * nq - 1, 0), 0))
    own = pl.BlockSpec((1, nq * BLOCK, SWA_KVW), lambda b, n: (b, n, 0))
    nxt = pl.BlockSpec((1, BLOCK, SWA_KVW), lambda b, n: (b, jnp.minimum((n + 1) * nq, nb - 1), 0))
    return pl.pallas_call(
        functools.partial(_swa_kernel, nq=nq),
        grid=(bn, nb // nq),
        in_specs=[
            pl.BlockSpec(memory_space=pltpu.SMEM),
            pl.BlockSpec((1, nq * BLOCK, SWA_WIDTH), lambda b, n: (b, n, 0)),
            pl.BlockSpec((1, nq * BLOCK, SWA_WIDTH), lambda b, n: (b, n, 0)),
            prev, own, nxt,
            prev, own, nxt,
            pl.BlockSpec((3, SWA_HEADS, 3 * BLOCK, BLOCK), lambda b, n: (0, 0, 0, 0)),
        ],
        out_specs=pl.BlockSpec((1, nq * BLOCK, SWA_WIDTH), lambda b, n: (b, n, 0)),
        out_shape=jax.ShapeDtypeStruct((bn, ln, SWA_WIDTH), BF16),
        scratch_shapes=[
            pltpu.VMEM((2 * SWA_KV_HEADS, (nq + 2) * BLOCK, SWA_KVW), BF16),
            pltpu.VMEM((SWA_KV_HEADS, SWA_HD + SWA_ONES_ROWS, (nq + 2) * BLOCK), BF16),
            pltpu.VMEM((4 * SWA_KV_HEADS, 3 * BLOCK, 2 * BLOCK), F32),
            pltpu.VMEM((4 * SWA_KV_HEADS, 3 * BLOCK, 2 * BLOCK), BF16),
            pltpu.VMEM((4 * SWA_KV_HEADS, SWA_HD, 2 * BLOCK), F32),
        ],
        compiler_params=_params("parallel", "arbitrary"),
        name="swa_attention",
    )(sink.reshape(1, SWA_HEADS).astype(F32), q, z, k, k, k, v, v, v, bias)


ODD_CHUNK = 256
HALO = 8
LN_ROWS = 256


def _tail_kernel(of_ref, ob_ref, z_ref, yb_ref, x_ref, wmix_ref, ng_ref, lg0_ref, lb0_ref,
                 win_ref, cw_ref, wout_ref, lg1_ref, lb1_ref, out_ref,
                 x1_ref, halo_ref, xcat_ref, u_ref, th_ref, mixed_ref, *, tm, nt):
    s = pl.program_id(0)
    last_step = pl.num_programs(0) - 1

    @pl.when(s == 0)
    def _():
        halo_ref[...] = jnp.zeros_like(halo_ref)

    @pl.when(s < last_step)
    def _even_tail():
        slot = lax.rem(s, 2)
        for r0 in range(0, tm, LN_ROWS):
            rs = slice(r0, r0 + LN_ROWS)
            o = of_ref[0, 0, rs, :].astype(F32) + ob_ref[0, 0, rs, :].astype(F32)
            parts = []
            for h in range(GLA_HEADS):
                oh = o[:, h * GLA_DK:(h + 1) * GLA_DK]
                parts.append(oh * lax.rsqrt(jnp.mean(oh * oh, axis=-1, keepdims=True) + NORM_EPS))
            on = jnp.concatenate(parts, axis=1) * ng_ref[...]
            ya = (on * z_ref[0, rs, :].astype(F32)).astype(BF16)
            sub = _dot(ya, wmix_ref[:GLA_WIDTH, :]) + _dot(yb_ref[0, rs, :], wmix_ref[GLA_WIDTH:, :])
            x1_ref[slot, rs, :] = _deepnorm(x_ref[0, rs, :], sub, lg0_ref[...], lb0_ref[...])

    @pl.when(s > 0)
    def _odd_layer():
        t = lax.rem(s - 1, nt)
        cur = x1_ref.at[lax.rem(s - 1, 2)]
        nxt = x1_ref.at[lax.rem(s, 2)]
        _odd_body(t, nt, halo_ref, cur, nxt, win_ref, cw_ref, wout_ref, lg1_ref, lb1_ref, out_ref,
                  xcat_ref, u_ref, th_ref, mixed_ref, tm)
        halo_ref[...] = cur[tm - HALO:tm, :]


def _odd_body(t, nt, prev_ref, x_ref, next_ref, win_ref, cw_ref, wout_ref, lg_ref, lb_ref, out_ref,
              xcat_ref, u_ref, th_ref, mixed_ref, tm):
    main = slice(HALO, HALO + tm)
    xcat_ref[...] = jnp.concatenate([prev_ref[...], x_ref[...], next_ref[0:HALO, :]], axis=0).astype(BF16)
    nj = CONV_WIDTH // ODD_CHUNK

    def in_proj(j):
        for i in range(4):
            u_ref[j % 2, :, i * ODD_CHUNK:(i + 1) * ODD_CHUNK] = _dot(
                xcat_ref[...], win_ref[:, i * CONV_WIDTH + j * ODD_CHUNK:i * CONV_WIDTH + (j + 1) * ODD_CHUNK])

    in_proj(0)
    for j in range(nj):
        cols = slice(j * ODD_CHUNK, (j + 1) * ODD_CHUNK)
        ub = u_ref.at[j % 2]
        if j + 1 < nj:
            in_proj(j + 1)
        th_ref[...] = ub[:, ODD_CHUNK:2 * ODD_CHUNK] * ub[:, 2 * ODD_CHUNK:3 * ODD_CHUNK]
        first = pl.ds(HALO - 1, 1)
        last = pl.ds(HALO + tm, 1)
        th_ref[first, :] = jnp.where(t == 0, 0.0, th_ref[first, :])
        th_ref[last, :] = jnp.where(t == nt - 1, 0.0, th_ref[last, :])
        conv = (cw_ref[0:1, cols] * th_ref[HALO - 1:HALO - 1 + tm, :]
                + cw_ref[1:2, cols] * th_ref[main, :]
                + cw_ref[2:3, cols] * th_ref[HALO + 1:HALO + 1 + tm, :])
        mixed_ref[:, cols] = (_silu(ub[main, 3 * ODD_CHUNK:]) * ub[main, :ODD_CHUNK] * conv).astype(BF16)
    for r0 in range(0, tm, LN_ROWS):
        rs = slice(r0, r0 + LN_ROWS)
        acc = _dot(mixed_ref[rs, :], wout_ref[...])
        out_ref[0, rs, :] = _deepnorm(x_ref[rs, :], acc, lg_ref[...], lb_ref[...])


def _tail(o, za, yb, x, wmix, ng, lg0, lb0, win, cw, wout, lg1, lb1, tm):
    bn, ln, _ = x.shape
    nt = ln // tm
    ntiles = bn * nt
    rows = tm + 2 * HALO

    def tile_in(s):
        g = jnp.minimum(s, ntiles - 1)
        return g // nt, g % nt

    def tile_out(s):
        g = jnp.maximum(s - 1, 0)
        return g // nt, g % nt

    whole = lambda *shape: pl.BlockSpec(shape, lambda s: (0,) * len(shape), pipeline_mode=pl.Buffered(1))
    return pl.pallas_call(
        functools.partial(_tail_kernel, tm=tm, nt=nt),
        grid=(ntiles + 1,),
        in_specs=[
            pl.BlockSpec((1, 1, tm, GLA_WIDTH), lambda s: (0, *tile_in(s), 0)),
            pl.BlockSpec((1, 1, tm, GLA_WIDTH), lambda s: (1, *tile_in(s), 0)),
            pl.BlockSpec((1, tm, GLA_WIDTH), lambda s: (*tile_in(s), 0)),
            pl.BlockSpec((1, tm, SWA_WIDTH), lambda s: (*tile_in(s), 0)),
            pl.BlockSpec((1, tm, D_MODEL), lambda s: (*tile_in(s), 0)),
            whole(GLA_WIDTH + SWA_WIDTH, D_MODEL),
            whole(1, GLA_WIDTH), whole(1, D_MODEL), whole(1, D_MODEL),
            whole(D_MODEL, 4 * CONV_WIDTH), whole(3, CONV_WIDTH), whole(CONV_WIDTH, D_MODEL),
            whole(1, D_MODEL), whole(1, D_MODEL),
        ],
        out_specs=pl.BlockSpec((1, tm, D_MODEL), lambda s: (*tile_out(s), 0)),
        out_shape=jax.ShapeDtypeStruct((bn, ln, D_MODEL), F32),
        scratch_shapes=[
            pltpu.VMEM((2, tm, D_MODEL), F32),
            pltpu.VMEM((HALO, D_MODEL), F32),
            pltpu.VMEM((rows, D_MODEL), BF16),
            pltpu.VMEM((2, rows, 4 * ODD_CHUNK), F32),
            pltpu.VMEM((rows, ODD_CHUNK), F32),
            pltpu.VMEM((tm, CONV_WIDTH), BF16),
        ],
        compiler_params=_params("arbitrary"),
        name="even_tail_odd_layer",
    )(o, o, za, yb, x, wmix, ng, lg0, lb0, win, cw, wout, lg1, lb1)


PREP_ROWS = 256


def _prep_even_kernel(w_ref, o_ref):
    g0 = GLA_COLS
    qb0 = g0 + GD_COLS
    kb0 = qb0 + SWA_WIDTH
    vb0 = kb0 + SWA_KVW
    zb0 = vb0 + SWA_KVW
    o_ref[:, :GLA_WIDTH] = (w_ref[:, :GLA_WIDTH] * (GLA_DK ** -0.5)).astype(BF16)
    o_ref[:, GLA_WIDTH:GLA_COLS] = w_ref[:, GLA_WIDTH:GLA_COLS].astype(BF16)
    c = GLA_COLS
    o_ref[:, c:c + SWA_WIDTH] = (w_ref[:, qb0:qb0 + SWA_WIDTH] * (SWA_HD ** -0.5 * LOG2E)).astype(BF16)
    c += SWA_WIDTH
    o_ref[:, c:c + SWA_WIDTH] = w_ref[:, zb0:zb0 + SWA_WIDTH].astype(BF16)
    c += SWA_WIDTH
    o_ref[:, c:c + SWA_KVW] = w_ref[:, kb0:kb0 + SWA_KVW].astype(BF16)
    c += SWA_KVW
    o_ref[:, c:c + SWA_KVW] = w_ref[:, vb0:vb0 + SWA_KVW].astype(BF16)
    c += SWA_KVW
    lane = lax.broadcasted_iota(jnp.int32, (PREP_ROWS, GD_PAD), 1)
    o_ref[:, c:] = jnp.where(lane < GD_COLS, w_ref[:, g0:g0 + GD_PAD], 0.0).astype(BF16)


def _prep_even(w_in_layers, w_up_f, b_f, w_up_b, b_b, norm_g, w_out_layers):
    even_in = w_in_layers.shape[2]
    w = pl.pallas_call(
        _prep_even_kernel,
        grid=(D_MODEL // PREP_ROWS,),
        in_specs=[pl.BlockSpec((None, PREP_ROWS, even_in), lambda r: (0, r, 0))],
        out_specs=pl.BlockSpec((PREP_ROWS, EVEN_COLS), lambda r: (r, 0)),
        out_shape=jax.ShapeDtypeStruct((D_MODEL, EVEN_COLS), BF16),
        compiler_params=_params("parallel"),
        name="prep_even_weights",
    )(w_in_layers)
    zr = jnp.zeros_like(w_up_f)
    zpad = jnp.zeros((GD_PAD - GD_COLS - 1, GLA_WIDTH), w_up_f.dtype)
    wup = (jnp.stack([jnp.concatenate([w_up_f, zr, b_f[None], zpad], axis=0),
                      jnp.concatenate([zr, w_up_b, b_b[None], zpad], axis=0)]) * LOG2E).astype(BF16)
    ng = jnp.tile(norm_g.astype(F32), GLA_HEADS).reshape(1, GLA_WIDTH)
    return w, wup, ng, _layer0_bf16(w_out_layers, 1.0 / DN_ALPHA)


def _cast_kernel(w_ref, o_ref, *, scale):
    o_ref[...] = (w_ref[...] * scale).astype(BF16)


def _layer0_bf16(w_layers, scale=1.0):
    _, rows, cols = w_layers.shape
    return pl.pallas_call(
        functools.partial(_cast_kernel, scale=scale),
        grid=(rows // PREP_ROWS,),
        in_specs=[pl.BlockSpec((None, PREP_ROWS, cols), lambda r: (0, r, 0))],
        out_specs=pl.BlockSpec((PREP_ROWS, cols), lambda r: (r, 0)),
        out_shape=jax.ShapeDtypeStruct((rows, cols), BF16),
        compiler_params=_params("parallel"),
        name="cast_weights",
    )(w_layers)


def _prep_odd(w_in_layers, conv_w, w_out_layers):
    return _layer0_bf16(w_in_layers), conv_w.astype(F32), _layer0_bf16(w_out_layers, 1.0 / DN_ALPHA)


TILE_INPROJ = 1024
TILE_GLA = 2048
TILE_SWA_BLOCKS = 16
TILE_TAIL = 1024


def _trunk(x, even, odd, bias_tab, sink, ln_g, ln_b, consts,
           tm=TILE_TAIL, tm_in=TILE_INPROJ, gla_tile=TILE_GLA, swa_nq=TILE_SWA_BLOCKS):
    w, wup, ng, wmix = even
    win, cw, wout = odd
    tri, mask = consts
    qa, ka, va, za, qb, zb, kb, vb, gd = _inproj_even(x, w, tm_in)
    o = _gla(qa, ka, va, gd, wup, tri, mask, gla_tile)
    yb = _swa(qb, zb, kb, vb, bias_tab, sink, swa_nq)
    lg = ln_g.astype(F32).reshape(DEPTH, 1, D_MODEL)
    lb = ln_b.astype(F32).reshape(DEPTH, 1, D_MODEL)
    return _tail(o, za, yb, x, wmix, ng, lg[0], lb[0], win, cw, wout, lg[1], lb[1], tm)


def kernel(x_prompt, x_sample, w_in_even, gla_w_up_fwd, gla_b_fwd, gla_w_up_bwd, gla_b_bwd, gla_norm_g, swa_sink,
           rel_bias, w_out_even, w_in_odd, conv_w, w_out_odd, ln_g, ln_b):
    even = _prep_even(w_in_even, gla_w_up_fwd[0], gla_b_fwd[0], gla_w_up_bwd[0], gla_b_bwd[0], gla_norm_g[0],
                      w_out_even)
    odd = _prep_odd(w_in_odd, conv_w[0], w_out_odd)
    bias_tab = _bias_table(rel_bias)
    consts = _gla_constants()
    run = lambda x: _trunk(x, even, odd, bias_tab, swa_sink[0], ln_g, ln_b, consts)
    return (run(x_prompt), run(x_sample))
```

```python
import functools
import math

import numpy as np
import jax
import jax.numpy as jnp
from jax import lax
from jax.experimental import pallas as pl
from jax.experimental.pallas import tpu as pltpu

F32 = jnp.float32
BF16 = jnp.bfloat16

D_MODEL = 1024
DEPTH = 2
GLA_HEADS = 4
GLA_DK = 128
GLA_WIDTH = 512
GLA_RANK = 16
GLA_TAU = 16.0
SWA_HEADS = 8
SWA_KV_HEADS = 2
SWA_HD = 64
SWA_WIDTH = 512
SWA_KVW = 128
WINDOW = 128
BLOCK = 128
REL_BUCKETS = 32
REL_MAX_DIST = 128
CONV_WIDTH = 1024
DN_ALPHA = (2 * DEPTH) ** 0.25
LN_EPS = 1e-5
NORM_EPS = 1e-6
NEG_BIG = -1e30
LOG2E = math.log2(math.e)
LN2 = math.log(2.0)
SWA_ONES_ROWS = 16
SWA_GROUP = 2

GLA_COLS = 4 * GLA_WIDTH
SWA_COLS = 2 * SWA_WIDTH + 2 * SWA_KVW
GD_COLS = 2 * GLA_RANK
GD_PAD = 128
EVEN_COLS = GLA_COLS + SWA_COLS + GD_PAD

GLA_CHUNK = 128
TOT_ROWS = 16
GLA_SAFE_LOGIT = -8.0
VMEM_LIMIT = 56 * 1024 * 1024


def _dot(a, b):
    return jnp.dot(a, b, preferred_element_type=F32)


def _dot_nt(a, b):
    return lax.dot_general(a, b, (((1,), (1,)), ((), ())), preferred_element_type=F32)


def _dot_tn(a, b):
    return lax.dot_general(a, b, (((0,), (0,)), ((), ())), preferred_element_type=F32)


def _silu(z):
    return z / (1.0 + jnp.exp(-z))


def _deepnorm(x, sub_over_alpha, g, b):
    y = x + sub_over_alpha
    mu = jnp.mean(y, axis=-1, keepdims=True)
    yc = y - mu
    var = jnp.mean(yc * yc, axis=-1, keepdims=True)
    return yc * lax.rsqrt(var + LN_EPS / (DN_ALPHA * DN_ALPHA)) * g + b


def _params(*sem):
    return pltpu.CompilerParams(dimension_semantics=sem, vmem_limit_bytes=VMEM_LIMIT)


EVEN_OUT_WIDTHS = (GLA_WIDTH,) * 4 + (SWA_WIDTH, SWA_WIDTH, SWA_KVW, SWA_KVW, GD_PAD)
EVEN_GATE_OUTPUTS = (3,)


def _inproj_even_kernel(x_ref, w_ref, *out_refs):
    xb = x_ref[0].astype(BF16)
    c0 = 0
    for i, (ref, width) in enumerate(zip(out_refs[:-3], EVEN_OUT_WIDTHS[:-3])):
        u = _dot(xb, w_ref[:, c0:c0 + width])
        ref[0] = (_silu(u) if i in EVEN_GATE_OUTPUTS else u).astype(BF16)
        c0 += width
    kvg = _dot(xb, w_ref[:, c0:])
    out_refs[-3][0] = kvg[:, :SWA_KVW].astype(BF16)
    out_refs[-2][0] = kvg[:, SWA_KVW:2 * SWA_KVW].astype(BF16)
    one_lane = (lax.broadcasted_iota(jnp.int32, (1, GD_PAD), 1) == GD_COLS).astype(F32)
    out_refs[-1][0] = (kvg[:, 2 * SWA_KVW:] + one_lane).astype(BF16)


def _inproj_even(x, w, tm):
    bn, ln, _ = x.shape
    return pl.pallas_call(
        _inproj_even_kernel,
        grid=(bn, ln // tm),
        in_specs=[
            pl.BlockSpec((1, tm, D_MODEL), lambda b, t: (b, t, 0)),
            pl.BlockSpec((D_MODEL, EVEN_COLS), lambda b, t: (0, 0), pipeline_mode=pl.Buffered(1)),
        ],
        out_specs=[pl.BlockSpec((1, tm, width), lambda b, t: (b, t, 0)) for width in EVEN_OUT_WIDTHS],
        out_shape=[jax.ShapeDtypeStruct((bn, ln, width), BF16) for width in EVEN_OUT_WIDTHS],
        compiler_params=_params("parallel", "parallel"),
        name="inproj_even",
    )(x, w)


def _gla_kernel(q_ref, k_ref, v_ref, gd_ref, gdn_ref, wup_ref, wupn_ref, tri_ref, mask_ref, o_ref,
                st_ref, flag_ref, logd_ref, sq_ref, kt_ref, u_ref, et_ref, cum_ref, qf_ref, kf_ref,
                *, tile):
    d = pl.program_id(0)
    b = pl.program_id(1)
    t = pl.program_id(2)
    step = (d * pl.num_programs(1) + b) * pl.num_programs(2) + t
    slot = lax.rem(step, 2)

    def gate(gd_blk, w_blk, sl):
        a2 = _dot(gd_blk, w_blk)
        logd_ref[sl] = ((jnp.minimum(a2, 0.0) - jnp.log2(1.0 + jnp.exp2(-jnp.abs(a2)))) * (LN2 / GLA_TAU)).astype(BF16)
        flag_ref[sl] = (jnp.min(a2) < GLA_SAFE_LOGIT * LOG2E).astype(jnp.int32)

    @pl.when(step == 0)
    def _():
        gate(gd_ref[0], wup_ref[0], slot)

    @pl.when(t == 0)
    def _():
        st_ref[...] = jnp.zeros_like(st_ref)

    unsafe = flag_ref[slot] != 0

    @pl.when(jnp.logical_not(unsafe))
    def _():
        _gla_fast_tile(d, q_ref, k_ref, v_ref, tri_ref, mask_ref, o_ref, st_ref, logd_ref.at[slot], sq_ref,
                       kt_ref, u_ref, et_ref, cum_ref, tile)
        gate(gdn_ref[0], wupn_ref[0], 1 - slot)

    @pl.when(unsafe)
    def _():
        _gla_pairwise_tile(d, q_ref, k_ref, v_ref, tri_ref, o_ref, st_ref, logd_ref.at[slot], cum_ref,
                           qf_ref, kf_ref, tile)
        gate(gdn_ref[0], wupn_ref[0], 1 - slot)


def _gla_pairwise_tile(d, q_ref, k_ref, v_ref, tri_ref, o_ref, st_ref, logd_ref, cum_ref, qf_ref, kf_ref, tile):
    nchunk = tile // GLA_CHUNK
    tri = tri_ref[0]
    jrow = lax.broadcasted_iota(jnp.int32, (GLA_CHUNK, GLA_CHUNK), 0)
    icol = lax.broadcasted_iota(jnp.int32, (GLA_CHUNK, GLA_CHUNK), 1)

    def chunk(c, carry):
        cc = c + d * (nchunk - 1 - 2 * c)
        rows = pl.ds(pl.multiple_of(cc * GLA_CHUNK, GLA_CHUNK), GLA_CHUNK)
        cum_ref[0] = _dot(tri, logd_ref[rows, :])
        for h in range(GLA_HEADS):
            hs = slice(h * GLA_DK, (h + 1) * GLA_DK)
            b = cum_ref[0, :GLA_CHUNK, hs]
            tot = cum_ref[0, GLA_CHUNK:GLA_CHUNK + 1, hs]
            qf_ref[...] = q_ref[0, rows, hs].astype(F32)
            kf_ref[...] = k_ref[0, rows, hs].astype(F32)
            v = v_ref[0, rows, hs]

            def pair_rows(g, st_t):
                base = pl.multiple_of(g * 8, 8)
                b8 = cum_ref[0, pl.ds(base, 8), hs]
                q8 = qf_ref[pl.ds(base, 8), :]
                for r in range(8):
                    i = base + r
                    w = jnp.exp(jnp.minimum(b8[r:r + 1] - cum_ref[0, :GLA_CHUNK, hs], 0.0))
                    col = jnp.sum(q8[r:r + 1] * kf_ref[...] * w, axis=1, keepdims=True)
                    valid = (1 - 2 * d) * (jrow - i) <= -d
                    st_t = jnp.where((icol == i) & valid, col, st_t)
                return st_t

            s = lax.fori_loop(0, GLA_CHUNK // 8, pair_rows, jnp.zeros((GLA_CHUNK, GLA_CHUNK), F32)).T.astype(BF16)
            qt = (qf_ref[...] * jnp.exp(b)).astype(BF16)
            kd = (kf_ref[...] * jnp.exp(tot - b)).astype(BF16)
            st = st_ref[h]
            vs = jnp.concatenate([v, st.T.astype(BF16)], axis=0)
            o_ref[0, 0, rows, hs] = _dot(jnp.concatenate([s, qt], axis=1), vs).astype(BF16)
            st_ref[h] = st * jnp.exp(tot) + _dot_tn(v, kd)
        return carry

    lax.fori_loop(0, nchunk, chunk, 0)


def _gla_fast_tile(d, q_ref, k_ref, v_ref, tri_ref, mask_ref, o_ref, st_ref, logd_ref, sq_ref,
                   kt_ref, u_ref, et_ref, cum_ref, tile):
    nchunk = tile // GLA_CHUNK
    tri = tri_ref[0]
    keep = mask_ref[0] > 0.0
    heads = [slice(h * GLA_DK, (h + 1) * GLA_DK) for h in range(GLA_HEADS)]
    s_cols = [slice(2 * h * GLA_DK, (2 * h + 1) * GLA_DK) for h in range(GLA_HEADS)]
    q_cols = [slice((2 * h + 1) * GLA_DK, (2 * h + 2) * GLA_DK) for h in range(GLA_HEADS)]
    sq_cols = [slice(2 * h * GLA_DK, (2 * h + 2) * GLA_DK) for h in range(GLA_HEADS)]

    for c in range(nchunk):
        rows = slice(c * GLA_CHUNK, (c + 1) * GLA_CHUNK)
        cum_ref[c % 2] = _dot(tri, logd_ref[rows, :])
        for h, hs in enumerate(heads):
            b = cum_ref[c % 2, :GLA_CHUNK, hs]
            etot = jnp.exp(cum_ref[c % 2, GLA_CHUNK:GLA_CHUNK + 1, hs])
            sq_ref[rows, q_cols[h]] = (q_ref[0, rows, hs].astype(F32) * jnp.exp(b)).astype(BF16)
            kt_ref[rows, hs] = (k_ref[0, rows, hs].astype(F32) * jnp.exp(-b)).astype(BF16)
            et_ref[c, :, hs] = etot

    for c in range(nchunk):
        rows = slice(c * GLA_CHUNK, (c + 1) * GLA_CHUNK)
        for h, hs in enumerate(heads):
            sq_ref[rows, s_cols[h]] = jnp.where(
                keep, _dot_nt(sq_ref[rows, q_cols[h]], kt_ref[rows, hs]), 0.0).astype(BF16)
            u_ref[c, h] = _dot_tn(v_ref[0, rows, hs], kt_ref[rows, hs])

    for c in range(nchunk):
        cc = c + d * (nchunk - 1 - 2 * c)
        rows = pl.ds(pl.multiple_of(cc * GLA_CHUNK, GLA_CHUNK), GLA_CHUNK)
        for h, hs in enumerate(heads):
            st = st_ref[h]
            vs = jnp.concatenate([v_ref[0, rows, hs], st.T.astype(BF16)], axis=0)
            o_ref[0, 0, rows, hs] = _dot(sq_ref[rows, sq_cols[h]], vs).astype(BF16)
            st_ref[h] = (st + u_ref[cc, h]) * et_ref[cc, :, hs]


def _gla(q, k, v, gd, wup, tri, mask, tile):
    bn, ln, _ = q.shape
    nt = ln // tile

    def tok(d, b, t):
        return t + d * (nt - 1 - 2 * t)

    def nxt(d, b, t):
        roll_t = t == nt - 1
        roll_b = roll_t & (b == bn - 1)
        d2 = jnp.minimum(jnp.where(roll_b, d + 1, d), 1)
        b2 = jnp.where(roll_b, 0, jnp.where(roll_t, b + 1, b))
        return d2, b2, jnp.where(roll_t, 0, t + 1)

    def gd_next(d, b, t):
        d2, b2, t2 = nxt(d, b, t)
        return b2, tok(d2, b2, t2), 0

    return pl.pallas_call(
        functools.partial(_gla_kernel, tile=tile),
        grid=(2, bn, nt),
        in_specs=[
            pl.BlockSpec((1, tile, GLA_WIDTH), lambda d, b, t: (b, tok(d, b, t), 0)),
            pl.BlockSpec((1, tile, GLA_WIDTH), lambda d, b, t: (b, tok(d, b, t), 0)),
            pl.BlockSpec((1, tile, GLA_WIDTH), lambda d, b, t: (b, tok(d, b, t), 0)),
            pl.BlockSpec((1, tile, GD_PAD), lambda d, b, t: (b, tok(d, b, t), 0)),
            pl.BlockSpec((1, tile, GD_PAD), gd_next),
            pl.BlockSpec((1, GD_PAD, GLA_WIDTH), lambda d, b, t: (d, 0, 0)),
            pl.BlockSpec((1, GD_PAD, GLA_WIDTH), lambda d, b, t: (nxt(d, b, t)[0], 0, 0)),
            pl.BlockSpec((1, GLA_CHUNK + TOT_ROWS, GLA_CHUNK), lambda d, b, t: (d, 0, 0)),
            pl.BlockSpec((1, GLA_CHUNK, GLA_CHUNK), lambda d, b, t: (d, 0, 0)),
        ],
        out_specs=pl.BlockSpec((1, 1, tile, GLA_WIDTH), lambda d, b, t: (d, b, tok(d, b, t), 0)),
        out_shape=jax.ShapeDtypeStruct((2, bn, ln, GLA_WIDTH), BF16),
        scratch_shapes=[
            pltpu.VMEM((GLA_HEADS, GLA_DK, GLA_DK), F32),
            pltpu.SMEM((2,), jnp.int32),
            pltpu.VMEM((2, tile, GLA_WIDTH), BF16),
            pltpu.VMEM((tile, 2 * GLA_WIDTH), BF16),
            pltpu.VMEM((tile, GLA_WIDTH), BF16),
            pltpu.VMEM((tile // GLA_CHUNK, GLA_HEADS, GLA_DK, GLA_DK), F32),
            pltpu.VMEM((tile // GLA_CHUNK, 1, GLA_WIDTH), F32),
            pltpu.VMEM((2, GLA_CHUNK + TOT_ROWS, GLA_WIDTH), F32),
            pltpu.VMEM((GLA_CHUNK, GLA_DK), F32),
            pltpu.VMEM((GLA_CHUNK, GLA_DK), F32),
        ],
        compiler_params=_params("arbitrary", "arbitrary", "arbitrary"),
        name="gla_scan",
    )(q, k, v, gd, gd, wup, wup, tri, mask)


def _gla_constants():
    i = np.arange(GLA_CHUNK)[:, None]
    j = np.arange(GLA_CHUNK)[None, :]
    lower = (j <= i).astype(np.float32)
    upper = (j >= i).astype(np.float32)
    tri = np.zeros((2, GLA_CHUNK + TOT_ROWS, GLA_CHUNK), np.float32)
    tri[0, :GLA_CHUNK] = lower
    tri[1, :GLA_CHUNK] = upper
    tri[:, GLA_CHUNK:] = 1.0
    mask = np.stack([(j <= i), (j > i)]).astype(np.float32)
    return jnp.asarray(tri, BF16), jnp.asarray(mask, F32)


def _rel_tables():
    i = np.arange(BLOCK)[:, None]
    j = np.arange(3 * BLOCK)[None, :]
    rel = j - BLOCK - i
    half = REL_BUCKETS // 2
    max_exact = half // 2
    n = np.abs(rel)
    large = max_exact + (np.log(np.maximum(n, 1) / max_exact) / np.log(REL_MAX_DIST / max_exact)
                         * (half - max_exact)).astype(np.int32)
    large = np.minimum(large, half - 1)
    bucket = (rel > 0).astype(np.int32) * half + np.where(n < max_exact, n, large)
    band = np.abs(rel) <= WINDOW
    col = np.broadcast_to(j, rel.shape)
    valid = np.stack([band & (col >= BLOCK), band, band & (col < 2 * BLOCK)])
    return np.ascontiguousarray(bucket.T).astype(np.int32), np.ascontiguousarray(valid.transpose(0, 2, 1)).astype(np.int32)


def _bias_kernel(rb_ref, bucket_ref, valid_ref, out_ref):
    h = pl.program_id(0)
    bucket = bucket_ref[...]
    acc = jnp.zeros(bucket.shape, F32)
    for kk in range(REL_BUCKETS):
        acc = jnp.where(bucket == kk, rb_ref[kk, h] * LOG2E, acc)
    for kind in range(3):
        out_ref[kind, 0] = jnp.where(valid_ref[kind] > 0, acc, NEG_BIG)


def _bias_table(rel_bias):
    bucket, valid = _rel_tables()
    return pl.pallas_call(
        _bias_kernel,
        grid=(SWA_HEADS,),
        in_specs=[
            pl.BlockSpec(memory_space=pltpu.SMEM),
            pl.BlockSpec((3 * BLOCK, BLOCK), lambda h: (0, 0)),
            pl.BlockSpec((3, 3 * BLOCK, BLOCK), lambda h: (0, 0, 0)),
        ],
        out_specs=pl.BlockSpec((3, 1, 3 * BLOCK, BLOCK), lambda h: (0, h, 0, 0)),
        out_shape=jax.ShapeDtypeStruct((3, SWA_HEADS, 3 * BLOCK, BLOCK), F32),
        compiler_params=_params("arbitrary"),
        name="swa_bias_table",
    )(rel_bias.astype(F32), jnp.asarray(bucket), jnp.asarray(valid))


def _swa_kernel(sink_ref, q_ref, z_ref, kp_ref, kc_ref, kn_ref, vp_ref, vc_ref, vn_ref, bias_ref, o_ref,
                km_ref, va_ref, st_ref, pt_ref, ot_ref, *, nq):
    n = pl.program_id(1)
    nsteps = pl.num_programs(1)
    combos = [(g, e) for g in range(SWA_KV_HEADS) for e in range(2)]
    ncomb = len(combos)
    kcat = jnp.concatenate([kp_ref[0], kc_ref[0], kn_ref[0]], axis=0)
    kswap = jnp.concatenate([kcat[:, SWA_HD:], kcat[:, :SWA_HD]], axis=1)
    low = lax.broadcasted_iota(jnp.int32, kcat.shape, 1) < SWA_HD
    zero = jnp.zeros_like(kcat)
    km_ref[0] = jnp.where(low, kcat, zero)
    km_ref[1] = jnp.where(low, zero, kswap)
    km_ref[2] = jnp.where(low, kswap, zero)
    km_ref[3] = jnp.where(low, zero, kcat)
    vt = jnp.concatenate([vp_ref[0], vc_ref[0], vn_ref[0]], axis=0).astype(F32).T.astype(BF16)
    for g in range(SWA_KV_HEADS):
        va_ref[g, :SWA_HD, :] = vt[g * SWA_HD:(g + 1) * SWA_HD]
        va_ref[g, SWA_HD:, :] = jnp.ones((SWA_ONES_ROWS, vt.shape[1]), BF16)
    half = lax.broadcasted_iota(jnp.int32, (1, 2 * BLOCK), 1) < BLOCK
    pairs_per_kv = SWA_HEADS // SWA_KV_HEADS // 2
    pairs = [[slice((pairs_per_kv * g + i) * 128, (pairs_per_kv * g + i + 1) * 128) for i in range(pairs_per_kv)]
             for g in range(SWA_KV_HEADS)]
    for qb0 in range(0, nq, SWA_GROUP):
        group = list(range(qb0, min(qb0 + SWA_GROUP, nq)))
        qrows = {qb: slice(qb * BLOCK, (qb + 1) * BLOCK) for qb in group}
        keys = {qb: slice(qb * BLOCK, (qb + 3) * BLOCK) for qb in group}
        slot = {qb: (qb % SWA_GROUP) * ncomb for qb in group}
        for qb in group:
            kind = 1
            if qb == 0:
                kind = jnp.where(n == 0, 0, kind)
            if qb == nq - 1:
                kind = jnp.where(n == nsteps - 1, 2, kind)
            for c, (g, e) in enumerate(combos):
                h0 = 2 * pairs_per_kv * g + e
                qg = jnp.concatenate([q_ref[0, qrows[qb], ps] for ps in pairs[g]], axis=0)
                st_ref[slot[qb] + c] = (_dot_nt(km_ref[c, keys[qb], :], qg)
                                        + jnp.concatenate([bias_ref[kind, h0], bias_ref[kind, h0 + 2]], axis=1))
        stats = {}
        for qb in group:
            for c, (g, e) in enumerate(combos):
                h0 = 2 * pairs_per_kv * g + e
                sink = jnp.where(half, sink_ref[0, h0], sink_ref[0, h0 + 2]) * LOG2E
                m = jnp.maximum(jnp.max(st_ref[slot[qb] + c], axis=0, keepdims=True), sink)
                pt_ref[slot[qb] + c] = jnp.exp2(st_ref[slot[qb] + c] - m).astype(BF16)
                stats[(qb, c)] = jnp.exp2(sink - m)
        for qb in group:
            for c, (g, e) in enumerate(combos):
                ot = _dot(va_ref[g, :, keys[qb]], pt_ref[slot[qb] + c])
                ot_ref[slot[qb] + c] = ot[:SWA_HD] * (1.0 / (ot[SWA_HD:SWA_HD + 1] + stats[(qb, c)]))
        for qb in group:
            for g in range(SWA_KV_HEADS):
                for i, ps in enumerate(pairs[g]):
                    cs = slice(i * BLOCK, (i + 1) * BLOCK)
                    o = jnp.concatenate([ot_ref[slot[qb] + 2 * g, :, cs], ot_ref[slot[qb] + 2 * g + 1, :, cs]],
                                        axis=0).T
                    o_ref[0, qrows[qb], ps] = (o * _silu(z_ref[0, qrows[qb], ps].astype(F32))).astype(BF16)


def _swa(q, z, k, v, bias, sink, nq):
    bn, ln, _ = q.shape
    nb = ln // BLOCK
    assert nb % nq == 0 and nb >= 2
    prev = pl.BlockSpec((1, BLOCK, SWA_KVW), lambda b, n: (b, jnp.maximum(n * nq - 1, 0), 0))
    own = pl.BlockSpec((1, nq * BLOCK, SWA_KVW), lambda b, n: (b, n, 0))
    nxt = pl.BlockSpec((1, BLOCK, SWA_KVW), lambda b, n: (b, jnp.minimum((n + 1) * nq, nb - 1), 0))
    return pl.pallas_call(
        functools.partial(_swa_kernel, nq=nq),
        grid=(bn, nb // nq),
        in_specs=[
            pl.BlockSpec(memory_space=pltpu.SMEM),
            pl.BlockSpec((1, nq * BLOCK, SWA_WIDTH), lambda b, n: (b, n, 0)),
            pl.BlockSpec((1, nq * BLOCK, SWA_WIDTH), lambda b, n: (b, n, 0)),
            prev, own, nxt,
            prev, own, nxt,
            pl.BlockSpec((3, SWA_HEADS, 3 * BLOCK, BLOCK), lambda b, n: (0, 0, 0, 0)),
        ],
        out_specs=pl.BlockSpec((1, nq * BLOCK, SWA_WIDTH), lambda b, n: (b, n, 0)),
        out_shape=jax.ShapeDtypeStruct((bn, ln, SWA_WIDTH), BF16),
        scratch_shapes=[
            pltpu.VMEM((2 * SWA_KV_HEADS, (nq + 2) * BLOCK, SWA_KVW), BF16),
            pltpu.VMEM((SWA_KV_HEADS, SWA_HD + SWA_ONES_ROWS, (nq + 2) * BLOCK), BF16),
            pltpu.VMEM((SWA_GROUP * 2 * SWA_KV_HEADS, 3 * BLOCK, 2 * BLOCK), F32),
            pltpu.VMEM((SWA_GROUP * 2 * SWA_KV_HEADS, 3 * BLOCK, 2 * BLOCK), BF16),
            pltpu.VMEM((SWA_GROUP * 2 * SWA_KV_HEADS, SWA_HD, 2 * BLOCK), F32),
        ],
        compiler_params=_params("parallel", "arbitrary"),
        name="swa_attention",
    )(sink.reshape(1, SWA_HEADS).astype(F32), q, z, k, k, k, v, v, v, bias)


ODD_CHUNK = 256
HALO = 8
LN_ROWS = 256


def _tail_kernel(of_ref, ob_ref, z_ref, yb_ref, x_ref, wmix_ref, ng_ref, lg0_ref, lb0_ref,
                 win_ref, cw_ref, wout_ref, lg1_ref, lb1_ref, out_ref,
                 x1_ref, halo_ref, xcat_ref, u_ref, th_ref, mixed_ref, *, tm, nt):
    s = pl.program_id(0)
    last_step = pl.num_programs(0) - 1

    @pl.when(s == 0)
    def _():
        halo_ref[...] = jnp.zeros_like(halo_ref)

    @pl.when(s < last_step)
    def _even_tail():
        slot = lax.rem(s, 2)
        for r0 in range(0, tm, LN_ROWS):
            rs = slice(r0, r0 + LN_ROWS)
            o = of_ref[0, 0, rs, :].astype(F32) + ob_ref[0, 0, rs, :].astype(F32)
            parts = []
            for h in range(GLA_HEADS):
                oh = o[:, h * GLA_DK:(h + 1) * GLA_DK]
                parts.append(oh * lax.rsqrt(jnp.mean(oh * oh, axis=-1, keepdims=True) + NORM_EPS))
            on = jnp.concatenate(parts, axis=1) * ng_ref[...]
            ya = (on * z_ref[0, rs, :].astype(F32)).astype(BF16)
            sub = _dot(ya, wmix_ref[:GLA_WIDTH, :]) + _dot(yb_ref[0, rs, :], wmix_ref[GLA_WIDTH:, :])
            x1_ref[slot, rs, :] = _deepnorm(x_ref[0, rs, :], sub, lg0_ref[...], lb0_ref[...])

    @pl.when(s > 0)
    def _odd_layer():
        t = lax.rem(s - 1, nt)
        cur = x1_ref.at[lax.rem(s - 1, 2)]
        nxt = x1_ref.at[lax.rem(s, 2)]
        _odd_body(t, nt, halo_ref, cur, nxt, win_ref, cw_ref, wout_ref, lg1_ref, lb1_ref, out_ref,
                  xcat_ref, u_ref, th_ref, mixed_ref, tm)
        halo_ref[...] = cur[tm - HALO:tm, :]


def _odd_body(t, nt, prev_ref, x_ref, next_ref, win_ref, cw_ref, wout_ref, lg_ref, lb_ref, out_ref,
              xcat_ref, u_ref, th_ref, mixed_ref, tm):
    main = slice(HALO, HALO + tm)
    xcat_ref[...] = jnp.concatenate([prev_ref[...], x_ref[...], next_ref[0:HALO, :]], axis=0).astype(BF16)
    nj = CONV_WIDTH // ODD_CHUNK

    def in_proj(j):
        for i in range(4):
            u_ref[j % 2, :, i * ODD_CHUNK:(i + 1) * ODD_CHUNK] = _dot(
                xcat_ref[...], win_ref[:, i * CONV_WIDTH + j * ODD_CHUNK:i * CONV_WIDTH + (j + 1) * ODD_CHUNK])

    in_proj(0)
    for j in range(nj):
        cols = slice(j * ODD_CHUNK, (j + 1) * ODD_CHUNK)
        ub = u_ref.at[j % 2]
        if j + 1 < nj:
            in_proj(j + 1)
        th_ref[...] = ub[:, ODD_CHUNK:2 * ODD_CHUNK] * ub[:, 2 * ODD_CHUNK:3 * ODD_CHUNK]
        first = pl.ds(HALO - 1, 1)
        last = pl.ds(HALO + tm, 1)
        th_ref[first, :] = jnp.where(t == 0, 0.0, th_ref[first, :])
        th_ref[last, :] = jnp.where(t == nt - 1, 0.0, th_ref[last, :])
        conv = (cw_ref[0:1, cols] * th_ref[HALO - 1:HALO - 1 + tm, :]
                + cw_ref[1:2, cols] * th_ref[main, :]
                + cw_ref[2:3, cols] * th_ref[HALO + 1:HALO + 1 + tm, :])
        mixed_ref[:, cols] = (_silu(ub[main, 3 * ODD_CHUNK:]) * ub[main, :ODD_CHUNK] * conv).astype(BF16)
    for r0 in range(0, tm, LN_ROWS):
        rs = slice(r0, r0 + LN_ROWS)
        acc = _dot(mixed_ref[rs, :], wout_ref[...])
        out_ref[0, rs, :] = _deepnorm(x_ref[rs, :], acc, lg_ref[...], lb_ref[...])


def _tail(o, za, yb, x, wmix, ng, lg0, lb0, win, cw, wout, lg1, lb1, tm):
    bn, ln, _ = x.shape
    nt = ln // tm
    ntiles = bn * nt
    rows = tm + 2 * HALO

    def tile_in(s):
        g = jnp.minimum(s, ntiles - 1)
        return g // nt, g % nt

    def tile_out(s):
        g = jnp.maximum(s - 1, 0)
        return g // nt, g % nt

    whole = lambda *shape: pl.BlockSpec(shape, lambda s: (0,) * len(shape), pipeline_mode=pl.Buffered(1))
    return pl.pallas_call(
        functools.partial(_tail_kernel, tm=tm, nt=nt),
        grid=(ntiles + 1,),
        in_specs=[
            pl.BlockSpec((1, 1, tm, GLA_WIDTH), lambda s: (0, *tile_in(s), 0)),
            pl.BlockSpec((1, 1, tm, GLA_WIDTH), lambda s: (1, *tile_in(s), 0)),
            pl.BlockSpec((1, tm, GLA_WIDTH), lambda s: (*tile_in(s), 0)),
            pl.BlockSpec((1, tm, SWA_WIDTH), lambda s: (*tile_in(s), 0)),
            pl.BlockSpec((1, tm, D_MODEL), lambda s: (*tile_in(s), 0)),
            whole(GLA_WIDTH + SWA_WIDTH, D_MODEL),
            whole(1, GLA_WIDTH), whole(1, D_MODEL), whole(1, D_MODEL),
            whole(D_MODEL, 4 * CONV_WIDTH), whole(3, CONV_WIDTH), whole(CONV_WIDTH, D_MODEL),
            whole(1, D_MODEL), whole(1, D_MODEL),
        ],
        out_specs=pl.BlockSpec((1, tm, D_MODEL), lambda s: (*tile_out(s), 0)),
        out_shape=jax.ShapeDtypeStruct((bn, ln, D_MODEL), F32),
        scratch_shapes=[
            pltpu.VMEM((2, tm, D_MODEL), F32),
            pltpu.VMEM((HALO, D_MODEL), F32),
            pltpu.VMEM((rows, D_MODEL), BF16),
            pltpu.VMEM((2, rows, 4 * ODD_CHUNK), F32),
            pltpu.VMEM((rows, ODD_CHUNK), F32),
            pltpu.VMEM((tm, CONV_WIDTH), BF16),
        ],
        compiler_params=_params("arbitrary"),
        name="even_tail_odd_layer",
    )(o, o, za, yb, x, wmix, ng, lg0, lb0, win, cw, wout, lg1, lb1)


PREP_ROWS = 256


def _prep_even_kernel(w_ref, o_ref):
    g0 = GLA_COLS
    qb0 = g0 + GD_COLS
    kb0 = qb0 + SWA_WIDTH
    vb0 = kb0 + SWA_KVW
    zb0 = vb0 + SWA_KVW
    o_ref[:, :GLA_WIDTH] = (w_ref[:, :GLA_WIDTH] * (GLA_DK ** -0.5)).astype(BF16)
    o_ref[:, GLA_WIDTH:GLA_COLS] = w_ref[:, GLA_WIDTH:GLA_COLS].astype(BF16)
    c = GLA_COLS
    o_ref[:, c:c + SWA_WIDTH] = (w_ref[:, qb0:qb0 + SWA_WIDTH] * (SWA_HD ** -0.5 * LOG2E)).astype(BF16)
    c += SWA_WIDTH
    o_ref[:, c:c + SWA_WIDTH] = w_ref[:, zb0:zb0 + SWA_WIDTH].astype(BF16)
    c += SWA_WIDTH
    o_ref[:, c:c + SWA_KVW] = w_ref[:, kb0:kb0 + SWA_KVW].astype(BF16)
    c += SWA_KVW
    o_ref[:, c:c + SWA_KVW] = w_ref[:, vb0:vb0 + SWA_KVW].astype(BF16)
    c += SWA_KVW
    lane = lax.broadcasted_iota(jnp.int32, (PREP_ROWS, GD_PAD), 1)
    o_ref[:, c:] = jnp.where(lane < GD_COLS, w_ref[:, g0:g0 + GD_PAD], 0.0).astype(BF16)


def _prep_even(w_in_layers, w_up_f, b_f, w_up_b, b_b, norm_g, w_out_layers):
    even_in = w_in_layers.shape[2]
    w = pl.pallas_call(
        _prep_even_kernel,
        grid=(D_MODEL // PREP_ROWS,),
        in_specs=[pl.BlockSpec((None, PREP_ROWS, even_in), lambda r: (0, r, 0))],
        out_specs=pl.BlockSpec((PREP_ROWS, EVEN_COLS), lambda r: (r, 0)),
        out_shape=jax.ShapeDtypeStruct((D_MODEL, EVEN_COLS), BF16),
        compiler_params=_params("parallel"),
        name="prep_even_weights",
    )(w_in_layers)
    zr = jnp.zeros_like(w_up_f)
    zpad = jnp.zeros((GD_PAD - GD_COLS - 1, GLA_WIDTH), w_up_f.dtype)
    wup = (jnp.stack([jnp.concatenate([w_up_f, zr, b_f[None], zpad], axis=0),
                      jnp.concatenate([zr, w_up_b, b_b[None], zpad], axis=0)]) * LOG2E).astype(BF16)
    ng = jnp.tile(norm_g.astype(F32), GLA_HEADS).reshape(1, GLA_WIDTH)
    return w, wup, ng, _layer0_bf16(w_out_layers, 1.0 / DN_ALPHA)


def _cast_kernel(w_ref, o_ref, *, scale):
    o_ref[...] = (w_ref[...] * scale).astype(BF16)


def _layer0_bf16(w_layers, scale=1.0):
    _, rows, cols = w_layers.shape
    return pl.pallas_call(
        functools.partial(_cast_kernel, scale=scale),
        grid=(rows // PREP_ROWS,),
        in_specs=[pl.BlockSpec((None, PREP_ROWS, cols), lambda r: (0, r, 0))],
        out_specs=pl.BlockSpec((PREP_ROWS, cols), lambda r: (r, 0)),
        out_shape=jax.ShapeDtypeStruct((rows, cols), BF16),
        compiler_params=_params("parallel"),
        name="cast_weights",
    )(w_layers)


def _prep_odd(w_in_layers, conv_w, w_out_layers):
    return _layer0_bf16(w_in_layers), conv_w.astype(F32), _layer0_bf16(w_out_layers, 1.0 / DN_ALPHA)


TILE_INPROJ = 1024
TILE_GLA = 2048
TILE_SWA_BLOCKS = 16
TILE_TAIL = 1024


def _trunk(x, even, odd, bias_tab, sink, ln_g, ln_b, consts,
           tm=TILE_TAIL, tm_in=TILE_INPROJ, gla_tile=TILE_GLA, swa_nq=TILE_SWA_BLOCKS):
    w, wup, ng, wmix = even
    win, cw, wout = odd
    tri, mask = consts
    qa, ka, va, za, qb, zb, kb, vb, gd = _inproj_even(x, w, tm_in)
    o = _gla(qa, ka, va, gd, wup, tri, mask, gla_tile)
    yb = _swa(qb, zb, kb, vb, bias_tab, sink, swa_nq)
    lg = ln_g.astype(F32).reshape(DEPTH, 1, D_MODEL)
    lb = ln_b.astype(F32).reshape(DEPTH, 1, D_MODEL)
    return _tail(o, za, yb, x, wmix, ng, lg[0], lb[0], win, cw, wout, lg[1], lb[1], tm)


def kernel(x_prompt, x_sample, w_in_even, gla_w_up_fwd, gla_b_fwd, gla_w_up_bwd, gla_b_bwd, gla_norm_g, swa_sink,
           rel_bias, w_out_even, w_in_odd, conv_w, w_out_odd, ln_g, ln_b):
    even = _prep_even(w_in_even, gla_w_up_fwd[0], gla_b_fwd[0], gla_w_up_bwd[0], gla_b_bwd[0], gla_norm_g[0],
                      w_out_even)
    odd = _prep_odd(w_in_odd, conv_w[0], w_out_odd)
    bias_tab = _bias_table(rel_bias)
    consts = _gla_constants()
    run = lambda x: _trunk(x, even, odd, bias_tab, swa_sink[0], ln_g, ln_b, consts)
    return (run(x_prompt), run(x_sample))
```

```python
import functools
import math

import numpy as np
import jax
import jax.numpy as jnp
from jax import lax
from jax.experimental import pallas as pl
from jax.experimental.pallas import tpu as pltpu

F32 = jnp.float32
BF16 = jnp.bfloat16

D_MODEL = 1024
DEPTH = 2
GLA_HEADS = 4
GLA_DK = 128
GLA_WIDTH = 512
GLA_RANK = 16
GLA_TAU = 16.0
SWA_HEADS = 8
SWA_KV_HEADS = 2
SWA_HD = 64
SWA_WIDTH = 512
SWA_KVW = 128
WINDOW = 128
BLOCK = 128
REL_BUCKETS = 32
REL_MAX_DIST = 128
CONV_WIDTH = 1024
DN_ALPHA = (2 * DEPTH) ** 0.25
LN_EPS = 1e-5
NORM_EPS = 1e-6
NEG_BIG = -1e30
LOG2E = math.log2(math.e)
LN2 = math.log(2.0)
SWA_ONES_ROWS = 16
SWA_GROUP = 2

GLA_COLS = 4 * GLA_WIDTH
SWA_COLS = 2 * SWA_WIDTH + 2 * SWA_KVW
GD_COLS = 2 * GLA_RANK
GD_PAD = 128
EVEN_COLS = GLA_COLS + SWA_COLS + GD_PAD

GLA_CHUNK = 128
TOT_ROWS = 16
GLA_SAFE_LOGIT = -8.0
VMEM_LIMIT = 56 * 1024 * 1024


def _dot(a, b):
    return jnp.dot(a, b, preferred_element_type=F32)


def _dot_nt(a, b):
    return lax.dot_general(a, b, (((1,), (1,)), ((), ())), preferred_element_type=F32)


def _dot_tn(a, b):
    return lax.dot_general(a, b, (((0,), (0,)), ((), ())), preferred_element_type=F32)


def _silu(z):
    return z / (1.0 + jnp.exp(-z))


def _deepnorm(x, sub_over_alpha, g, b):
    y = x + sub_over_alpha
    mu = jnp.mean(y, axis=-1, keepdims=True)
    yc = y - mu
    var = jnp.mean(yc * yc, axis=-1, keepdims=True)
    return yc * lax.rsqrt(var + LN_EPS / (DN_ALPHA * DN_ALPHA)) * g + b


def _params(*sem):
    return pltpu.CompilerParams(dimension_semantics=sem, vmem_limit_bytes=VMEM_LIMIT)


EVEN_OUT_WIDTHS = (GLA_WIDTH,) * 4 + (SWA_WIDTH, SWA_WIDTH, SWA_KVW, SWA_KVW, GD_PAD)
EVEN_GATE_OUTPUTS = (3,)


def _inproj_even_kernel(x_ref, w_ref, *out_refs):
    xb = x_ref[0].astype(BF16)
    c0 = 0
    for i, (ref, width) in enumerate(zip(out_refs[:-3], EVEN_OUT_WIDTHS[:-3])):
        u = _dot(xb, w_ref[:, c0:c0 + width])
        ref[0] = (_silu(u) if i in EVEN_GATE_OUTPUTS else u).astype(BF16)
        c0 += width
    kvg = _dot(xb, w_ref[:, c0:])
    out_refs[-3][0] = kvg[:, :SWA_KVW].astype(BF16)
    out_refs[-2][0] = kvg[:, SWA_KVW:2 * SWA_KVW].astype(BF16)
    one_lane = (lax.broadcasted_iota(jnp.int32, (1, GD_PAD), 1) == GD_COLS).astype(F32)
    out_refs[-1][0] = (kvg[:, 2 * SWA_KVW:] + one_lane).astype(BF16)


def _inproj_even(x, w, tm):
    bn, ln, _ = x.shape
    return pl.pallas_call(
        _inproj_even_kernel,
        grid=(bn, ln // tm),
        in_specs=[
            pl.BlockSpec((1, tm, D_MODEL), lambda b, t: (b, t, 0)),
            pl.BlockSpec((D_MODEL, EVEN_COLS), lambda b, t: (0, 0), pipeline_mode=pl.Buffered(1)),
        ],
        out_specs=[pl.BlockSpec((1, tm, width), lambda b, t: (b, t, 0)) for width in EVEN_OUT_WIDTHS],
        out_shape=[jax.ShapeDtypeStruct((bn, ln, width), BF16) for width in EVEN_OUT_WIDTHS],
        compiler_params=_params("parallel", "parallel"),
        name="inproj_even",
    )(x, w)


def _gla_kernel(q_ref, k_ref, v_ref, gd_ref, gdn_ref, wup_ref, wupn_ref, tri_ref, mask_ref, o_ref,
                st_ref, flag_ref, logd_ref, sq_ref, kt_ref, u_ref, et_ref, cum_ref, qf_ref, kf_ref,
                *, tile):
    d = pl.program_id(0)
    b = pl.program_id(1)
    t = pl.program_id(2)
    step = (d * pl.num_programs(1) + b) * pl.num_programs(2) + t
    slot = lax.rem(step, 2)

    def gate(gd_blk, w_blk, sl):
        a2 = _dot(gd_blk, w_blk)
        logd_ref[sl] = ((jnp.minimum(a2, 0.0) - jnp.log2(1.0 + jnp.exp2(-jnp.abs(a2)))) * (LN2 / GLA_TAU)).astype(BF16)
        flag_ref[sl] = (jnp.min(a2) < GLA_SAFE_LOGIT * LOG2E).astype(jnp.int32)

    @pl.when(step == 0)
    def _():
        gate(gd_ref[0], wup_ref[0], slot)

    @pl.when(t == 0)
    def _():
        st_ref[...] = jnp.zeros_like(st_ref)

    unsafe = flag_ref[slot] != 0

    @pl.when(jnp.logical_not(unsafe))
    def _():
        _gla_fast_tile(d, q_ref, k_ref, v_ref, tri_ref, mask_ref, o_ref, st_ref, logd_ref.at[slot], sq_ref,
                       kt_ref, u_ref, et_ref, cum_ref, tile,
                       next_gate=lambda: gate(gdn_ref[0], wupn_ref[0], 1 - slot))

    @pl.when(unsafe)
    def _():
        _gla_pairwise_tile(d, q_ref, k_ref, v_ref, tri_ref, o_ref, st_ref, logd_ref.at[slot], cum_ref,
                           qf_ref, kf_ref, tile)
        gate(gdn_ref[0], wupn_ref[0], 1 - slot)


def _gla_pairwise_tile(d, q_ref, k_ref, v_ref, tri_ref, o_ref, st_ref, logd_ref, cum_ref, qf_ref, kf_ref, tile):
    nchunk = tile // GLA_CHUNK
    tri = tri_ref[0]
    jrow = lax.broadcasted_iota(jnp.int32, (GLA_CHUNK, GLA_CHUNK), 0)
    icol = lax.broadcasted_iota(jnp.int32, (GLA_CHUNK, GLA_CHUNK), 1)

    def chunk(c, carry):
        cc = c + d * (nchunk - 1 - 2 * c)
        rows = pl.ds(pl.multiple_of(cc * GLA_CHUNK, GLA_CHUNK), GLA_CHUNK)
        cum_ref[0] = _dot(tri, logd_ref[rows, :])
        for h in range(GLA_HEADS):
            hs = slice(h * GLA_DK, (h + 1) * GLA_DK)
            b = cum_ref[0, :GLA_CHUNK, hs]
            tot = cum_ref[0, GLA_CHUNK:GLA_CHUNK + 1, hs]
            qf_ref[...] = q_ref[0, rows, hs].astype(F32)
            kf_ref[...] = k_ref[0, rows, hs].astype(F32)
            v = v_ref[0, rows, hs]

            def pair_rows(g, st_t):
                base = pl.multiple_of(g * 8, 8)
                b8 = cum_ref[0, pl.ds(base, 8), hs]
                q8 = qf_ref[pl.ds(base, 8), :]
                for r in range(8):
                    i = base + r
                    w = jnp.exp(jnp.minimum(b8[r:r + 1] - cum_ref[0, :GLA_CHUNK, hs], 0.0))
                    col = jnp.sum(q8[r:r + 1] * kf_ref[...] * w, axis=1, keepdims=True)
                    valid = (1 - 2 * d) * (jrow - i) <= -d
                    st_t = jnp.where((icol == i) & valid, col, st_t)
                return st_t

            s = lax.fori_loop(0, GLA_CHUNK // 8, pair_rows, jnp.zeros((GLA_CHUNK, GLA_CHUNK), F32)).T.astype(BF16)
            qt = (qf_ref[...] * jnp.exp(b)).astype(BF16)
            kd = (kf_ref[...] * jnp.exp(tot - b)).astype(BF16)
            st = st_ref[h]
            vs = jnp.concatenate([v, st.T.astype(BF16)], axis=0)
            o_ref[0, 0, rows, hs] = _dot(jnp.concatenate([s, qt], axis=1), vs).astype(BF16)
            st_ref[h] = st * jnp.exp(tot) + _dot_tn(v, kd)
        return carry

    lax.fori_loop(0, nchunk, chunk, 0)


def _gla_fast_tile(d, q_ref, k_ref, v_ref, tri_ref, mask_ref, o_ref, st_ref, logd_ref, sq_ref,
                   kt_ref, u_ref, et_ref, cum_ref, tile, next_gate):
    nchunk = tile // GLA_CHUNK
    tri = tri_ref[0]
    keep = mask_ref[0] > 0.0
    heads = [slice(h * GLA_DK, (h + 1) * GLA_DK) for h in range(GLA_HEADS)]
    s_cols = [slice(2 * h * GLA_DK, (2 * h + 1) * GLA_DK) for h in range(GLA_HEADS)]
    q_cols = [slice((2 * h + 1) * GLA_DK, (2 * h + 2) * GLA_DK) for h in range(GLA_HEADS)]
    sq_cols = [slice(2 * h * GLA_DK, (2 * h + 2) * GLA_DK) for h in range(GLA_HEADS)]

    def phase1(c):
        rows = slice(c * GLA_CHUNK, (c + 1) * GLA_CHUNK)
        cum_ref[c % 2] = _dot(tri, logd_ref[rows, :])
        for h, hs in enumerate(heads):
            b = cum_ref[c % 2, :GLA_CHUNK, hs]
            etot = jnp.exp(cum_ref[c % 2, GLA_CHUNK:GLA_CHUNK + 1, hs])
            sq_ref[rows, q_cols[h]] = (q_ref[0, rows, hs].astype(F32) * jnp.exp(b)).astype(BF16)
            kt_ref[rows, hs] = (k_ref[0, rows, hs].astype(F32) * jnp.exp(-b)).astype(BF16)
            et_ref[c, :, hs] = etot

    def phase2(c):
        rows = slice(c * GLA_CHUNK, (c + 1) * GLA_CHUNK)
        for h, hs in enumerate(heads):
            sq_ref[rows, s_cols[h]] = jnp.where(
                keep, _dot_nt(sq_ref[rows, q_cols[h]], kt_ref[rows, hs]), 0.0).astype(BF16)
            u_ref[c, h] = _dot_tn(v_ref[0, rows, hs], kt_ref[rows, hs])

    phase1(0)
    for c in range(nchunk):
        if c + 1 < nchunk:
            phase1(c + 1)
        phase2(c)
    next_gate()

    def phase3(order):
        for c in order:
            rows = slice(c * GLA_CHUNK, (c + 1) * GLA_CHUNK)
            for h, hs in enumerate(heads):
                st = st_ref[h]
                vs = jnp.concatenate([v_ref[0, rows, hs], st.T.astype(BF16)], axis=0)
                o_ref[0, 0, rows, hs] = _dot(sq_ref[rows, sq_cols[h]], vs).astype(BF16)
                st_ref[h] = (st + u_ref[c, h]) * et_ref[c, :, hs]

    @pl.when(d == 0)
    def _():
        phase3(range(nchunk))

    @pl.when(d != 0)
    def _():
        phase3(range(nchunk - 1, -1, -1))


def _gla(q, k, v, gd, wup, tri, mask, tile):
    bn, ln, _ = q.shape
    nt = ln // tile

    def tok(d, b, t):
        return t + d * (nt - 1 - 2 * t)

    def nxt(d, b, t):
        roll_t = t == nt - 1
        roll_b = roll_t & (b == bn - 1)
        d2 = jnp.minimum(jnp.where(roll_b, d + 1, d), 1)
        b2 = jnp.where(roll_b, 0, jnp.where(roll_t, b + 1, b))
        return d2, b2, jnp.where(roll_t, 0, t + 1)

    def gd_next(d, b, t):
        d2, b2, t2 = nxt(d, b, t)
        return b2, tok(d2, b2, t2), 0

    return pl.pallas_call(
        functools.partial(_gla_kernel, tile=tile),
        grid=(2, bn, nt),
        in_specs=[
            pl.BlockSpec((1, tile, GLA_WIDTH), lambda d, b, t: (b, tok(d, b, t), 0)),
            pl.BlockSpec((1, tile, GLA_WIDTH), lambda d, b, t: (b, tok(d, b, t), 0)),
            pl.BlockSpec((1, tile, GLA_WIDTH), lambda d, b, t: (b, tok(d, b, t), 0)),
            pl.BlockSpec((1, tile, GD_PAD), lambda d, b, t: (b, tok(d, b, t), 0)),
            pl.BlockSpec((1, tile, GD_PAD), gd_next),
            pl.BlockSpec((1, GD_PAD, GLA_WIDTH), lambda d, b, t: (d, 0, 0)),
            pl.BlockSpec((1, GD_PAD, GLA_WIDTH), lambda d, b, t: (nxt(d, b, t)[0], 0, 0)),
            pl.BlockSpec((1, GLA_CHUNK + TOT_ROWS, GLA_CHUNK), lambda d, b, t: (d, 0, 0)),
            pl.BlockSpec((1, GLA_CHUNK, GLA_CHUNK), lambda d, b, t: (d, 0, 0)),
        ],
        out_specs=pl.BlockSpec((1, 1, tile, GLA_WIDTH), lambda d, b, t: (d, b, tok(d, b, t), 0)),
        out_shape=jax.ShapeDtypeStruct((2, bn, ln, GLA_WIDTH), BF16),
        scratch_shapes=[
            pltpu.VMEM((GLA_HEADS, GLA_DK, GLA_DK), F32),
            pltpu.SMEM((2,), jnp.int32),
            pltpu.VMEM((2, tile, GLA_WIDTH), BF16),
            pltpu.VMEM((tile, 2 * GLA_WIDTH), BF16),
            pltpu.VMEM((tile, GLA_WIDTH), BF16),
            pltpu.VMEM((tile // GLA_CHUNK, GLA_HEADS, GLA_DK, GLA_DK), F32),
            pltpu.VMEM((tile // GLA_CHUNK, 1, GLA_WIDTH), F32),
            pltpu.VMEM((2, GLA_CHUNK + TOT_ROWS, GLA_WIDTH), F32),
            pltpu.VMEM((GLA_CHUNK, GLA_DK), F32),
            pltpu.VMEM((GLA_CHUNK, GLA_DK), F32),
        ],
        compiler_params=_params("arbitrary", "arbitrary", "arbitrary"),
        name="gla_scan",
    )(q, k, v, gd, gd, wup, wup, tri, mask)


def _gla_constants():
    i = np.arange(GLA_CHUNK)[:, None]
    j = np.arange(GLA_CHUNK)[None, :]
    lower = (j <= i).astype(np.float32)
    upper = (j >= i).astype(np.float32)
    tri = np.zeros((2, GLA_CHUNK + TOT_ROWS, GLA_CHUNK), np.float32)
    tri[0, :GLA_CHUNK] = lower
    tri[1, :GLA_CHUNK] = upper
    tri[:, GLA_CHUNK:] = 1.0
    mask = np.stack([(j <= i), (j > i)]).astype(np.float32)
    return jnp.asarray(tri, BF16), jnp.asarray(mask, F32)


def _rel_tables():
    i = np.arange(BLOCK)[:, None]
    j = np.arange(3 * BLOCK)[None, :]
    rel = j - BLOCK - i
    half = REL_BUCKETS // 2
    max_exact = half // 2
    n = np.abs(rel)
    large = max_exact + (np.log(np.maximum(n, 1) / max_exact) / np.log(REL_MAX_DIST / max_exact)
                         * (half - max_exact)).astype(np.int32)
    large = np.minimum(large, half - 1)
    bucket = (rel > 0).astype(np.int32) * half + np.where(n < max_exact, n, large)
    band = np.abs(rel) <= WINDOW
    col = np.broadcast_to(j, rel.shape)
    valid = np.stack([band & (col >= BLOCK), band, band & (col < 2 * BLOCK)])
    return np.ascontiguousarray(bucket.T).astype(np.int32), np.ascontiguousarray(valid.transpose(0, 2, 1)).astype(np.int32)


def _bias_kernel(rb_ref, bucket_ref, valid_ref, out_ref):
    h = pl.program_id(0)
    bucket = bucket_ref[...]
    acc = jnp.zeros(bucket.shape, F32)
    for kk in range(REL_BUCKETS):
        acc = jnp.where(bucket == kk, rb_ref[kk, h] * LOG2E, acc)
    for kind in range(3):
        out_ref[kind, 0] = jnp.where(valid_ref[kind] > 0, acc, NEG_BIG)


def _bias_table(rel_bias):
    bucket, valid = _rel_tables()
    return pl.pallas_call(
        _bias_kernel,
        grid=(SWA_HEADS,),
        in_specs=[
            pl.BlockSpec(memory_space=pltpu.SMEM),
            pl.BlockSpec((3 * BLOCK, BLOCK), lambda h: (0, 0)),
            pl.BlockSpec((3, 3 * BLOCK, BLOCK), lambda h: (0, 0, 0)),
        ],
        out_specs=pl.BlockSpec((3, 1, 3 * BLOCK, BLOCK), lambda h: (0, h, 0, 0)),
        out_shape=jax.ShapeDtypeStruct((3, SWA_HEADS, 3 * BLOCK, BLOCK), F32),
        compiler_params=_params("arbitrary"),
        name="swa_bias_table",
    )(rel_bias.astype(F32), jnp.asarray(bucket), jnp.asarray(valid))


def _swa_kernel(sink_ref, q_ref, z_ref, kp_ref, kc_ref, kn_ref, vp_ref, vc_ref, vn_ref, bias_ref, o_ref,
                km_ref, va_ref, st_ref, pt_ref, ot_ref, *, nq):
    n = pl.program_id(1)
    nsteps = pl.num_programs(1)
    combos = [(g, e) for g in range(SWA_KV_HEADS) for e in range(2)]
    ncomb = len(combos)
    kcat = jnp.concatenate([kp_ref[0], kc_ref[0], kn_ref[0]], axis=0)
    kswap = jnp.concatenate([kcat[:, SWA_HD:], kcat[:, :SWA_HD]], axis=1)
    low = lax.broadcasted_iota(jnp.int32, kcat.shape, 1) < SWA_HD
    zero = jnp.zeros_like(kcat)
    km_ref[0] = jnp.where(low, kcat, zero)
    km_ref[1] = jnp.where(low, zero, kswap)
    km_ref[2] = jnp.where(low, kswap, zero)
    km_ref[3] = jnp.where(low, zero, kcat)
    vt = jnp.concatenate([vp_ref[0], vc_ref[0], vn_ref[0]], axis=0).astype(F32).T.astype(BF16)
    for g in range(SWA_KV_HEADS):
        va_ref[g, :SWA_HD, :] = vt[g * SWA_HD:(g + 1) * SWA_HD]
        va_ref[g, SWA_HD:, :] = jnp.ones((SWA_ONES_ROWS, vt.shape[1]), BF16)
    half = lax.broadcasted_iota(jnp.int32, (1, 2 * BLOCK), 1) < BLOCK
    pairs_per_kv = SWA_HEADS // SWA_KV_HEADS // 2
    pairs = [[slice((pairs_per_kv * g + i) * 128, (pairs_per_kv * g + i + 1) * 128) for i in range(pairs_per_kv)]
             for g in range(SWA_KV_HEADS)]
    for qb0 in range(0, nq, SWA_GROUP):
        group = list(range(qb0, min(qb0 + SWA_GROUP, nq)))
        qrows = {qb: slice(qb * BLOCK, (qb + 1) * BLOCK) for qb in group}
        keys = {qb: slice(qb * BLOCK, (qb + 3) * BLOCK) for qb in group}
        slot = {qb: (qb % SWA_GROUP) * ncomb for qb in group}
        for qb in group:
            kind = 1
            if qb == 0:
                kind = jnp.where(n == 0, 0, kind)
            if qb == nq - 1:
                kind = jnp.where(n == nsteps - 1, 2, kind)
            for c, (g, e) in enumerate(combos):
                h0 = 2 * pairs_per_kv * g + e
                qg = jnp.concatenate([q_ref[0, qrows[qb], ps] for ps in pairs[g]], axis=0)
                st_ref[slot[qb] + c] = (_dot_nt(km_ref[c, keys[qb], :], qg)
                                        + jnp.concatenate([bias_ref[kind, h0], bias_ref[kind, h0 + 2]], axis=1))
        stats = {}
        for qb in group:
            for c, (g, e) in enumerate(combos):
                h0 = 2 * pairs_per_kv * g + e
                sink = jnp.where(half, sink_ref[0, h0], sink_ref[0, h0 + 2]) * LOG2E
                m = jnp.maximum(jnp.max(st_ref[slot[qb] + c], axis=0, keepdims=True), sink)
                pt_ref[slot[qb] + c] = jnp.exp2(st_ref[slot[qb] + c] - m).astype(BF16)
                stats[(qb, c)] = jnp.exp2(sink - m)
        for qb in group:
            for c, (g, e) in enumerate(combos):
                ot = _dot(va_ref[g, :, keys[qb]], pt_ref[slot[qb] + c])
                ot_ref[slot[qb] + c] = ot[:SWA_HD] * (1.0 / (ot[SWA_HD:SWA_HD + 1] + stats[(qb, c)]))
        for qb in group:
            for g in range(SWA_KV_HEADS):
                for i, ps in enumerate(pairs[g]):
                    cs = slice(i * BLOCK, (i + 1) * BLOCK)
                    o = jnp.concatenate([ot_ref[slot[qb] + 2 * g, :, cs], ot_ref[slot[qb] + 2 * g + 1, :, cs]],
                                        axis=0).T
                    o_ref[0, qrows[qb], ps] = (o * _silu(z_ref[0, qrows[qb], ps].astype(F32))).astype(BF16)


def _swa(q, z, k, v, bias, sink, nq):
    bn, ln, _ = q.shape
    nb = ln // BLOCK
    assert nb % nq == 0 and nb >= 2
    prev = pl.BlockSpec((1, BLOCK, SWA_KVW), lambda b, n: (b, jnp.maximum(n * nq - 1, 0), 0))
    own = pl.BlockSpec((1, nq * BLOCK, SWA_KVW), lambda b, n: (b, n, 0))
    nxt = pl.BlockSpec((1, BLOCK, SWA_KVW), lambda b, n: (b, jnp.minimum((n + 1) * nq, nb - 1), 0))
    return pl.pallas_call(
        functools.partial(_swa_kernel, nq=nq),
        grid=(bn, nb // nq),
        in_specs=[
            pl.BlockSpec(memory_space=pltpu.SMEM),
            pl.BlockSpec((1, nq * BLOCK, SWA_WIDTH), lambda b, n: (b, n, 0)),
            pl.BlockSpec((1, nq * BLOCK, SWA_WIDTH), lambda b, n: (b, n, 0)),
            prev, own, nxt,
            prev, own, nxt,
            pl.BlockSpec((3, SWA_HEADS, 3 * BLOCK, BLOCK), lambda b, n: (0, 0, 0, 0)),
        ],
        out_specs=pl.BlockSpec((1, nq * BLOCK, SWA_WIDTH), lambda b, n: (b, n, 0)),
        out_shape=jax.ShapeDtypeStruct((bn, ln, SWA_WIDTH), BF16),
        scratch_shapes=[
            pltpu.VMEM((2 * SWA_KV_HEADS, (nq + 2) * BLOCK, SWA_KVW), BF16),
            pltpu.VMEM((SWA_KV_HEADS, SWA_HD + SWA_ONES_ROWS, (nq + 2) * BLOCK), BF16),
            pltpu.VMEM((SWA_GROUP * 2 * SWA_KV_HEADS, 3 * BLOCK, 2 * BLOCK), F32),
            pltpu.VMEM((SWA_GROUP * 2 * SWA_KV_HEADS, 3 * BLOCK, 2 * BLOCK), BF16),
            pltpu.VMEM((SWA_GROUP * 2 * SWA_KV_HEADS, SWA_HD, 2 * BLOCK), F32),
        ],
        compiler_params=_params("parallel", "arbitrary"),
        name="swa_attention",
    )(sink.reshape(1, SWA_HEADS).astype(F32), q, z, k, k, k, v, v, v, bias)


ODD_CHUNK = 256
HALO = 8
LN_ROWS = 256


def _tail_kernel(of_ref, ob_ref, z_ref, yb_ref, x_ref, wmix_ref, ng_ref, lg0_ref, lb0_ref,
                 win_ref, cw_ref, wout_ref, lg1_ref, lb1_ref, out_ref,
                 x1_ref, halo_ref, xcat_ref, u_ref, th_ref, mixed_ref, *, tm, nt):
    s = pl.program_id(0)
    last_step = pl.num_programs(0) - 1

    @pl.when(s == 0)
    def _():
        halo_ref[...] = jnp.zeros_like(halo_ref)

    @pl.when(s < last_step)
    def _even_tail():
        slot = lax.rem(s, 2)
        for r0 in range(0, tm, LN_ROWS):
            rs = slice(r0, r0 + LN_ROWS)
            o = of_ref[0, 0, rs, :].astype(F32) + ob_ref[0, 0, rs, :].astype(F32)
            parts = []
            for h in range(GLA_HEADS):
                oh = o[:, h * GLA_DK:(h + 1) * GLA_DK]
                parts.append(oh * lax.rsqrt(jnp.mean(oh * oh, axis=-1, keepdims=True) + NORM_EPS))
            on = jnp.concatenate(parts, axis=1) * ng_ref[...]
            ya = (on * z_ref[0, rs, :].astype(F32)).astype(BF16)
            sub = _dot(ya, wmix_ref[:GLA_WIDTH, :]) + _dot(yb_ref[0, rs, :], wmix_ref[GLA_WIDTH:, :])
            x1_ref[slot, rs, :] = _deepnorm(x_ref[0, rs, :], sub, lg0_ref[...], lb0_ref[...])

    @pl.when(s > 0)
    def _odd_layer():
        t = lax.rem(s - 1, nt)
        cur = x1_ref.at[lax.rem(s - 1, 2)]
        nxt = x1_ref.at[lax.rem(s, 2)]
        _odd_body(t, nt, halo_ref, cur, nxt, win_ref, cw_ref, wout_ref, lg1_ref, lb1_ref, out_ref,
                  xcat_ref, u_ref, th_ref, mixed_ref, tm)
        halo_ref[...] = cur[tm - HALO:tm, :]


def _odd_body(t, nt, prev_ref, x_ref, next_ref, win_ref, cw_ref, wout_ref, lg_ref, lb_ref, out_ref,
              xcat_ref, u_ref, th_ref, mixed_ref, tm):
    main = slice(HALO, HALO + tm)
    xcat_ref[...] = jnp.concatenate([prev_ref[...], x_ref[...], next_ref[0:HALO, :]], axis=0).astype(BF16)
    nj = CONV_WIDTH // ODD_CHUNK

    def in_proj(j):
        for i in range(4):
            u_ref[j % 2, :, i * ODD_CHUNK:(i + 1) * ODD_CHUNK] = _dot(
                xcat_ref[...], win_ref[:, i * CONV_WIDTH + j * ODD_CHUNK:i * CONV_WIDTH + (j + 1) * ODD_CHUNK])

    in_proj(0)
    for j in range(nj):
        cols = slice(j * ODD_CHUNK, (j + 1) * ODD_CHUNK)
        ub = u_ref.at[j % 2]
        if j + 1 < nj:
            in_proj(j + 1)
        th_ref[...] = ub[:, ODD_CHUNK:2 * ODD_CHUNK] * ub[:, 2 * ODD_CHUNK:3 * ODD_CHUNK]
        first = pl.ds(HALO - 1, 1)
        last = pl.ds(HALO + tm, 1)
        th_ref[first, :] = jnp.where(t == 0, 0.0, th_ref[first, :])
        th_ref[last, :] = jnp.where(t == nt - 1, 0.0, th_ref[last, :])
        conv = (cw_ref[0:1, cols] * th_ref[HALO - 1:HALO - 1 + tm, :]
                + cw_ref[1:2, cols] * th_ref[main, :]
                + cw_ref[2:3, cols] * th_ref[HALO + 1:HALO + 1 + tm, :])
        mixed_ref[:, cols] = (_silu(ub[main, 3 * ODD_CHUNK:]) * ub[main, :ODD_CHUNK] * conv).astype(BF16)
    for r0 in range(0, tm, LN_ROWS):
        rs = slice(r0, r0 + LN_ROWS)
        acc = _dot(mixed_ref[rs, :], wout_ref[...])
        out_ref[0, rs, :] = _deepnorm(x_ref[rs, :], acc, lg_ref[...], lb_ref[...])


def _tail(o, za, yb, x, wmix, ng, lg0, lb0, win, cw, wout, lg1, lb1, tm):
    bn, ln, _ = x.shape
    nt = ln // tm
    ntiles = bn * nt
    rows = tm + 2 * HALO

    def tile_in(s):
        g = jnp.minimum(s, ntiles - 1)
        return g // nt, g % nt

    def tile_out(s):
        g = jnp.maximum(s - 1, 0)
        return g // nt, g % nt

    whole = lambda *shape: pl.BlockSpec(shape, lambda s: (0,) * len(shape), pipeline_mode=pl.Buffered(1))
    return pl.pallas_call(
        functools.partial(_tail_kernel, tm=tm, nt=nt),
        grid=(ntiles + 1,),
        in_specs=[
            pl.BlockSpec((1, 1, tm, GLA_WIDTH), lambda s: (0, *tile_in(s), 0)),
            pl.BlockSpec((1, 1, tm, GLA_WIDTH), lambda s: (1, *tile_in(s), 0)),
            pl.BlockSpec((1, tm, GLA_WIDTH), lambda s: (*tile_in(s), 0)),
            pl.BlockSpec((1, tm, SWA_WIDTH), lambda s: (*tile_in(s), 0)),
            pl.BlockSpec((1, tm, D_MODEL), lambda s: (*tile_in(s), 0)),
            whole(GLA_WIDTH + SWA_WIDTH, D_MODEL),
            whole(1, GLA_WIDTH), whole(1, D_MODEL), whole(1, D_MODEL),
            whole(D_MODEL, 4 * CONV_WIDTH), whole(3, CONV_WIDTH), whole(CONV_WIDTH, D_MODEL),
            whole(1, D_MODEL), whole(1, D_MODEL),
        ],
        out_specs=pl.BlockSpec((1, tm, D_MODEL), lambda s: (*tile_out(s), 0)),
        out_shape=jax.ShapeDtypeStruct((bn, ln, D_MODEL), F32),
        scratch_shapes=[
            pltpu.VMEM((2, tm, D_MODEL), F32),
            pltpu.VMEM((HALO, D_MODEL), F32),
            pltpu.VMEM((rows, D_MODEL), BF16),
            pltpu.VMEM((2, rows, 4 * ODD_CHUNK), F32),
            pltpu.VMEM((rows, ODD_CHUNK), F32),
            pltpu.VMEM((tm, CONV_WIDTH), BF16),
        ],
        compiler_params=_params("arbitrary"),
        name="even_tail_odd_layer",
    )(o, o, za, yb, x, wmix, ng, lg0, lb0, win, cw, wout, lg1, lb1)


PREP_ROWS = 256


def _prep_even_kernel(w_ref, o_ref):
    g0 = GLA_COLS
    qb0 = g0 + GD_COLS
    kb0 = qb0 + SWA_WIDTH
    vb0 = kb0 + SWA_KVW
    zb0 = vb0 + SWA_KVW
    o_ref[:, :GLA_WIDTH] = (w_ref[:, :GLA_WIDTH] * (GLA_DK ** -0.5)).astype(BF16)
    o_ref[:, GLA_WIDTH:GLA_COLS] = w_ref[:, GLA_WIDTH:GLA_COLS].astype(BF16)
    c = GLA_COLS
    o_ref[:, c:c + SWA_WIDTH] = (w_ref[:, qb0:qb0 + SWA_WIDTH] * (SWA_HD ** -0.5 * LOG2E)).astype(BF16)
    c += SWA_WIDTH
    o_ref[:, c:c + SWA_WIDTH] = w_ref[:, zb0:zb0 + SWA_WIDTH].astype(BF16)
    c += SWA_WIDTH
    o_ref[:, c:c + SWA_KVW] = w_ref[:, kb0:kb0 + SWA_KVW].astype(BF16)
    c += SWA_KVW
    o_ref[:, c:c + SWA_KVW] = w_ref[:, vb0:vb0 + SWA_KVW].astype(BF16)
    c += SWA_KVW
    lane = lax.broadcasted_iota(jnp.int32, (PREP_ROWS, GD_PAD), 1)
    o_ref[:, c:] = jnp.where(lane < GD_COLS, w_ref[:, g0:g0 + GD_PAD], 0.0).astype(BF16)


def _prep_even(w_in_layers, w_up_f, b_f, w_up_b, b_b, norm_g, w_out_layers):
    even_in = w_in_layers.shape[2]
    w = pl.pallas_call(
        _prep_even_kernel,
        grid=(D_MODEL // PREP_ROWS,),
        in_specs=[pl.BlockSpec((None, PREP_ROWS, even_in), lambda r: (0, r, 0))],
        out_specs=pl.BlockSpec((PREP_ROWS, EVEN_COLS), lambda r: (r, 0)),
        out_shape=jax.ShapeDtypeStruct((D_MODEL, EVEN_COLS), BF16),
        compiler_params=_params("parallel"),
        name="prep_even_weights",
    )(w_in_layers)
    zr = jnp.zeros_like(w_up_f)
    zpad = jnp.zeros((GD_PAD - GD_COLS - 1, GLA_WIDTH), w_up_f.dtype)
    wup = (jnp.stack([jnp.concatenate([w_up_f, zr, b_f[None], zpad], axis=0),
                      jnp.concatenate([zr, w_up_b, b_b[None], zpad], axis=0)]) * LOG2E).astype(BF16)
    ng = jnp.tile(norm_g.astype(F32), GLA_HEADS).reshape(1, GLA_WIDTH)
    return w, wup, ng, _layer0_bf16(w_out_layers, 1.0 / DN_ALPHA)


def _cast_kernel(w_ref, o_ref, *, scale):
    o_ref[...] = (w_ref[...] * scale).astype(BF16)


def _layer0_bf16(w_layers, scale=1.0):
    _, rows, cols = w_layers.shape
    return pl.pallas_call(
        functools.partial(_cast_kernel, scale=scale),
        grid=(rows // PREP_ROWS,),
        in_specs=[pl.BlockSpec((None, PREP_ROWS, cols), lambda r: (0, r, 0))],
        out_specs=pl.BlockSpec((PREP_ROWS, cols), lambda r: (r, 0)),
        out_shape=jax.ShapeDtypeStruct((rows, cols), BF16),
        compiler_params=_params("parallel"),
        name="cast_weights",
    )(w_layers)


def _prep_odd(w_in_layers, conv_w, w_out_layers):
    return _layer0_bf16(w_in_layers), conv_w.astype(F32), _layer0_bf16(w_out_layers, 1.0 / DN_ALPHA)


TILE_INPROJ = 1024
TILE_GLA = 2048
TILE_SWA_BLOCKS = 16
TILE_TAIL = 1024


def _trunk(x, even, odd, bias_tab, sink, ln_g, ln_b, consts,
           tm=TILE_TAIL, tm_in=TILE_INPROJ, gla_tile=TILE_GLA, swa_nq=TILE_SWA_BLOCKS):
    w, wup, ng, wmix = even
    win, cw, wout = odd
    tri, mask = consts
    qa, ka, va, za, qb, zb, kb, vb, gd = _inproj_even(x, w, tm_in)
    o = _gla(qa, ka, va, gd, wup, tri, mask, gla_tile)
    yb = _swa(qb, zb, kb, vb, bias_tab, sink, swa_nq)
    lg = ln_g.astype(F32).reshape(DEPTH, 1, D_MODEL)
    lb = ln_b.astype(F32).reshape(DEPTH, 1, D_MODEL)
    return _tail(o, za, yb, x, wmix, ng, lg[0], lb[0], win, cw, wout, lg[1], lb[1], tm)


def kernel(x_prompt, x_sample, w_in_even, gla_w_up_fwd, gla_b_fwd, gla_w_up_bwd, gla_b_bwd, gla_norm_g, swa_sink,
           rel_bias, w_out_even, w_in_odd, conv_w, w_out_odd, ln_g, ln_b):
    even = _prep_even(w_in_even, gla_w_up_fwd[0], gla_b_fwd[0], gla_w_up_bwd[0], gla_b_bwd[0], gla_norm_g[0],
                      w_out_even)
    odd = _prep_odd(w_in_odd, conv_w[0], w_out_odd)
    bias_tab = _bias_table(rel_bias)
    consts = _gla_constants()
    run = lambda x: _trunk(x, even, odd, bias_tab, swa_sink[0], ln_g, ln_b, consts)
    return (run(x_prompt), run(x_sample))
```

```python
import functools
import math

import numpy as np
import jax
import jax.numpy as jnp
from jax import lax
from jax.experimental import pallas as pl
from jax.experimental.pallas import tpu as pltpu

F32 = jnp.float32
BF16 = jnp.bfloat16

D_MODEL = 1024
DEPTH = 2
GLA_HEADS = 4
GLA_DK = 128
GLA_WIDTH = 512
GLA_RANK = 16
GLA_TAU = 16.0
SWA_HEADS = 8
SWA_KV_HEADS = 2
SWA_HD = 64
SWA_WIDTH = 512
SWA_KVW = 128
WINDOW = 128
BLOCK = 128
REL_BUCKETS = 32
REL_MAX_DIST = 128
CONV_WIDTH = 1024
DN_ALPHA = (2 * DEPTH) ** 0.25
LN_EPS = 1e-5
NORM_EPS = 1e-6
NEG_BIG = -1e30
LOG2E = math.log2(math.e)
LN2 = math.log(2.0)
SWA_ONES_ROWS = 16
SWA_GROUP = 2

GLA_COLS = 4 * GLA_WIDTH
SWA_COLS = 2 * SWA_WIDTH + 2 * SWA_KVW
GD_COLS = 2 * GLA_RANK
GD_PAD = 128
EVEN_COLS = GLA_COLS + SWA_COLS + GD_PAD

GLA_CHUNK = 128
TOT_ROWS = 16
GLA_SAFE_LOGIT = -8.0
VMEM_LIMIT = 56 * 1024 * 1024


def _dot(a, b):
    return jnp.dot(a, b, preferred_element_type=F32)


def _dot_nt(a, b):
    return lax.dot_general(a, b, (((1,), (1,)), ((), ())), preferred_element_type=F32)


def _dot_tn(a, b):
    return lax.dot_general(a, b, (((0,), (0,)), ((), ())), preferred_element_type=F32)


def _silu(z):
    return z / (1.0 + jnp.exp(-z))


def _deepnorm(x, sub_over_alpha, g, b):
    y = x + sub_over_alpha
    mu = jnp.mean(y, axis=-1, keepdims=True)
    yc = y - mu
    var = jnp.mean(yc * yc, axis=-1, keepdims=True)
    return yc * lax.rsqrt(var + LN_EPS / (DN_ALPHA * DN_ALPHA)) * g + b


def _params(*sem):
    return pltpu.CompilerParams(dimension_semantics=sem, vmem_limit_bytes=VMEM_LIMIT)


EVEN_OUT_WIDTHS = (GLA_WIDTH,) * 4 + (SWA_WIDTH, SWA_WIDTH, SWA_KVW, SWA_KVW, GD_PAD)
EVEN_GATE_OUTPUTS = (3,)


def _inproj_even_kernel(x_ref, w_ref, *out_refs):
    xb = x_ref[0].astype(BF16)
    c0 = 0
    for i, (ref, width) in enumerate(zip(out_refs[:-3], EVEN_OUT_WIDTHS[:-3])):
        u = _dot(xb, w_ref[:, c0:c0 + width])
        ref[0] = (_silu(u) if i in EVEN_GATE_OUTPUTS else u).astype(BF16)
        c0 += width
    kvg = _dot(xb, w_ref[:, c0:])
    out_refs[-3][0] = kvg[:, :SWA_KVW].astype(BF16)
    out_refs[-2][0] = kvg[:, SWA_KVW:2 * SWA_KVW].astype(BF16)
    one_lane = (lax.broadcasted_iota(jnp.int32, (1, GD_PAD), 1) == GD_COLS).astype(F32)
    out_refs[-1][0] = (kvg[:, 2 * SWA_KVW:] + one_lane).astype(BF16)


def _inproj_even(x, w, tm):
    bn, ln, _ = x.shape
    return pl.pallas_call(
        _inproj_even_kernel,
        grid=(bn, ln // tm),
        in_specs=[
            pl.BlockSpec((1, tm, D_MODEL), lambda b, t: (b, t, 0)),
            pl.BlockSpec((D_MODEL, EVEN_COLS), lambda b, t: (0, 0), pipeline_mode=pl.Buffered(1)),
        ],
        out_specs=[pl.BlockSpec((1, tm, width), lambda b, t: (b, t, 0)) for width in EVEN_OUT_WIDTHS],
        out_shape=[jax.ShapeDtypeStruct((bn, ln, width), BF16) for width in EVEN_OUT_WIDTHS],
        compiler_params=_params("parallel", "parallel"),
        name="inproj_even",
    )(x, w)


def _gla_kernel(q_ref, k_ref, v_ref, gd_ref, gdn_ref, wup_ref, wupn_ref, tri_ref, mask_ref, o_ref,
                st_ref, flag_ref, logd_ref, sq_ref, kt_ref, u_ref, et_ref, cum_ref, qf_ref, kf_ref,
                *, tile):
    d = pl.program_id(0)
    b = pl.program_id(1)
    t = pl.program_id(2)
    step = (d * pl.num_programs(1) + b) * pl.num_programs(2) + t
    slot = lax.rem(step, 2)

    def gate(gd_blk, w_blk, sl):
        a2 = _dot(gd_blk, w_blk)
        logd_ref[sl] = ((jnp.minimum(a2, 0.0) - jnp.log2(1.0 + jnp.exp2(-jnp.abs(a2)))) * (LN2 / GLA_TAU)).astype(BF16)
        flag_ref[sl] = (jnp.min(a2) < GLA_SAFE_LOGIT * LOG2E).astype(jnp.int32)

    @pl.when(step == 0)
    def _():
        gate(gd_ref[0], wup_ref[0], slot)

    @pl.when(t == 0)
    def _():
        st_ref[...] = jnp.zeros_like(st_ref)

    unsafe = flag_ref[slot] != 0

    @pl.when(jnp.logical_not(unsafe))
    def _():
        _gla_fast_tile(d, q_ref, k_ref, v_ref, tri_ref, mask_ref, o_ref, st_ref, logd_ref.at[slot], sq_ref,
                       kt_ref, u_ref, et_ref, cum_ref, tile,
                       next_gate=lambda: gate(gdn_ref[0], wupn_ref[0], 1 - slot))

    @pl.when(unsafe)
    def _():
        _gla_pairwise_tile(d, q_ref, k_ref, v_ref, tri_ref, o_ref, st_ref, logd_ref.at[slot], cum_ref,
                           qf_ref, kf_ref, tile)
        gate(gdn_ref[0], wupn_ref[0], 1 - slot)


def _gla_pairwise_tile(d, q_ref, k_ref, v_ref, tri_ref, o_ref, st_ref, logd_ref, cum_ref, qf_ref, kf_ref, tile):
    nchunk = tile // GLA_CHUNK
    tri = tri_ref[0]
    jrow = lax.broadcasted_iota(jnp.int32, (GLA_CHUNK, GLA_CHUNK), 0)
    icol = lax.broadcasted_iota(jnp.int32, (GLA_CHUNK, GLA_CHUNK), 1)

    def chunk(c, carry):
        cc = c + d * (nchunk - 1 - 2 * c)
        rows = pl.ds(pl.multiple_of(cc * GLA_CHUNK, GLA_CHUNK), GLA_CHUNK)
        cum_ref[0] = _dot(tri, logd_ref[rows, :])
        for h in range(GLA_HEADS):
            hs = slice(h * GLA_DK, (h + 1) * GLA_DK)
            b = cum_ref[0, :GLA_CHUNK, hs]
            tot = cum_ref[0, GLA_CHUNK:GLA_CHUNK + 1, hs]
            qf_ref[...] = q_ref[0, rows, hs].astype(F32)
            kf_ref[...] = k_ref[0, rows, hs].astype(F32)
            v = v_ref[0, rows, hs]

            def pair_rows(g, st_t):
                base = pl.multiple_of(g * 8, 8)
                b8 = cum_ref[0, pl.ds(base, 8), hs]
                q8 = qf_ref[pl.ds(base, 8), :]
                for r in range(8):
                    i = base + r
                    w = jnp.exp(jnp.minimum(b8[r:r + 1] - cum_ref[0, :GLA_CHUNK, hs], 0.0))
                    col = jnp.sum(q8[r:r + 1] * kf_ref[...] * w, axis=1, keepdims=True)
                    valid = (1 - 2 * d) * (jrow - i) <= -d
                    st_t = jnp.where((icol == i) & valid, col, st_t)
                return st_t

            s = lax.fori_loop(0, GLA_CHUNK // 8, pair_rows, jnp.zeros((GLA_CHUNK, GLA_CHUNK), F32)).T.astype(BF16)
            qt = (qf_ref[...] * jnp.exp(b)).astype(BF16)
            kd = (kf_ref[...] * jnp.exp(tot - b)).astype(BF16)
            st = st_ref[h]
            vs = jnp.concatenate([v, st.T.astype(BF16)], axis=0)
            o_ref[0, 0, rows, hs] = _dot(jnp.concatenate([s, qt], axis=1), vs).astype(BF16)
            st_ref[h] = st * jnp.exp(tot) + _dot_tn(v, kd)
        return carry

    lax.fori_loop(0, nchunk, chunk, 0)


def _gla_fast_tile(d, q_ref, k_ref, v_ref, tri_ref, mask_ref, o_ref, st_ref, logd_ref, sq_ref,
                   kt_ref, u_ref, et_ref, cum_ref, tile, next_gate):
    nchunk = tile // GLA_CHUNK
    tri = tri_ref[0]
    keep = mask_ref[0] > 0.0
    heads = [slice(h * GLA_DK, (h + 1) * GLA_DK) for h in range(GLA_HEADS)]
    s_cols = [slice(2 * h * GLA_DK, (2 * h + 1) * GLA_DK) for h in range(GLA_HEADS)]
    q_cols = [slice((2 * h + 1) * GLA_DK, (2 * h + 2) * GLA_DK) for h in range(GLA_HEADS)]
    sq_cols = [slice(2 * h * GLA_DK, (2 * h + 2) * GLA_DK) for h in range(GLA_HEADS)]

    def phase1(c):
        rows = slice(c * GLA_CHUNK, (c + 1) * GLA_CHUNK)
        cum_ref[c % 2] = _dot(tri, logd_ref[rows, :])
        for h, hs in enumerate(heads):
            b = cum_ref[c % 2, :GLA_CHUNK, hs]
            etot = jnp.exp(cum_ref[c % 2, GLA_CHUNK:GLA_CHUNK + 1, hs])
            sq_ref[rows, q_cols[h]] = (q_ref[0, rows, hs].astype(F32) * jnp.exp(b)).astype(BF16)
            kt_ref[rows, hs] = (k_ref[0, rows, hs].astype(F32) * jnp.exp(-b)).astype(BF16)
            et_ref[c, :, hs] = etot

    def phase2(c):
        rows = slice(c * GLA_CHUNK, (c + 1) * GLA_CHUNK)
        for h, hs in enumerate(heads):
            sq_ref[rows, s_cols[h]] = jnp.where(
                keep, _dot_nt(sq_ref[rows, q_cols[h]], kt_ref[rows, hs]), 0.0).astype(BF16)
            u_ref[c, h] = _dot_tn(v_ref[0, rows, hs], kt_ref[rows, hs])

    phase1(0)
    phase1(1)
    for c in range(nchunk):
        if c + 2 < nchunk:
            phase1(c + 2)
        phase2(c)
    next_gate()

    def phase3(order):
        for c in order:
            rows = slice(c * GLA_CHUNK, (c + 1) * GLA_CHUNK)
            for h, hs in enumerate(heads):
                st = st_ref[h]
                vs = jnp.concatenate([v_ref[0, rows, hs], st.T.astype(BF16)], axis=0)
                o_ref[0, 0, rows, hs] = _dot(sq_ref[rows, sq_cols[h]], vs).astype(BF16)
                st_ref[h] = (st + u_ref[c, h]) * et_ref[c, :, hs]

    @pl.when(d == 0)
    def _():
        phase3(range(nchunk))

    @pl.when(d != 0)
    def _():
        phase3(range(nchunk - 1, -1, -1))


def _gla(q, k, v, gd, wup, tri, mask, tile):
    bn, ln, _ = q.shape
    nt = ln // tile

    def tok(d, b, t):
        return t + d * (nt - 1 - 2 * t)

    def nxt(d, b, t):
        roll_t = t == nt - 1
        roll_b = roll_t & (b == bn - 1)
        d2 = jnp.minimum(jnp.where(roll_b, d + 1, d), 1)
        b2 = jnp.where(roll_b, 0, jnp.where(roll_t, b + 1, b))
        return d2, b2, jnp.where(roll_t, 0, t + 1)

    def gd_next(d, b, t):
        d2, b2, t2 = nxt(d, b, t)
        return b2, tok(d2, b2, t2), 0

    return pl.pallas_call(
        functools.partial(_gla_kernel, tile=tile),
        grid=(2, bn, nt),
        in_specs=[
            pl.BlockSpec((1, tile, GLA_WIDTH), lambda d, b, t: (b, tok(d, b, t), 0)),
            pl.BlockSpec((1, tile, GLA_WIDTH), lambda d, b, t: (b, tok(d, b, t), 0)),
            pl.BlockSpec((1, tile, GLA_WIDTH), lambda d, b, t: (b, tok(d, b, t), 0)),
            pl.BlockSpec((1, tile, GD_PAD), lambda d, b, t: (b, tok(d, b, t), 0)),
            pl.BlockSpec((1, tile, GD_PAD), gd_next),
            pl.BlockSpec((1, GD_PAD, GLA_WIDTH), lambda d, b, t: (d, 0, 0)),
            pl.BlockSpec((1, GD_PAD, GLA_WIDTH), lambda d, b, t: (nxt(d, b, t)[0], 0, 0)),
            pl.BlockSpec((1, GLA_CHUNK + TOT_ROWS, GLA_CHUNK), lambda d, b, t: (d, 0, 0)),
            pl.BlockSpec((1, GLA_CHUNK, GLA_CHUNK), lambda d, b, t: (d, 0, 0)),
        ],
        out_specs=pl.BlockSpec((1, 1, tile, GLA_WIDTH), lambda d, b, t: (d, b, tok(d, b, t), 0)),
        out_shape=jax.ShapeDtypeStruct((2, bn, ln, GLA_WIDTH), BF16),
        scratch_shapes=[
            pltpu.VMEM((GLA_HEADS, GLA_DK, GLA_DK), F32),
            pltpu.SMEM((2,), jnp.int32),
            pltpu.VMEM((2, tile, GLA_WIDTH), BF16),
            pltpu.VMEM((tile, 2 * GLA_WIDTH), BF16),
            pltpu.VMEM((tile, GLA_WIDTH), BF16),
            pltpu.VMEM((tile // GLA_CHUNK, GLA_HEADS, GLA_DK, GLA_DK), F32),
            pltpu.VMEM((tile // GLA_CHUNK, 1, GLA_WIDTH), F32),
            pltpu.VMEM((2, GLA_CHUNK + TOT_ROWS, GLA_WIDTH), F32),
            pltpu.VMEM((GLA_CHUNK, GLA_DK), F32),
            pltpu.VMEM((GLA_CHUNK, GLA_DK), F32),
        ],
        compiler_params=_params("arbitrary", "arbitrary", "arbitrary"),
        name="gla_scan",
    )(q, k, v, gd, gd, wup, wup, tri, mask)


def _gla_constants():
    i = np.arange(GLA_CHUNK)[:, None]
    j = np.arange(GLA_CHUNK)[None, :]
    lower = (j <= i).astype(np.float32)
    upper = (j >= i).astype(np.float32)
    tri = np.zeros((2, GLA_CHUNK + TOT_ROWS, GLA_CHUNK), np.float32)
    tri[0, :GLA_CHUNK] = lower
    tri[1, :GLA_CHUNK] = upper
    tri[:, GLA_CHUNK:] = 1.0
    mask = np.stack([(j <= i), (j > i)]).astype(np.float32)
    return jnp.asarray(tri, BF16), jnp.asarray(mask, F32)


def _rel_tables():
    i = np.arange(BLOCK)[:, None]
    j = np.arange(3 * BLOCK)[None, :]
    rel = j - BLOCK - i
    half = REL_BUCKETS // 2
    max_exact = half // 2
    n = np.abs(rel)
    large = max_exact + (np.log(np.maximum(n, 1) / max_exact) / np.log(REL_MAX_DIST / max_exact)
                         * (half - max_exact)).astype(np.int32)
    large = np.minimum(large, half - 1)
    bucket = (rel > 0).astype(np.int32) * half + np.where(n < max_exact, n, large)
    band = np.abs(rel) <= WINDOW
    col = np.broadcast_to(j, rel.shape)
    valid = np.stack([band & (col >= BLOCK), band, band & (col < 2 * BLOCK)])
    return np.ascontiguousarray(bucket.T).astype(np.int32), np.ascontiguousarray(valid.transpose(0, 2, 1)).astype(np.int32)


def _bias_kernel(rb_ref, bucket_ref, valid_ref, out_ref):
    h = pl.program_id(0)
    bucket = bucket_ref[...]
    acc = jnp.zeros(bucket.shape, F32)
    for kk in range(REL_BUCKETS):
        acc = jnp.where(bucket == kk, rb_ref[kk, h] * LOG2E, acc)
    for kind in range(3):
        out_ref[kind, 0] = jnp.where(valid_ref[kind] > 0, acc, NEG_BIG)


def _bias_table(rel_bias):
    bucket, valid = _rel_tables()
    return pl.pallas_call(
        _bias_kernel,
        grid=(SWA_HEADS,),
        in_specs=[
            pl.BlockSpec(memory_space=pltpu.SMEM),
            pl.BlockSpec((3 * BLOCK, BLOCK), lambda h: (0, 0)),
            pl.BlockSpec((3, 3 * BLOCK, BLOCK), lambda h: (0, 0, 0)),
        ],
        out_specs=pl.BlockSpec((3, 1, 3 * BLOCK, BLOCK), lambda h: (0, h, 0, 0)),
        out_shape=jax.ShapeDtypeStruct((3, SWA_HEADS, 3 * BLOCK, BLOCK), F32),
        compiler_params=_params("arbitrary"),
        name="swa_bias_table",
    )(rel_bias.astype(F32), jnp.asarray(bucket), jnp.asarray(valid))


def _swa_kernel(sink_ref, q_ref, z_ref, kp_ref, kc_ref, kn_ref, vp_ref, vc_ref, vn_ref, bias_ref, o_ref,
                km_ref, va_ref, st_ref, pt_ref, ot_ref, *, nq):
    n = pl.program_id(1)
    nsteps = pl.num_programs(1)
    combos = [(g, e) for g in range(SWA_KV_HEADS) for e in range(2)]
    ncomb = len(combos)
    kcat = jnp.concatenate([kp_ref[0], kc_ref[0], kn_ref[0]], axis=0)
    kswap = jnp.concatenate([kcat[:, SWA_HD:], kcat[:, :SWA_HD]], axis=1)
    low = lax.broadcasted_iota(jnp.int32, kcat.shape, 1) < SWA_HD
    zero = jnp.zeros_like(kcat)
    km_ref[0] = jnp.where(low, kcat, zero)
    km_ref[1] = jnp.where(low, zero, kswap)
    km_ref[2] = jnp.where(low, kswap, zero)
    km_ref[3] = jnp.where(low, zero, kcat)
    vt = jnp.concatenate([vp_ref[0], vc_ref[0], vn_ref[0]], axis=0).astype(F32).T.astype(BF16)
    for g in range(SWA_KV_HEADS):
        va_ref[g, :SWA_HD, :] = vt[g * SWA_HD:(g + 1) * SWA_HD]
        va_ref[g, SWA_HD:, :] = jnp.ones((SWA_ONES_ROWS, vt.shape[1]), BF16)
    half = lax.broadcasted_iota(jnp.int32, (1, 2 * BLOCK), 1) < BLOCK
    pairs_per_kv = SWA_HEADS // SWA_KV_HEADS // 2
    pairs = [[slice((pairs_per_kv * g + i) * 128, (pairs_per_kv * g + i + 1) * 128) for i in range(pairs_per_kv)]
             for g in range(SWA_KV_HEADS)]
    qrows = lambda qb: slice(qb * BLOCK, (qb + 1) * BLOCK)
    keys = lambda qb: slice(qb * BLOCK, (qb + 3) * BLOCK)
    slot = lambda qb: (qb % (2 * SWA_GROUP)) * ncomb
    groups = [list(range(qb0, min(qb0 + SWA_GROUP, nq))) for qb0 in range(0, nq, SWA_GROUP)]

    def stage1(group):
        for qb in group:
            kind = 1
            if qb == 0:
                kind = jnp.where(n == 0, 0, kind)
            if qb == nq - 1:
                kind = jnp.where(n == nsteps - 1, 2, kind)
            for c, (g, e) in enumerate(combos):
                h0 = 2 * pairs_per_kv * g + e
                qg = jnp.concatenate([q_ref[0, qrows(qb), ps] for ps in pairs[g]], axis=0)
                st_ref[slot(qb) + c] = (_dot_nt(km_ref[c, keys(qb), :], qg)
                                        + jnp.concatenate([bias_ref[kind, h0], bias_ref[kind, h0 + 2]], axis=1))

    def stages23(group):
        stats = {}
        for qb in group:
            for c, (g, e) in enumerate(combos):
                h0 = 2 * pairs_per_kv * g + e
                sink = jnp.where(half, sink_ref[0, h0], sink_ref[0, h0 + 2]) * LOG2E
                m = jnp.maximum(jnp.max(st_ref[slot(qb) + c], axis=0, keepdims=True), sink)
                pt_ref[slot(qb) + c] = jnp.exp2(st_ref[slot(qb) + c] - m).astype(BF16)
                stats[(qb, c)] = jnp.exp2(sink - m)
        for qb in group:
            for c, (g, e) in enumerate(combos):
                ot = _dot(va_ref[g, :, keys(qb)], pt_ref[slot(qb) + c])
                ot_ref[slot(qb) + c] = ot[:SWA_HD] * (1.0 / (ot[SWA_HD:SWA_HD + 1] + stats[(qb, c)]))
        for qb in group:
            for g in range(SWA_KV_HEADS):
                for i, ps in enumerate(pairs[g]):
                    cs = slice(i * BLOCK, (i + 1) * BLOCK)
                    o = jnp.concatenate([ot_ref[slot(qb) + 2 * g, :, cs], ot_ref[slot(qb) + 2 * g + 1, :, cs]],
                                        axis=0).T
                    o_ref[0, qrows(qb), ps] = (o * _silu(z_ref[0, qrows(qb), ps].astype(F32))).astype(BF16)

    stage1(groups[0])
    for gi, group in enumerate(groups):
        if gi + 1 < len(groups):
            stage1(groups[gi + 1])
        stages23(group)


def _swa(q, z, k, v, bias, sink, nq):
    bn, ln, _ = q.shape
    nb = ln // BLOCK
    assert nb % nq == 0 and nb >= 2
    prev = pl.BlockSpec((1, BLOCK, SWA_KVW), lambda b, n: (b, jnp.maximum(n * nq - 1, 0), 0))
    own = pl.BlockSpec((1, nq * BLOCK, SWA_KVW), lambda b, n: (b, n, 0))
    nxt = pl.BlockSpec((1, BLOCK, SWA_KVW), lambda b, n: (b, jnp.minimum((n + 1) * nq, nb - 1), 0))
    return pl.pallas_call(
        functools.partial(_swa_kernel, nq=nq),
        grid=(bn, nb // nq),
        in_specs=[
            pl.BlockSpec(memory_space=pltpu.SMEM),
            pl.BlockSpec((1, nq * BLOCK, SWA_WIDTH), lambda b, n: (b, n, 0)),
            pl.BlockSpec((1, nq * BLOCK, SWA_WIDTH), lambda b, n: (b, n, 0)),
            prev, own, nxt,
            prev, own, nxt,
            pl.BlockSpec((3, SWA_HEADS, 3 * BLOCK, BLOCK), lambda b, n: (0, 0, 0, 0)),
        ],
        out_specs=pl.BlockSpec((1, nq * BLOCK, SWA_WIDTH), lambda b, n: (b, n, 0)),
        out_shape=jax.ShapeDtypeStruct((bn, ln, SWA_WIDTH), BF16),
        scratch_shapes=[
            pltpu.VMEM((2 * SWA_KV_HEADS, (nq + 2) * BLOCK, SWA_KVW), BF16),
            pltpu.VMEM((SWA_KV_HEADS, SWA_HD + SWA_ONES_ROWS, (nq + 2) * BLOCK), BF16),
            pltpu.VMEM((2 * SWA_GROUP * 2 * SWA_KV_HEADS, 3 * BLOCK, 2 * BLOCK), F32),
            pltpu.VMEM((2 * SWA_GROUP * 2 * SWA_KV_HEADS, 3 * BLOCK, 2 * BLOCK), BF16),
            pltpu.VMEM((2 * SWA_GROUP * 2 * SWA_KV_HEADS, SWA_HD, 2 * BLOCK), F32),
        ],
        compiler_params=_params("parallel", "arbitrary"),
        name="swa_attention",
    )(sink.reshape(1, SWA_HEADS).astype(F32), q, z, k, k, k, v, v, v, bias)


ODD_CHUNK = 256
HALO = 8
LN_ROWS = 256


def _tail_kernel(of_ref, ob_ref, z_ref, yb_ref, x_ref, wmix_ref, ng_ref, lg0_ref, lb0_ref,
                 win_ref, cw_ref, wout_ref, lg1_ref, lb1_ref, out_ref,
                 x1_ref, halo_ref, xcat_ref, u_ref, th_ref, mixed_ref, *, tm, nt):
    s = pl.program_id(0)
    last_step = pl.num_programs(0) - 1

    @pl.when(s == 0)
    def _():
        halo_ref[...] = jnp.zeros_like(halo_ref)

    @pl.when(s < last_step)
    def _even_tail():
        slot = lax.rem(s, 2)
        for r0 in range(0, tm, LN_ROWS):
            rs = slice(r0, r0 + LN_ROWS)
            o = of_ref[0, 0, rs, :].astype(F32) + ob_ref[0, 0, rs, :].astype(F32)
            parts = []
            for h in range(GLA_HEADS):
                oh = o[:, h * GLA_DK:(h + 1) * GLA_DK]
                parts.append(oh * lax.rsqrt(jnp.mean(oh * oh, axis=-1, keepdims=True) + NORM_EPS))
            on = jnp.concatenate(parts, axis=1) * ng_ref[...]
            ya = (on * z_ref[0, rs, :].astype(F32)).astype(BF16)
            sub = _dot(ya, wmix_ref[:GLA_WIDTH, :]) + _dot(yb_ref[0, rs, :], wmix_ref[GLA_WIDTH:, :])
            x1_ref[slot, rs, :] = _deepnorm(x_ref[0, rs, :], sub, lg0_ref[...], lb0_ref[...])

    @pl.when(s > 0)
    def _odd_layer():
        t = lax.rem(s - 1, nt)
        cur = x1_ref.at[lax.rem(s - 1, 2)]
        nxt = x1_ref.at[lax.rem(s, 2)]
        _odd_body(t, nt, halo_ref, cur, nxt, win_ref, cw_ref, wout_ref, lg1_ref, lb1_ref, out_ref,
                  xcat_ref, u_ref, th_ref, mixed_ref, tm)
        halo_ref[...] = cur[tm - HALO:tm, :]


def _odd_body(t, nt, prev_ref, x_ref, next_ref, win_ref, cw_ref, wout_ref, lg_ref, lb_ref, out_ref,
              xcat_ref, u_ref, th_ref, mixed_ref, tm):
    main = slice(HALO, HALO + tm)
    xcat_ref[...] = jnp.concatenate([prev_ref[...], x_ref[...], next_ref[0:HALO, :]], axis=0).astype(BF16)
    nj = CONV_WIDTH // ODD_CHUNK

    def in_proj(j):
        for i in range(4):
            u_ref[j % 2, :, i * ODD_CHUNK:(i + 1) * ODD_CHUNK] = _dot(
                xcat_ref[...], win_ref[:, i * CONV_WIDTH + j * ODD_CHUNK:i * CONV_WIDTH + (j + 1) * ODD_CHUNK])

    in_proj(0)
    for j in range(nj):
        cols = slice(j * ODD_CHUNK, (j + 1) * ODD_CHUNK)
        ub = u_ref.at[j % 2]
        if j + 1 < nj:
            in_proj(j + 1)
        th_ref[...] = ub[:, ODD_CHUNK:2 * ODD_CHUNK] * ub[:, 2 * ODD_CHUNK:3 * ODD_CHUNK]
        first = pl.ds(HALO - 1, 1)
        last = pl.ds(HALO + tm, 1)
        th_ref[first, :] = jnp.where(t == 0, 0.0, th_ref[first, :])
        th_ref[last, :] = jnp.where(t == nt - 1, 0.0, th_ref[last, :])
        conv = (cw_ref[0:1, cols] * th_ref[HALO - 1:HALO - 1 + tm, :]
                + cw_ref[1:2, cols] * th_ref[main, :]
                + cw_ref[2:3, cols] * th_ref[HALO + 1:HALO + 1 + tm, :])
        mixed_ref[:, cols] = (_silu(ub[main, 3 * ODD_CHUNK:]) * ub[main, :ODD_CHUNK] * conv).astype(BF16)
    for r0 in range(0, tm, LN_ROWS):
        rs = slice(r0, r0 + LN_ROWS)
        acc = _dot(mixed_ref[rs, :], wout_ref[...])
        out_ref[0, rs, :] = _deepnorm(x_ref[rs, :], acc, lg_ref[...], lb_ref[...])


def _tail(o, za, yb, x, wmix, ng, lg0, lb0, win, cw, wout, lg1, lb1, tm):
    bn, ln, _ = x.shape
    nt = ln // tm
    ntiles = bn * nt
    rows = tm + 2 * HALO

    def tile_in(s):
        g = jnp.minimum(s, ntiles - 1)
        return g // nt, g % nt

    def tile_out(s):
        g = jnp.maximum(s - 1, 0)
        return g // nt, g % nt

    whole = lambda *shape: pl.BlockSpec(shape, lambda s: (0,) * len(shape), pipeline_mode=pl.Buffered(1))
    return pl.pallas_call(
        functools.partial(_tail_kernel, tm=tm, nt=nt),
        grid=(ntiles + 1,),
        in_specs=[
            pl.BlockSpec((1, 1, tm, GLA_WIDTH), lambda s: (0, *tile_in(s), 0)),
            pl.BlockSpec((1, 1, tm, GLA_WIDTH), lambda s: (1, *tile_in(s), 0)),
            pl.BlockSpec((1, tm, GLA_WIDTH), lambda s: (*tile_in(s), 0)),
            pl.BlockSpec((1, tm, SWA_WIDTH), lambda s: (*tile_in(s), 0)),
            pl.BlockSpec((1, tm, D_MODEL), lambda s: (*tile_in(s), 0)),
            whole(GLA_WIDTH + SWA_WIDTH, D_MODEL),
            whole(1, GLA_WIDTH), whole(1, D_MODEL), whole(1, D_MODEL),
            whole(D_MODEL, 4 * CONV_WIDTH), whole(3, CONV_WIDTH), whole(CONV_WIDTH, D_MODEL),
            whole(1, D_MODEL), whole(1, D_MODEL),
        ],
        out_specs=pl.BlockSpec((1, tm, D_MODEL), lambda s: (*tile_out(s), 0)),
        out_shape=jax.ShapeDtypeStruct((bn, ln, D_MODEL), F32),
        scratch_shapes=[
            pltpu.VMEM((2, tm, D_MODEL), F32),
            pltpu.VMEM((HALO, D_MODEL), F32),
            pltpu.VMEM((rows, D_MODEL), BF16),
            pltpu.VMEM((2, rows, 4 * ODD_CHUNK), F32),
            pltpu.VMEM((rows, ODD_CHUNK), F32),
            pltpu.VMEM((tm, CONV_WIDTH), BF16),
        ],
        compiler_params=_params("arbitrary"),
        name="even_tail_odd_layer",
    )(o, o, za, yb, x, wmix, ng, lg0, lb0, win, cw, wout, lg1, lb1)


PREP_ROWS = 256


def _prep_even_kernel(w_ref, o_ref):
    g0 = GLA_COLS
    qb0 = g0 + GD_COLS
    kb0 = qb0 + SWA_WIDTH
    vb0 = kb0 + SWA_KVW
    zb0 = vb0 + SWA_KVW
    o_ref[:, :GLA_WIDTH] = (w_ref[:, :GLA_WIDTH] * (GLA_DK ** -0.5)).astype(BF16)
    o_ref[:, GLA_WIDTH:GLA_COLS] = w_ref[:, GLA_WIDTH:GLA_COLS].astype(BF16)
    c = GLA_COLS
    o_ref[:, c:c + SWA_WIDTH] = (w_ref[:, qb0:qb0 + SWA_WIDTH] * (SWA_HD ** -0.5 * LOG2E)).astype(BF16)
    c += SWA_WIDTH
    o_ref[:, c:c + SWA_WIDTH] = w_ref[:, zb0:zb0 + SWA_WIDTH].astype(BF16)
    c += SWA_WIDTH
    o_ref[:, c:c + SWA_KVW] = w_ref[:, kb0:kb0 + SWA_KVW].astype(BF16)
    c += SWA_KVW
    o_ref[:, c:c + SWA_KVW] = w_ref[:, vb0:vb0 + SWA_KVW].astype(BF16)
    c += SWA_KVW
    lane = lax.broadcasted_iota(jnp.int32, (PREP_ROWS, GD_PAD), 1)
    o_ref[:, c:] = jnp.where(lane < GD_COLS, w_ref[:, g0:g0 + GD_PAD], 0.0).astype(BF16)


def _prep_even(w_in_layers, w_up_f, b_f, w_up_b, b_b, norm_g, w_out_layers):
    even_in = w_in_layers.shape[2]
    w = pl.pallas_call(
        _prep_even_kernel,
        grid=(D_MODEL // PREP_ROWS,),
        in_specs=[pl.BlockSpec((None, PREP_ROWS, even_in), lambda r: (0, r, 0))],
        out_specs=pl.BlockSpec((PREP_ROWS, EVEN_COLS), lambda r: (r, 0)),
        out_shape=jax.ShapeDtypeStruct((D_MODEL, EVEN_COLS), BF16),
        compiler_params=_params("parallel"),
        name="prep_even_weights",
    )(w_in_layers)
    zr = jnp.zeros_like(w_up_f)
    zpad = jnp.zeros((GD_PAD - GD_COLS - 1, GLA_WIDTH), w_up_f.dtype)
    wup = (jnp.stack([jnp.concatenate([w_up_f, zr, b_f[None], zpad], axis=0),
                      jnp.concatenate([zr, w_up_b, b_b[None], zpad], axis=0)]) * LOG2E).astype(BF16)
    ng = jnp.tile(norm_g.astype(F32), GLA_HEADS).reshape(1, GLA_WIDTH)
    return w, wup, ng, _layer0_bf16(w_out_layers, 1.0 / DN_ALPHA)


def _cast_kernel(w_ref, o_ref, *, scale):
    o_ref[...] = (w_ref[...] * scale).astype(BF16)


def _layer0_bf16(w_layers, scale=1.0):
    _, rows, cols = w_layers.shape
    return pl.pallas_call(
        functools.partial(_cast_kernel, scale=scale),
        grid=(rows // PREP_ROWS,),
        in_specs=[pl.BlockSpec((None, PREP_ROWS, cols), lambda r: (0, r, 0))],
        out_specs=pl.BlockSpec((PREP_ROWS, cols), lambda r: (r, 0)),
        out_shape=jax.ShapeDtypeStruct((rows, cols), BF16),
        compiler_params=_params("parallel"),
        name="cast_weights",
    )(w_layers)


def _prep_odd(w_in_layers, conv_w, w_out_layers):
    return _layer0_bf16(w_in_layers), conv_w.astype(F32), _layer0_bf16(w_out_layers, 1.0 / DN_ALPHA)


TILE_INPROJ = 1024
TILE_GLA = 2048
TILE_SWA_BLOCKS = 16
TILE_TAIL = 1024


def _trunk(x, even, odd, bias_tab, sink, ln_g, ln_b, consts,
           tm=TILE_TAIL, tm_in=TILE_INPROJ, gla_tile=TILE_GLA, swa_nq=TILE_SWA_BLOCKS):
    w, wup, ng, wmix = even
    win, cw, wout = odd
    tri, mask = consts
    qa, ka, va, za, qb, zb, kb, vb, gd = _inproj_even(x, w, tm_in)
    o = _gla(qa, ka, va, gd, wup, tri, mask, gla_tile)
    yb = _swa(qb, zb, kb, vb, bias_tab, sink, swa_nq)
    lg = ln_g.astype(F32).reshape(DEPTH, 1, D_MODEL)
    lb = ln_b.astype(F32).reshape(DEPTH, 1, D_MODEL)
    return _tail(o, za, yb, x, wmix, ng, lg[0], lb[0], win, cw, wout, lg[1], lb[1], tm)


def kernel(x_prompt, x_sample, w_in_even, gla_w_up_fwd, gla_b_fwd, gla_w_up_bwd, gla_b_bwd, gla_norm_g, swa_sink,
           rel_bias, w_out_even, w_in_odd, conv_w, w_out_odd, ln_g, ln_b):
    even = _prep_even(w_in_even, gla_w_up_fwd[0], gla_b_fwd[0], gla_w_up_bwd[0], gla_b_bwd[0], gla_norm_g[0],
                      w_out_even)
    odd = _prep_odd(w_in_odd, conv_w[0], w_out_odd)
    bias_tab = _bias_table(rel_bias)
    consts = _gla_constants()
    run = lambda x: _trunk(x, even, odd, bias_tab, swa_sink[0], ln_g, ln_b, consts)
    return (run(x_prompt), run(x_sample))
```
